```python
import jax, jax.numpy as jnp
from jax import lax
import numpy as np

D_MODEL = 1024
BATCH = 8
SEQ = 2048
DEPTH = 4
DEC_BATCH = 128
DEC_SEQ = 1
PAST_LEN = 16384
PAGE_SIZE = 128

HD_A = 64
H_A = D_MODEL // HD_A
D_A = H_A * HD_A
W_LORA = 64
A_LORA = 64
G_LORA = 128
A_COLS = 3 * D_A + W_LORA + A_LORA + G_LORA
RWKV_GN_EPS = 6.4e-4
HD_B = 128
H_B = D_MODEL // HD_B
D_B = H_B * HD_B
CONV_W = 4
GDN_CHUNK = 64
B_COLS = 4 * D_B + 2 * H_B
IN_COLS = A_COLS + B_COLS + 2 * D_MODEL
D_FF = 2816
N_SUB = 3
FFN_RES = 0.5
RMS_EPS = 1e-6

kernel_name = 'hybrid_rwkv7_gdn_macaron_decoder_step'


def split_cols(x, sizes):
    offsets = np.cumsum(sizes)[:-1].tolist()
    return jnp.split(x, offsets, axis=-1)


def rmsnorm(x, w):
    x32 = x.astype(jnp.float32)
    y = x32 * lax.rsqrt(jnp.mean(x32 * x32, axis=-1, keepdims=True) + RMS_EPS)
    return (y * w.astype(jnp.float32)).astype(x.dtype)


def l2norm(x, eps):
    return x * lax.rsqrt(jnp.sum(x * x, axis=-1, keepdims=True) + eps)


def swiglu(h, up, down):
    gate, val = jnp.split(h @ up, 2, axis=-1)
    return (jax.nn.silu(gate) * val) @ down


def rwkv7_step(S, inp):
    r, w, k, v, kk, b = inp
    sa = jnp.einsum('bhvk,bhk->bhv', S, -kk)
    S = S * w[:, :, None, :] + sa[..., None] * b[:, :, None, :] + v[..., None] * k[:, :, None, :]
    y = jnp.einsum('bhvk,bhk->bhv', S, r)
    return S, y


def rwkv7_branch(pa, shift0, wkv0, mu, w0, w2, a0, a2, g2, k_k, k_a, r_k, ln_w, ln_b):
    B, T, _ = pa.shape
    f32 = jnp.float32
    prev = jnp.concatenate([shift0[:, None, :].astype(pa.dtype), pa[:, :-1]], axis=1)
    xs = (pa + mu * (prev - pa)).astype(f32)
    r, k, v, wd, ad, gd = split_cols(xs, [D_A, D_A, D_A, W_LORA, A_LORA, G_LORA])
    w_log = -jax.nn.softplus(-(w0 + jnp.tanh(wd) @ w2)) - 0.5
    decay = jnp.exp(-jnp.exp(w_log))
    a = jax.nn.sigmoid(a0 + ad @ a2)
    g = jax.nn.sigmoid(gd) @ g2
    heads = lambda t: t.reshape(B, T, H_A, HD_A)
    kk = l2norm(heads(k * k_k), 1e-12)
    k = k * (1.0 + (a - 1.0) * k_a)
    r_h, k_h, v_h, w_h, a_h = heads(r), heads(k), heads(v), heads(decay), heads(a)
    tmaj = lambda t: jnp.moveaxis(t, 1, 0)
    S, y = lax.scan(rwkv7_step, wkv0.astype(f32),
                    (tmaj(r_h), tmaj(w_h), tmaj(k_h), tmaj(v_h), tmaj(kk), tmaj(kk * a_h)))
    y = jnp.moveaxis(y, 0, 1)
    mean = jnp.mean(y, axis=-1, keepdims=True)
    var = jnp.mean(jnp.square(y - mean), axis=-1, keepdims=True)
    y = ((y - mean) * lax.rsqrt(var + RWKV_GN_EPS)).reshape(B, T, D_A) * ln_w + ln_b
    bonus = jnp.sum(r_h * k_h * r_k, axis=-1, keepdims=True) * v_h
    out = (y + bonus.reshape(B, T, D_A)) * g
    return out, pa[:, -1], S


def gated_delta_chunked(q, k, v, g, beta, S0):
    B, T, H, _ = q.shape
    DV = v.shape[-1]
    C = GDN_CHUNK
    n = (T + C - 1) // C
    pad = n * C - T

    def blocks(t):
        t = jnp.pad(t, [(0, 0), (0, pad)] + [(0, 0)] * (t.ndim - 2))
        t = t.reshape((B, n, C) + t.shape[2:])
        return jnp.moveaxis(t, (1, 3), (0, 2))

    qb, kb, vb, gb, bb = blocks(q), blocks(k), blocks(v), blocks(g), blocks(beta)
    gc = jnp.cumsum(gb, axis=-1)
    idx = jnp.arange(C)
    incl = idx[:, None] >= idx[None, :]
    strict = idx[:, None] > idx[None, :]
    decay = jnp.exp(jnp.where(incl, gc[..., :, None] - gc[..., None, :], -jnp.inf))
    k_beta = kb * bb[..., None]
    v_beta = vb * bb[..., None]
    L = jnp.where(strict, jnp.einsum('nbhcd,nbhsd->nbhcs', k_beta, kb) * decay, 0.0)
    eye = jnp.eye(C, dtype=L.dtype)
    tinv = lax.linalg.triangular_solve(eye + L, jnp.broadcast_to(eye, L.shape),
                                       left_side=True, lower=True, unit_diagonal=True)
    u = jnp.einsum('nbhcs,nbhse->nbhce', tinv, v_beta)
    w = jnp.einsum('nbhcs,nbhsd->nbhcd', tinv, k_beta * jnp.exp(gc)[..., None])
    attn = jnp.einsum('nbhcd,nbhsd->nbhcs', qb, kb) * decay
    g_last = gc[..., -1]
    q_dec = qb * jnp.exp(gc)[..., None]
    k_dec = kb * jnp.exp(g_last[..., None] - gc)[..., None]

    def step(S, inp):
        q_c, k_c, u_c, w_c, a_c, gl_c = inp
        v_new = u_c - jnp.einsum('bhcd,bhde->bhce', w_c, S)
        o = jnp.einsum('bhcd,bhde->bhce', q_c, S) + jnp.einsum('bhcs,bhse->bhce', a_c, v_new)
        S = S * jnp.exp(gl_c)[..., None, None] + jnp.einsum('bhcd,bhce->bhde', k_c, v_new)
        return S, o

    S, o = lax.scan(step, S0, (q_dec, k_dec, u, w, attn, g_last))
    o = jnp.moveaxis(o, (0, 2), (1, 3)).reshape(B, n * C, H, DV)[:, :T]
    return o, S


def gdn_branch(pb, conv0, ssm0, conv_w, a_log, dt_bias, norm_w):
    B, T, _ = pb.shape
    f32 = jnp.float32
    qkv, a_in, b_in, z = split_cols(pb, [3 * D_B, H_B, H_B, D_B])
    xc = jnp.concatenate([conv0.astype(qkv.dtype), qkv], axis=1)
    conv = sum(xc[:, j:j + T] * conv_w[j] for j in range(CONV_W))
    qkv_c = jax.nn.silu(conv).astype(f32)
    q, k, v = split_cols(qkv_c, [D_B, D_B, D_B])
    q = l2norm(q.reshape(B, T, H_B, HD_B), 1e-6) * (HD_B ** -0.5)
    k = l2norm(k.reshape(B, T, H_B, HD_B), 1e-6)
    v = v.reshape(B, T, H_B, HD_B)
    beta = jax.nn.sigmoid(b_in.astype(f32))
    g = -jnp.exp(a_log.astype(f32)) * jax.nn.softplus(a_in.astype(f32) + dt_bias)
    o, S = gated_delta_chunked(q, k, v, g, beta, ssm0.astype(f32))
    o = o * lax.rsqrt(jnp.mean(o * o, axis=-1, keepdims=True) + RMS_EPS) * norm_w
    o = o * jax.nn.silu(z.astype(f32).reshape(B, T, H_B, HD_B))
    return o.reshape(B, T, D_B), xc[:, T:], S


def decoder_layer(x, c, shift0, wkv0, conv0, ssm0, w_ada, b_ada, norm_pre, norm_post,
                  ffn_up, ffn_down, w_in, w_out, rwkv_mu, rwkv_w0, rwkv_w2, rwkv_a0, rwkv_a2,
                  rwkv_g2, rwkv_k_k, rwkv_k_a, rwkv_r_k, rwkv_ln_w, rwkv_ln_b,
                  gdn_conv, gdn_a_log, gdn_dt_bias, gdn_norm_w):
    B = x.shape[0]
    mod = (jax.nn.silu(c) @ w_ada + b_ada).reshape(B, N_SUB, 3, D_MODEL)
    m_shift, m_scale, m_gate = mod[:, :, 0], mod[:, :, 1], mod[:, :, 2]

    def sub_in(t, i):
        return rmsnorm(t, norm_pre[i]) * (1.0 + m_scale[:, i, None]) + m_shift[:, i, None]

    def sub_out(t, y, i, res):
        return t + res * m_gate[:, i, None] * rmsnorm(y, norm_post[i])

    x = sub_out(x, swiglu(sub_in(x, 0), ffn_up[0], ffn_down[0]), 0, FFN_RES)
    h = sub_in(x, 1)
    pa, pb, pg = split_cols(h @ w_in, [A_COLS, B_COLS, 2 * D_MODEL])
    ya, shift_new, wkv_new = rwkv7_branch(pa, shift0, wkv0, rwkv_mu, rwkv_w0, rwkv_w2, rwkv_a0,
                                          rwkv_a2, rwkv_g2, rwkv_k_k, rwkv_k_a, rwkv_r_k,
                                          rwkv_ln_w, rwkv_ln_b)
    yb, conv_new, ssm_new = gdn_branch(pb, conv0, ssm0, gdn_conv, gdn_a_log, gdn_dt_bias, gdn_norm_w)
    g_a, g_b = jnp.split(jax.nn.sigmoid(pg), 2, axis=-1)
    merged = (g_a * ya.astype(x.dtype) + g_b * yb.astype(x.dtype)) @ w_out
    x = sub_out(x, merged, 1, 1.0)
    x = sub_out(x, swiglu(sub_in(x, 2), ffn_up[1], ffn_down[1]), 2, FFN_RES)
    return x, shift_new, wkv_new, conv_new, ssm_new


def setup_inputs(seed: int = 0) -> dict:
    key = jax.random.key(seed)
    ks = jax.random.split(key, 40)
    f32 = jnp.float32

    def nrm(k, shape, s):
        return jax.random.normal(k, shape, f32) * s

    def uni(k, shape, lo, hi):
        return jax.random.uniform(k, shape, f32, minval=lo, maxval=hi)

    return {
        'x_prompt': nrm(ks[0], (BATCH, SEQ, D_MODEL), 1.0),
        'x_sample': nrm(ks[1], (DEC_BATCH, DEC_SEQ, D_MODEL), 1.0),
        'c_prompt': nrm(ks[2], (BATCH, D_MODEL), 1.0),
        'c_sample': nrm(ks[3], (DEC_BATCH, D_MODEL), 1.0),
        'state_rwkv_shift': nrm(ks[4], (DEPTH, DEC_BATCH, A_COLS), 1.0),
        'state_rwkv_wkv': nrm(ks[5], (DEPTH, DEC_BATCH, H_A, HD_A, HD_A), 0.1),
        'state_gdn_conv': nrm(ks[6], (DEPTH, DEC_BATCH, CONV_W - 1, 3 * D_B), 0.5),
        'state_gdn_ssm': nrm(ks[7], (DEPTH, DEC_BATCH, H_B, HD_B, HD_B), 0.05),
        'w_ada': nrm(ks[8], (DEPTH, D_MODEL, 3 * N_SUB * D_MODEL), 0.5 * D_MODEL ** -0.5),
        'b_ada': nrm(ks[9], (DEPTH, 3 * N_SUB * D_MODEL), 0.02),
        'norm_pre': 1.0 + nrm(ks[10], (DEPTH, N_SUB, D_MODEL), 0.1),
        'norm_post': 1.0 + nrm(ks[11], (DEPTH, N_SUB, D_MODEL), 0.1),
        'ffn_up': nrm(ks[12], (DEPTH, 2, D_MODEL, 2 * D_FF), D_MODEL ** -0.5),
        'ffn_down': nrm(ks[13], (DEPTH, 2, D_FF, D_MODEL), D_FF ** -0.5),
        'w_in': nrm(ks[14], (DEPTH, D_MODEL, IN_COLS), D_MODEL ** -0.5),
        'w_out': nrm(ks[15], (DEPTH, D_MODEL, D_MODEL), D_MODEL ** -0.5),
        'rwkv_mu': uni(ks[16], (DEPTH, A_COLS), 0.0, 1.0),
        'rwkv_w0': uni(ks[17], (DEPTH, D_A), -6.0, -1.0),
        'rwkv_w2': nrm(ks[18], (DEPTH, W_LORA, D_A), 0.5 * W_LORA ** -0.5),
        'rwkv_a0': nrm(ks[19], (DEPTH, D_A), 0.1),
        'rwkv_a2': nrm(ks[20], (DEPTH, A_LORA, D_A), A_LORA ** -0.5),
        'rwkv_g2': nrm(ks[21], (DEPTH, G_LORA, D_A), G_LORA ** -0.5),
        'rwkv_k_k': 0.85 + nrm(ks[22], (DEPTH, D_A), 0.05),
        'rwkv_k_a': 1.0 + nrm(ks[23], (DEPTH, D_A), 0.05),
        'rwkv_r_k': nrm(ks[24], (DEPTH, H_A, HD_A), 0.1),
        'rwkv_ln_w': 1.0 + nrm(ks[25], (DEPTH, D_A), 0.1),
        'rwkv_ln_b': nrm(ks[26], (DEPTH, D_A), 0.02),
        'gdn_conv': nrm(ks[27], (DEPTH, CONV_W, 3 * D_B), CONV_W ** -0.5),
        'gdn_a_log': jnp.log(uni(ks[28], (DEPTH, H_B), 1.0, 16.0)),
        'gdn_dt_bias': jnp.log(jnp.expm1(uni(ks[29], (DEPTH, H_B), 0.001, 0.1))),
        'gdn_norm_w': 1.0 + nrm(ks[30], (DEPTH, HD_B), 0.1),
    }


def reference(x_prompt, x_sample, c_prompt, c_sample, state_rwkv_shift, state_rwkv_wkv,
              state_gdn_conv, state_gdn_ssm, w_ada, b_ada, norm_pre, norm_post, ffn_up, ffn_down,
              w_in, w_out, rwkv_mu, rwkv_w0, rwkv_w2, rwkv_a0, rwkv_a2, rwkv_g2, rwkv_k_k,
              rwkv_k_a, rwkv_r_k, rwkv_ln_w, rwkv_ln_b, gdn_conv, gdn_a_log, gdn_dt_bias,
              gdn_norm_w):
    bp = x_prompt.shape[0]
    dt = x_prompt.dtype
    p_shift0 = jnp.zeros((bp, A_COLS), dt)
    p_wkv0 = jnp.zeros((bp, H_A, HD_A, HD_A), jnp.float32)
    p_conv0 = jnp.zeros((bp, CONV_W - 1, 3 * D_B), dt)
    p_ssm0 = jnp.zeros((bp, H_B, HD_B, HD_B), jnp.float32)
    yp, ys = x_prompt, x_sample
    ps_l, pw_l, pc_l, pm_l = [], [], [], []
    ss_l, sw_l, sc_l, sm_l = [], [], [], []
    for l in range(DEPTH):
        lp = (w_ada[l], b_ada[l], norm_pre[l], norm_post[l], ffn_up[l], ffn_down[l], w_in[l],
              w_out[l], rwkv_mu[l], rwkv_w0[l], rwkv_w2[l], rwkv_a0[l], rwkv_a2[l], rwkv_g2[l],
              rwkv_k_k[l], rwkv_k_a[l], rwkv_r_k[l], rwkv_ln_w[l], rwkv_ln_b[l], gdn_conv[l],
              gdn_a_log[l], gdn_dt_bias[l], gdn_norm_w[l])
        yp, ps, pw, pc, pm = decoder_layer(yp, c_prompt, p_shift0, p_wkv0, p_conv0, p_ssm0, *lp)
        ys, ss, sw, sc, sm = decoder_layer(ys, c_sample, state_rwkv_shift[l], state_rwkv_wkv[l],
                                           state_gdn_conv[l], state_gdn_ssm[l], *lp)
        ps_l.append(ps); pw_l.append(pw); pc_l.append(pc); pm_l.append(pm)
        ss_l.append(ss); sw_l.append(sw); sc_l.append(sc); sm_l.append(sm)
    return (yp, ys, jnp.stack(ps_l), jnp.stack(pw_l), jnp.stack(pc_l), jnp.stack(pm_l),
            jnp.stack(ss_l), jnp.stack(sw_l), jnp.stack(sc_l), jnp.stack(sm_l))
```

```python
import functools

import jax
import jax.numpy as jnp
from jax import lax
from jax.experimental import pallas as pl
from jax.experimental.pallas import tpu as pltpu

F32 = jnp.float32
BF16 = jnp.bfloat16

D_MODEL = 1024
HD_A = 64
H_A = D_MODEL // HD_A
W_LORA = 64
A_LORA = 64
G_LORA = 128
A_COLS = 3 * D_MODEL + W_LORA + A_LORA + G_LORA
RWKV_GN_EPS = 6.4e-4
HD_B = 128
H_B = D_MODEL // HD_B
CONV_W = 4
D_FF = 2816
N_SUB = 3
FFN_RES = 0.5
RMS_EPS = 1e-6

LANES = 128
N_PAIR = D_MODEL // LANES
RW_C = 64
GD_C = 128
AB_COLS = LANES
IN_COLS_R = A_COLS + 3 * D_MODEL + D_MODEL + 2 * D_MODEL + AB_COLS
VMEM_LIMIT = 56 * 1024 * 1024
REC_PASSES = 3

_NN = (((1,), (0,)), ((), ()))
_NT = (((1,), (1,)), ((), ()))


def _dg(a, b, dn=_NN):
    return lax.dot_general(a, b, dn, preferred_element_type=F32)


def _split2(x):
    hi = x.astype(BF16)
    lo = (x - hi.astype(F32)).astype(BF16)
    return hi, lo


def _split3(x):
    h1 = x.astype(BF16)
    r1 = x - h1.astype(F32)
    h2 = r1.astype(BF16)
    h3 = (r1 - h2.astype(F32)).astype(BF16)
    return h1, h2, h3


def _mm(a, b, dn=_NN, passes=REC_PASSES):
    if passes == 1:
        return _dg(a.astype(BF16), b.astype(BF16), dn)
    ah, al = _split2(a)
    bh, bl = _split2(b)
    return _dg(ah, bh, dn) + (_dg(ah, bl, dn) + _dg(al, bh, dn))


def _mm_sel_rhs(x, e, pieces):
    parts = _split3(x) if pieces == 3 else _split2(x)
    out = _dg(parts[0], e)
    for p in parts[1:]:
        out = out + _dg(p, e)
    return out


def _mm_sel_lhs(e, x, pieces):
    parts = _split3(x) if pieces == 3 else _split2(x)
    out = _dg(e, parts[0])
    for p in parts[1:]:
        out = out + _dg(e, p)
    return out


def _iota2(shape, dim):
    return lax.broadcasted_iota(jnp.int32, shape, dim)


def _softplus(x):
    return jnp.maximum(x, 0.0) + jnp.log(1.0 + jnp.exp(-jnp.abs(x)))


def _silu(x):
    return x * jax.nn.sigmoid(x)


def _rms(x, w):
    return x * lax.rsqrt(jnp.mean(x * x, axis=-1, keepdims=True) + RMS_EPS) * w


def _mod_in(x, npre, mod_ref, sub):
    shift = mod_ref[3 * sub, 0]
    scale = mod_ref[3 * sub + 1, 0]
    return _rms(x, npre) * (1.0 + scale) + shift


def _tri_masks(n, top):
    i = _iota2((n, n), 0)
    j = _iota2((n, n), 1)
    masks = []
    s = 1
    while s <= top:
        sh = s.bit_length() - 1
        same = (i >> (sh + 1)) == (j >> (sh + 1))
        masks.append(same & (((i >> sh) & 1) == 1) & (((j >> sh) & 1) == 0))
        s *= 2
    return masks


def _tri_inv(L, masks):
    n = L.shape[0]
    eye = jnp.where(_iota2((n, n), 0) == _iota2((n, n), 1), 1.0, 0.0).astype(F32)
    X = eye - jnp.where(masks[0], L, 0.0)
    for m in masks[1:]:
        N = jnp.where(m, L, 0.0)
        X = X - _mm(_mm(X, N), X)
    return X


def _to_tiles(x):
    return jnp.stack([x[:, LANES * p:LANES * (p + 1)] for p in range(N_PAIR)], axis=0)


def _pair_headsum(x3, ebd):
    p, n, l = x3.shape
    return _mm_sel_rhs(x3.reshape(p * n, l), ebd, 2).reshape(p, n, l)


def _ebd():
    i = _iota2((LANES, LANES), 0)
    j = _iota2((LANES, LANES), 1)
    return jnp.where((i >> 6) == (j >> 6), 1.0, 0.0).astype(BF16)


def _rwkv_prep(pa, prev, mu, w0, w2p, a0, a2p, g2, k_k, k_a):
    xs = pa + mu * (prev - pa)
    r = xs[:, 0:D_MODEL]
    k = xs[:, D_MODEL:2 * D_MODEL]
    v = xs[:, 2 * D_MODEL:3 * D_MODEL]
    wa = xs[:, 3 * D_MODEL:3 * D_MODEL + LANES]
    gd = xs[:, 3 * D_MODEL + LANES:A_COLS]
    w_lin = _mm(jnp.tanh(wa), w2p, passes=3)
    a_lin = _mm(wa, a2p, passes=3)
    g = _mm(jax.nn.sigmoid(gd), g2, passes=3)
    w_log = -_softplus(-(w0 + w_lin)) - 0.5
    lw = -jnp.exp(w_log)
    a = jax.nn.sigmoid(a0 + a_lin)
    kkr = k * k_k
    km = k * (1.0 + (a - 1.0) * k_a)
    return r, km, v, kkr, a, lw, g


def _rwkv_epilogue(y3, r3, km3, v3, rk3, lnw3, lnb3, ebd):
    mean = _pair_headsum(y3, ebd) * (1.0 / HD_A)
    yc = y3 - mean
    var = _pair_headsum(yc * yc, ebd) * (1.0 / HD_A)
    yn = yc * lax.rsqrt(var + RWKV_GN_EPS) * lnw3 + lnb3
    bonus = _pair_headsum(r3 * km3 * rk3, ebd) * v3
    return yn + bonus


def _gdn_gates(ab, alog, dtb):
    g_all = -jnp.exp(alog) * _softplus(ab + dtb)
    beta_all = jax.nn.sigmoid(ab)
    return g_all, beta_all


def _sel_matrix(offset):
    i = _iota2((LANES, D_MODEL), 0)
    j = _iota2((LANES, D_MODEL), 1)
    return jnp.where(i == (j >> 7) + offset, 1.0, 0.0).astype(BF16)


def _head_l2norm(x, eps):
    outs = []
    for h in range(H_B):
        xh = x[:, LANES * h:LANES * (h + 1)]
        outs.append(xh * lax.rsqrt(jnp.sum(xh * xh, axis=-1, keepdims=True) + eps))
    return outs


def _ada_kernel(c_ref, w_ref, b_ref, o_ref):
    s = _silu(c_ref[...]).astype(BF16)
    o_ref[...] = _dg(s, w_ref[...].astype(BF16)) + b_ref[...]


def _ffn_kernel(x_ref, mod_ref, npre_ref, npost_ref, up_ref, down_ref, o_ref, *, sub):
    x = x_ref[0]
    h = _mod_in(x, npre_ref[...], mod_ref, sub).astype(BF16)
    gate = _dg(h, up_ref[:, 0:D_FF])
    val = _dg(h, up_ref[:, D_FF:2 * D_FF])
    act = (_silu(gate) * val).astype(BF16)
    y = _dg(act, down_ref[...])
    o_ref[0] = x + FFN_RES * mod_ref[3 * sub + 2, 0] * _rms(y, npost_ref[...])


def _inproj_kernel(x_ref, mod_ref, npre_ref, w_ref, pa_ref, qkv_ref, z_ref, pg_ref, ab_ref):
    h = _mod_in(x_ref[0], npre_ref[...], mod_ref, 1).astype(BF16)
    o = 0
    for ref in (pa_ref, qkv_ref, z_ref, pg_ref, ab_ref):
        n = ref.shape[-1]
        ref[0] = _dg(h, w_ref[:, o:o + n])
        o += n


def _outproj_kernel(x_ref, ya_ref, yb_ref, pg_ref, mod_ref, npost_ref, w_ref, o_ref):
    pg = pg_ref[0]
    m = jax.nn.sigmoid(pg[:, 0:D_MODEL]) * ya_ref[0] + jax.nn.sigmoid(pg[:, D_MODEL:]) * yb_ref[0]
    y = _dg(m.astype(BF16), w_ref[...])
    o_ref[0] = x_ref[0] + mod_ref[3 * 1 + 2, 0] * _rms(y, npost_ref[...])


def _rwkv_prompt_kernel(pa_ref, mu_ref, w0_ref, w2_ref, a0_ref, a2_ref, g2_ref, kk_ref, ka_ref,
                        rk3_ref, lnw3_ref, lnb3_ref, ya_ref, s_ref, shift_ref, carry_ref, sbd_ref):
    C = RW_C
    t = pl.program_id(1)

    @pl.when(t == 0)
    def _():
        carry_ref[...] = jnp.zeros_like(carry_ref)
        sbd_ref[...] = jnp.zeros_like(sbd_ref)

    pa = pa_ref[0]
    prev = jnp.where(_iota2((C, 1), 0) == 0, carry_ref[...], pltpu.roll(pa, 1, 0))
    carry_ref[...] = pa[C - 1:C]
    r, km, v, kkr, a, lw, g = _rwkv_prep(pa, prev, mu_ref[...], w0_ref[...], w2_ref[...], a0_ref[...],
                                         a2_ref[...], g2_ref[...], kk_ref[...], ka_ref[...])

    tri = jnp.where(_iota2((C, C), 0) >= _iota2((C, C), 1), 1.0, 0.0).astype(BF16)
    cum = _mm_sel_lhs(tri, lw, 3)
    ebd = _ebd()
    r3, km3, v3, kkr3, a3, lw3, cum3 = (_to_tiles(x) for x in (r, km, v, kkr, a, lw, cum))
    kk3 = kkr3 * lax.rsqrt(_pair_headsum(kkr3 * kkr3, ebd) + 1e-12)
    b3 = kk3 * a3
    cl = cum3[:, C - 1:C, :]
    e_neg = jnp.exp(-cum3)
    e_end = jnp.exp(cl - cum3)
    rh3 = r3 * jnp.exp(cum3)
    ah3 = kk3 * jnp.exp(cum3 - lw3)
    bt3 = b3 * e_neg
    kt3 = km3 * e_neg
    be3 = b3 * e_end
    ke3 = km3 * e_end
    e_last3 = jnp.exp(cl)

    lane0 = _iota2((1, LANES), 1) < HD_A

    def sm(x):
        return jnp.concatenate([jnp.where(lane0, x, 0.0), jnp.where(lane0, 0.0, x)], axis=0)

    ii = _iota2((2 * C, 2 * C), 0)
    jj = _iota2((2 * C, 2 * C), 1)
    strict = jj < ii
    incl = jj <= ii
    masks = _tri_masks(2 * C, C // 2)

    ys = []
    for p in range(N_PAIR):
        As, Rs, Bs, Ks, Vs, Bes, Kes = (sm(x[p]) for x in (ah3, rh3, bt3, kt3, v3, be3, ke3))
        S = sbd_ref[p]
        Lab = jnp.where(strict, _mm(As, Bs, _NT), 0.0)
        Lak = jnp.where(strict, _mm(As, Ks, _NT), 0.0)
        Arb = jnp.where(incl, _mm(Rs, Bs, _NT), 0.0)
        Ark = jnp.where(incl, _mm(Rs, Ks, _NT), 0.0)
        M = _mm(As, S, _NT) + _mm(Lak, Vs)
        U = -_mm(_tri_inv(Lab, masks), M)
        Y = _mm(Rs, S, _NT) + _mm(Arb, U) + _mm(Ark, Vs)
        sbd_ref[p] = S * e_last3[p] + _mm(U.T, Bes) + _mm(Vs.T, Kes)
        ys.append(Y[0:C] + Y[C:2 * C])
    y3 = jnp.stack(ys, axis=0)

    out3 = _rwkv_epilogue(y3, r3, km3, v3, rk3_ref[...], lnw3_ref[...], lnb3_ref[...], ebd)
    for p in range(N_PAIR):
        ya_ref[0, :, LANES * p:LANES * (p + 1)] = out3[p] * g[:, LANES * p:LANES * (p + 1)]

    @pl.when(t == pl.num_programs(1) - 1)
    def _():
        shift_ref[0] = pa[C - 1:C]
        for p in range(N_PAIR):
            S = sbd_ref[p]
            s_ref[0, 2 * p] = S[0:HD_A, 0:HD_A]
            s_ref[0, 2 * p + 1] = pltpu.roll(S, HD_A, 1)[HD_A:LANES, 0:HD_A]


def _gdn_prompt_kernel(qkv_ref, ab_ref, z_ref, cw_ref, alog_ref, dtb_ref, nw_ref,
                       yb_ref, conv_ref, ssm_ref, carry_ref):
    C = GD_C
    t = pl.program_id(1)

    @pl.when(t == 0)
    def _():
        carry_ref[...] = jnp.zeros_like(carry_ref)
        ssm_ref[...] = jnp.zeros_like(ssm_ref)

    x = qkv_ref[0]
    ext = jnp.concatenate([carry_ref[...], x], axis=0)
    conv = x * cw_ref[CONV_W - 1:CONV_W, :]
    for s in range(1, CONV_W):
        conv = conv + pltpu.roll(ext, s, 0)[8:] * cw_ref[CONV_W - 1 - s:CONV_W - s, :]
    carry_ref[...] = x[C - 8:C]
    qkvc = _silu(conv)
    qs = _head_l2norm(qkvc[:, 0:D_MODEL], 1e-6)
    ks = _head_l2norm(qkvc[:, D_MODEL:2 * D_MODEL], 1e-6)
    vv = qkvc[:, 2 * D_MODEL:3 * D_MODEL]

    g_all, beta_all = _gdn_gates(ab_ref[0], alog_ref[...], dtb_ref[...])
    tri = jnp.where(_iota2((C, C), 0) >= _iota2((C, C), 1), 1.0, 0.0).astype(BF16)
    gc_all = _mm_sel_lhs(tri, g_all, 3)
    gexp = _mm_sel_rhs(gc_all, _sel_matrix(0), 3)
    bexp = _mm_sel_rhs(beta_all, _sel_matrix(H_B), 3)

    ii = _iota2((C, C), 0)
    jj = _iota2((C, C), 1)
    strict = jj < ii
    incl = jj <= ii
    masks = _tri_masks(C, C // 2)
    z = z_ref[0]

    for h in range(H_B):
        sl = slice(LANES * h, LANES * (h + 1))
        qh = qs[h] * (HD_B ** -0.5)
        kh = ks[h]
        vh = vv[:, sl]
        gh = gexp[:, sl]
        bh = bexp[:, sl]
        dec = jnp.exp(gh - gh.T)
        eg = jnp.exp(gh)
        glast = gh[C - 1:C, :]
        kb = kh * bh
        L = jnp.where(strict, _mm(kb, kh, _NT) * dec, 0.0)
        Tinv = _tri_inv(L, masks)
        u = _mm(Tinv, vh * bh)
        w = _mm(Tinv, kb * eg)
        attn = jnp.where(incl, _mm(qh, kh, _NT) * dec, 0.0)
        S = ssm_ref[0, h]
        v_new = u - _mm(w, S)
        o = _mm(qh * eg, S) + _mm(attn, v_new)
        k_dec = kh * jnp.exp(glast - gh)
        ssm_ref[0, h] = S * jnp.exp(glast) + _mm(k_dec.T, v_new)
        on = o * lax.rsqrt(jnp.mean(o * o, axis=-1, keepdims=True) + RMS_EPS) * nw_ref[...]
        yb_ref[0, :, sl] = on * _silu(z[:, sl])

    @pl.when(t == pl.num_programs(1) - 1)
    def _():
        conv_ref[0] = pltpu.roll(x, CONV_W - 1, 0)[0:CONV_W - 1]


def _rwkv_sample_prep_kernel(pa_ref, prev_ref, mu_ref, w0_ref, w2_ref, a0_ref, a2_ref, g2_ref, kk_ref,
                             ka_ref, r_ref, w_ref, km_ref, v_ref, kkn_ref, b_ref, g_ref):
    r, km, v, kkr, a, lw, g = _rwkv_prep(pa_ref[...], prev_ref[...], mu_ref[...], w0_ref[...], w2_ref[...],
                                         a0_ref[...], a2_ref[...], g2_ref[...], kk_ref[...], ka_ref[...])
    ebd = _ebd()
    kkr3 = _to_tiles(kkr)
    kk3 = kkr3 * lax.rsqrt(_pair_headsum(kkr3 * kkr3, ebd) + 1e-12)
    b3 = kk3 * _to_tiles(a)
    for p in range(N_PAIR):
        kkn_ref[:, LANES * p:LANES * (p + 1)] = kk3[p]
        b_ref[:, LANES * p:LANES * (p + 1)] = b3[p]
    r_ref[...] = r
    w_ref[...] = jnp.exp(lw)
    km_ref[...] = km
    v_ref[...] = v
    g_ref[...] = g


def _rwkv_sample_state_kernel(r_ref, w_ref, km_ref, kkn_ref, b_ref, vt_ref, s0_ref, s_ref, yt_ref, *, bb):
    lane = _iota2((HD_A, H_A), 1)

    def body(i, carry):
        rb, wb, kb, kkb, bbv = r_ref[i], w_ref[i], km_ref[i], kkn_ref[i], b_ref[i]
        vt = vt_ref[i]
        yt = jnp.zeros((HD_A, H_A), F32)
        for h in range(H_A):
            S = s0_ref[i, h]
            sa = -jnp.sum(S * kkb[h:h + 1, :], axis=1, keepdims=True)
            Sn = S * wb[h:h + 1, :] + sa * bbv[h:h + 1, :] + vt[:, h:h + 1] * kb[h:h + 1, :]
            s_ref[i, h] = Sn
            yt = jnp.where(lane == h, jnp.sum(Sn * rb[h:h + 1, :], axis=1, keepdims=True), yt)
        yt_ref[i] = yt
        return carry

    lax.fori_loop(0, bb, body, 0)


def _rwkv_sample_post_kernel(y_ref, r_ref, km_ref, v_ref, g_ref, rk3_ref, lnw3_ref, lnb3_ref, ya_ref):
    ebd = _ebd()
    y3, r3, km3, v3 = (_to_tiles(x[...]) for x in (y_ref, r_ref, km_ref, v_ref))
    out3 = _rwkv_epilogue(y3, r3, km3, v3, rk3_ref[...], lnw3_ref[...], lnb3_ref[...], ebd)
    g = g_ref[...]
    for p in range(N_PAIR):
        ya_ref[:, LANES * p:LANES * (p + 1)] = out3[p] * g[:, LANES * p:LANES * (p + 1)]


def _gdn_sample_prep_kernel(qkv_ref, c0_ref, ab_ref, cw_ref, alog_ref, dtb_ref,
                            q_ref, k_ref, v_ref, eg_ref, beta_ref, cnew_ref):
    x = qkv_ref[...]
    conv = x * cw_ref[CONV_W - 1:CONV_W, :]
    for j in range(CONV_W - 1):
        conv = conv + c0_ref[j] * cw_ref[j:j + 1, :]
    for j in range(CONV_W - 2):
        cnew_ref[j] = c0_ref[j + 1]
    cnew_ref[CONV_W - 2] = x
    qkvc = _silu(conv)
    qs = _head_l2norm(qkvc[:, 0:D_MODEL], 1e-6)
    ks = _head_l2norm(qkvc[:, D_MODEL:2 * D_MODEL], 1e-6)
    for h in range(H_B):
        q_ref[:, LANES * h:LANES * (h + 1)] = qs[h] * (HD_B ** -0.5)
        k_ref[:, LANES * h:LANES * (h + 1)] = ks[h]
    v_ref[...] = qkvc[:, 2 * D_MODEL:3 * D_MODEL]
    g_all, beta_all = _gdn_gates(ab_ref[...], alog_ref[...], dtb_ref[...])
    eg_ref[...] = jnp.exp(_mm_sel_rhs(g_all, _sel_matrix(0), 3))
    beta_ref[...] = _mm_sel_rhs(beta_all, _sel_matrix(H_B), 3)


def _gdn_sample_state_kernel(qt_ref, kt_ref, v_ref, eg_ref, beta_ref, s0_ref, s_ref, o_ref, *, bb):
    def body(i, carry):
        qt, kt = qt_ref[i], kt_ref[i]
        vb, egb, btb = v_ref[i], eg_ref[i], beta_ref[i]
        for h in range(H_B):
            Sd = s0_ref[i, h] * egb[h:h + 1, :]
            kcol = kt[:, h:h + 1]
            ks_row = jnp.sum(Sd * kcol, axis=0, keepdims=True)
            v_new = btb[h:h + 1, :] * (vb[h:h + 1, :] - ks_row)
            Sn = Sd + kcol * v_new
            s_ref[i, h] = Sn
            o_ref[i, h:h + 1, :] = jnp.sum(Sn * qt[:, h:h + 1], axis=0, keepdims=True)
        return carry

    lax.fori_loop(0, bb, body, 0)


def _gdn_sample_post_kernel(o_ref, z_ref, nw_ref, yb_ref):
    o = o_ref[...]
    z = z_ref[...]
    for h in range(H_B):
        sl = slice(LANES * h, LANES * (h + 1))
        oh = o[:, sl]
        on = oh * lax.rsqrt(jnp.mean(oh * oh, axis=-1, keepdims=True) + RMS_EPS) * nw_ref[...]
        yb_ref[:, sl] = on * _silu(z[:, sl])


def _params(sem):
    return pltpu.CompilerParams(dimension_semantics=sem, vmem_limit_bytes=VMEM_LIMIT)


def _full(shape):
    return pl.BlockSpec(shape, lambda *_: (0,) * len(shape))


_RESIDENT = pl.BlockSpec(memory_space=pltpu.VMEM)


def _row_tile(t, cap):
    tm = min(t, cap)
    assert t % tm == 0
    return tm


def _mod_spec(mod):
    rows = mod.shape[2]
    if rows == 1:
        return pl.BlockSpec((3 * N_SUB, 1, 1, D_MODEL), lambda b, i: (0, b, 0, 0))
    return pl.BlockSpec((3 * N_SUB, 1, rows, D_MODEL), lambda b, i: (0, 0, i, 0))


def _ada_call(c_all, w_ada, b_ada):
    depth = w_ada.shape[0]
    n = c_all.shape[0]
    return pl.pallas_call(
        _ada_kernel,
        grid=(depth, 3 * N_SUB),
        in_specs=[pl.BlockSpec((n, D_MODEL), lambda l, j: (0, 0)),
                  pl.BlockSpec((None, D_MODEL, D_MODEL), lambda l, j: (l, 0, j)),
                  pl.BlockSpec((None, 1, D_MODEL), lambda l, j: (l, 0, j))],
        out_specs=pl.BlockSpec((None, None, n, D_MODEL), lambda l, j: (l, j, 0, 0)),
        out_shape=jax.ShapeDtypeStruct((depth, 3 * N_SUB, n, D_MODEL), F32),
        compiler_params=_params(("arbitrary", "arbitrary")),
        name="ada_mod",
    )(c_all, w_ada, b_ada.reshape(depth, 1, 3 * N_SUB * D_MODEL))


def _ffn_call(x, mod, npre, npost, up, down, sub):
    b, t, _ = x.shape
    tm = _row_tile(t, 512)
    xspec = pl.BlockSpec((1, tm, D_MODEL), lambda b, i: (b, i, 0))
    return pl.pallas_call(
        functools.partial(_ffn_kernel, sub=sub),
        grid=(b, t // tm),
        in_specs=[xspec, _mod_spec(mod), _full((1, D_MODEL)), _full((1, D_MODEL)), _RESIDENT, _RESIDENT],
        out_specs=xspec,
        out_shape=jax.ShapeDtypeStruct(x.shape, F32),
        compiler_params=_params(("arbitrary", "arbitrary")),
        name="ffn",
    )(x, mod, npre, npost, up, down)


def _inproj_call(x, mod, npre, w):
    b, t, _ = x.shape
    tm = _row_tile(t, 256)
    widths = (A_COLS, 3 * D_MODEL, D_MODEL, 2 * D_MODEL, AB_COLS)
    spec = lambda n: pl.BlockSpec((1, tm, n), lambda b, i: (b, i, 0))
    return pl.pallas_call(
        _inproj_kernel,
        grid=(b, t // tm),
        in_specs=[spec(D_MODEL), _mod_spec(mod), _full((1, D_MODEL)), _RESIDENT],
        out_specs=[spec(n) for n in widths],
        out_shape=[jax.ShapeDtypeStruct((b, t, n), F32) for n in widths],
        compiler_params=_params(("arbitrary", "arbitrary")),
        name="in_proj",
    )(x, mod, npre, w)


def _outproj_call(x, ya, yb, pg, mod, npost, w):
    b, t, _ = x.shape
    tm = _row_tile(t, 512)
    spec = lambda n: pl.BlockSpec((1, tm, n), lambda b, i: (b, i, 0))
    return pl.pallas_call(
        _outproj_kernel,
        grid=(b, t // tm),
        in_specs=[spec(D_MODEL), spec(D_MODEL), spec(D_MODEL), spec(2 * D_MODEL), _mod_spec(mod),
                  _full((1, D_MODEL)), _RESIDENT],
        out_specs=spec(D_MODEL),
        out_shape=jax.ShapeDtypeStruct(x.shape, F32),
        compiler_params=_params(("arbitrary", "arbitrary")),
        name="out_proj",
    )(x, ya, yb, pg, mod, npost, w)


def _rwkv_prompt_call(pa, rp):
    b, t, _ = pa.shape
    C = RW_C
    assert t % C == 0
    vec = _full((1, D_MODEL))
    lora = _full((LANES, D_MODEL))
    tile3 = _full((N_PAIR, 1, LANES))
    return pl.pallas_call(
        _rwkv_prompt_kernel,
        grid=(b, t // C),
        in_specs=[pl.BlockSpec((1, C, A_COLS), lambda b, i: (b, i, 0)), _full((1, A_COLS)),
                  vec, lora, vec, lora, lora, vec, vec, tile3, tile3, tile3],
        out_specs=[pl.BlockSpec((1, C, D_MODEL), lambda b, i: (b, i, 0)),
                   pl.BlockSpec((1, H_A, HD_A, HD_A), lambda b, i: (b, 0, 0, 0)),
                   pl.BlockSpec((1, 1, A_COLS), lambda b, i: (b, 0, 0))],
        out_shape=[jax.ShapeDtypeStruct((b, t, D_MODEL), F32),
                   jax.ShapeDtypeStruct((b, H_A, HD_A, HD_A), F32),
                   jax.ShapeDtypeStruct((b, 1, A_COLS), F32)],
        scratch_shapes=[pltpu.VMEM((1, A_COLS), F32), pltpu.VMEM((N_PAIR, LANES, LANES), F32)],
        compiler_params=_params(("arbitrary", "arbitrary")),
        name="rwkv_prompt",
    )(pa, rp["mu"], rp["w0"], rp["w2p"], rp["a0"], rp["a2p"], rp["g2"], rp["k_k"], rp["k_a"],
      rp["rk3"], rp["lnw3"], rp["lnb3"])


def _gdn_prompt_call(qkv, ab, z, gp):
    b, t, _ = qkv.shape
    C = GD_C
    assert t % C == 0
    spec = lambda n: pl.BlockSpec((1, C, n), lambda b, i: (b, i, 0))
    return pl.pallas_call(
        _gdn_prompt_kernel,
        grid=(b, t // C),
        in_specs=[spec(3 * D_MODEL), spec(AB_COLS), spec(D_MODEL), _full((CONV_W, 3 * D_MODEL)),
                  _full((1, LANES)), _full((1, LANES)), _full((1, HD_B))],
        out_specs=[spec(D_MODEL),
                   pl.BlockSpec((1, CONV_W - 1, 3 * D_MODEL), lambda b, i: (b, 0, 0)),
                   pl.BlockSpec((1, H_B, HD_B, HD_B), lambda b, i: (b, 0, 0, 0))],
        out_shape=[jax.ShapeDtypeStruct((b, t, D_MODEL), F32),
                   jax.ShapeDtypeStruct((b, CONV_W - 1, 3 * D_MODEL), F32),
                   jax.ShapeDtypeStruct((b, H_B, HD_B, HD_B), F32)],
        scratch_shapes=[pltpu.VMEM((8, 3 * D_MODEL), F32)],
        compiler_params=_params(("arbitrary", "arbitrary")),
        name="gdn_prompt",
    )(qkv, ab, z, gp["conv_w"], gp["alog"], gp["dtb"], gp["norm_w"])


def _rwkv_sample(pa, shift0, wkv0, rp):
    n = pa.shape[0]
    vec = _full((1, D_MODEL))
    lora = _full((LANES, D_MODEL))
    flat = _full((n, D_MODEL))
    r, w, km, v, kkn, bb_, g = pl.pallas_call(
        _rwkv_sample_prep_kernel,
        grid=(1,),
        in_specs=[_full((n, A_COLS)), _full((n, A_COLS)), _full((1, A_COLS)), vec, lora, vec, lora, lora,
                  vec, vec],
        out_specs=[flat] * 7,
        out_shape=[jax.ShapeDtypeStruct((n, D_MODEL), F32)] * 7,
        compiler_params=_params(("arbitrary",)),
        name="rwkv_sample_prep",
    )(pa, shift0, rp["mu"], rp["w0"], rp["w2p"], rp["a0"], rp["a2p"], rp["g2"], rp["k_k"], rp["k_a"])

    bb = 8
    assert n % bb == 0
    heads = lambda x: x.reshape(n, H_A, HD_A)
    hspec = pl.BlockSpec((bb, H_A, HD_A), lambda i: (i, 0, 0))
    tspec = pl.BlockSpec((bb, HD_A, H_A), lambda i: (i, 0, 0))
    sspec = pl.BlockSpec((bb, H_A, HD_A, HD_A), lambda i: (i, 0, 0, 0))
    s_new, yt = pl.pallas_call(
        functools.partial(_rwkv_sample_state_kernel, bb=bb),
        grid=(n // bb,),
        in_specs=[hspec] * 5 + [tspec, sspec],
        out_specs=[sspec, tspec],
        out_shape=[jax.ShapeDtypeStruct(wkv0.shape, F32), jax.ShapeDtypeStruct((n, HD_A, H_A), F32)],
        compiler_params=_params(("arbitrary",)),
        name="rwkv_sample_state",
    )(heads(r), heads(w), heads(km), heads(kkn), heads(bb_), jnp.swapaxes(heads(v), 1, 2), wkv0)
    y = jnp.swapaxes(yt, 1, 2).reshape(n, D_MODEL)

    tile3 = _full((N_PAIR, 1, LANES))
    ya = pl.pallas_call(
        _rwkv_sample_post_kernel,
        grid=(1,),
        in_specs=[flat] * 5 + [tile3] * 3,
        out_specs=flat,
        out_shape=jax.ShapeDtypeStruct((n, D_MODEL), F32),
        compiler_params=_params(("arbitrary",)),
        name="rwkv_sample_post",
    )(y, r, km, v, g, rp["rk3"], rp["lnw3"], rp["lnb3"])
    return ya, s_new


def _gdn_sample(qkv, conv0, ssm0, ab, z, gp):
    n = qkv.shape[0]
    flat = _full((n, D_MODEL))
    cspec = _full((CONV_W - 1, n, 3 * D_MODEL))
    q, k, v, eg, beta, cnew = pl.pallas_call(
        _gdn_sample_prep_kernel,
        grid=(1,),
        in_specs=[_full((n, 3 * D_MODEL)), cspec, _full((n, AB_COLS)), _full((CONV_W, 3 * D_MODEL)),
                  _full((1, LANES)), _full((1, LANES))],
        out_specs=[flat] * 5 + [cspec],
        out_shape=[jax.ShapeDtypeStruct((n, D_MODEL), F32)] * 5
        + [jax.ShapeDtypeStruct((CONV_W - 1, n, 3 * D_MODEL), F32)],
        compiler_params=_params(("arbitrary",)),
        name="gdn_sample_prep",
    )(qkv, jnp.swapaxes(conv0, 0, 1), ab, gp["conv_w"], gp["alog"], gp["dtb"])

    bb = 4
    assert n % bb == 0
    heads = lambda x: x.reshape(n, H_B, HD_B)
    hspec = pl.BlockSpec((bb, H_B, HD_B), lambda i: (i, 0, 0))
    tspec = pl.BlockSpec((bb, HD_B, H_B), lambda i: (i, 0, 0))
    sspec = pl.BlockSpec((bb, H_B, HD_B, HD_B), lambda i: (i, 0, 0, 0))
    s_new, o = pl.pallas_call(
        functools.partial(_gdn_sample_state_kernel, bb=bb),
        grid=(n // bb,),
        in_specs=[tspec, tspec, hspec, hspec, hspec, sspec],
        out_specs=[sspec, hspec],
        out_shape=[jax.ShapeDtypeStruct(ssm0.shape, F32), jax.ShapeDtypeStruct((n, H_B, HD_B), F32)],
        compiler_params=_params(("arbitrary",)),
        name="gdn_sample_state",
    )(jnp.swapaxes(heads(q), 1, 2), jnp.swapaxes(heads(k), 1, 2), heads(v), heads(eg), heads(beta), ssm0)

    yb = pl.pallas_call(
        _gdn_sample_post_kernel,
        grid=(1,),
        in_specs=[flat, flat, _full((1, HD_B))],
        out_specs=flat,
        out_shape=jax.ShapeDtypeStruct((n, D_MODEL), F32),
        compiler_params=_params(("arbitrary",)),
        name="gdn_sample_post",
    )(o.reshape(n, D_MODEL), z, gp["norm_w"])
    return yb, jnp.swapaxes(cnew, 0, 1), s_new


def _pad_rows(x, rows, at):
    out = jnp.zeros((rows, x.shape[1]), x.dtype)
    return lax.dynamic_update_slice(out, x, (at, 0))


def kernel(x_prompt, x_sample, c_prompt, c_sample, state_rwkv_shift, state_rwkv_wkv, state_gdn_conv, state_gdn_ssm, w_ada, b_ada, norm_pre, norm_post, ffn_up, ffn_down, w_in, w_out, rwkv_mu, rwkv_w0, rwkv_w2, rwkv_a0, rwkv_a2, rwkv_g2, rwkv_k_k, rwkv_k_a, rwkv_r_k, rwkv_ln_w, rwkv_ln_b, gdn_conv, gdn_a_log, gdn_dt_bias, gdn_norm_w):
    depth = w_ada.shape[0]
    bp = x_prompt.shape[0]
    ns = x_sample.shape[0]

    n_c = bp + ns
    n_cp = -(-n_c // 16) * 16
    c_all = jnp.concatenate([c_prompt, c_sample, jnp.zeros((n_cp - n_c, D_MODEL), F32)], axis=0)
    mod = _ada_call(c_all, w_ada, b_ada)

    o_b = A_COLS
    o_ab = o_b + 3 * D_MODEL
    o_z = o_ab + 2 * H_B
    o_g = o_z + D_MODEL
    w_in_r = jnp.concatenate(
        [w_in[:, :, 0:o_b], w_in[:, :, o_b:o_ab], w_in[:, :, o_z:o_g], w_in[:, :, o_g:],
         w_in[:, :, o_ab:o_z], jnp.zeros((depth, D_MODEL, AB_COLS - 2 * H_B), F32)], axis=-1).astype(BF16)
    up_b = ffn_up.astype(BF16)
    down_b = ffn_down.astype(BF16)
    w_out_b = w_out.astype(BF16)

    yp = x_prompt
    ys = x_sample.reshape(1, ns, D_MODEL)
    outs = [[] for _ in range(8)]
    for l in range(depth):
        mod_p = mod[l, :, 0:bp].reshape(3 * N_SUB, bp, 1, D_MODEL)
        mod_s = mod[l, :, bp:bp + ns].reshape(3 * N_SUB, 1, ns, D_MODEL)
        npre = [norm_pre[l, i][None] for i in range(N_SUB)]
        npost = [norm_post[l, i][None] for i in range(N_SUB)]
        row = lambda x: x[l][None]
        tile3 = lambda x: x[l].reshape(N_PAIR, 1, LANES)
        rp = dict(mu=row(rwkv_mu), w0=row(rwkv_w0), a0=row(rwkv_a0), k_k=row(rwkv_k_k), k_a=row(rwkv_k_a),
                  w2p=_pad_rows(rwkv_w2[l], LANES, 0), a2p=_pad_rows(rwkv_a2[l], LANES, W_LORA),
                  g2=rwkv_g2[l], rk3=tile3(rwkv_r_k), lnw3=tile3(rwkv_ln_w), lnb3=tile3(rwkv_ln_b))
        lane_row = lambda x: jnp.zeros((1, LANES), F32).at[0, 0:H_B].set(x[l])
        gp = dict(conv_w=gdn_conv[l], norm_w=row(gdn_norm_w), alog=lane_row(gdn_a_log),
                  dtb=lane_row(gdn_dt_bias))

        yp = _ffn_call(yp, mod_p, npre[0], npost[0], up_b[l, 0], down_b[l, 0], 0)
        pa, qkv, z, pg, ab = _inproj_call(yp, mod_p, npre[1], w_in_r[l])
        ya, p_wkv, p_shift = _rwkv_prompt_call(pa, rp)
        yb, p_conv, p_ssm = _gdn_prompt_call(qkv, ab, z, gp)
        yp = _outproj_call(yp, ya, yb, pg, mod_p, npost[1], w_out_b[l])
        yp = _ffn_call(yp, mod_p, npre[2], npost[2], up_b[l, 1], down_b[l, 1], 2)

        ys = _ffn_call(ys, mod_s, npre[0], npost[0], up_b[l, 0], down_b[l, 0], 0)
        pa, qkv, z, pg, ab = _inproj_call(ys, mod_s, npre[1], w_in_r[l])
        ya, s_wkv = _rwkv_sample(pa[0], state_rwkv_shift[l], state_rwkv_wkv[l], rp)
        yb, s_conv, s_ssm = _gdn_sample(qkv[0], state_gdn_conv[l], state_gdn_ssm[l], ab[0], z[0], gp)
        ys = _outproj_call(ys, ya[None], yb[None], pg, mod_s, npost[1], w_out_b[l])
        ys = _ffn_call(ys, mod_s, npre[2], npost[2], up_b[l, 1], down_b[l, 1], 2)

        for lst, val in zip(outs, (p_shift[:, 0], p_wkv, p_conv, p_ssm, pa[0], s_wkv, s_conv, s_ssm)):
            lst.append(val)

    return (yp, ys.reshape(ns, 1, D_MODEL)) + tuple(jnp.stack(o) for o in outs)
```

```python
import functools

import jax
import jax.numpy as jnp
from jax import lax
from jax.experimental import pallas as pl
from jax.experimental.pallas import tpu as pltpu

F32 = jnp.float32
BF16 = jnp.bfloat16

D_MODEL = 1024
HD_A = 64
H_A = D_MODEL // HD_A
W_LORA = 64
A_LORA = 64
G_LORA = 128
A_COLS = 3 * D_MODEL + W_LORA + A_LORA + G_LORA
RWKV_GN_EPS = 6.4e-4
HD_B = 128
H_B = D_MODEL // HD_B
CONV_W = 4
D_FF = 2816
N_SUB = 3
FFN_RES = 0.5
RMS_EPS = 1e-6

LANES = 128
N_PAIR = D_MODEL // LANES
RW_C = 64
GD_C = 128
AB_COLS = LANES
IN_COLS_R = A_COLS + 3 * D_MODEL + D_MODEL + 2 * D_MODEL + AB_COLS
VMEM_LIMIT = 56 * 1024 * 1024
REC_PASSES = 1

_NN = (((1,), (0,)), ((), ()))
_NT = (((1,), (1,)), ((), ()))


def _dg(a, b, dn=_NN):
    return lax.dot_general(a, b, dn, preferred_element_type=F32)


def _split2(x):
    hi = x.astype(BF16)
    lo = (x - hi.astype(F32)).astype(BF16)
    return hi, lo


def _split3(x):
    h1 = x.astype(BF16)
    r1 = x - h1.astype(F32)
    h2 = r1.astype(BF16)
    h3 = (r1 - h2.astype(F32)).astype(BF16)
    return h1, h2, h3


def _mm(a, b, dn=_NN, passes=REC_PASSES):
    if passes == 1:
        return _dg(a.astype(BF16), b.astype(BF16), dn)
    ah, al = _split2(a)
    bh, bl = _split2(b)
    return _dg(ah, bh, dn) + (_dg(ah, bl, dn) + _dg(al, bh, dn))


def _mm_sel_rhs(x, e, pieces):
    parts = _split3(x) if pieces == 3 else _split2(x)
    out = _dg(parts[0], e)
    for p in parts[1:]:
        out = out + _dg(p, e)
    return out


def _mm_sel_lhs(e, x, pieces):
    parts = _split3(x) if pieces == 3 else _split2(x)
    out = _dg(e, parts[0])
    for p in parts[1:]:
        out = out + _dg(e, p)
    return out


def _iota2(shape, dim):
    return lax.broadcasted_iota(jnp.int32, shape, dim)


def _softplus(x):
    return jnp.maximum(x, 0.0) + jnp.log(1.0 + jnp.exp(-jnp.abs(x)))


def _silu(x):
    return x * jax.nn.sigmoid(x)


def _rms(x, w):
    return x * lax.rsqrt(jnp.mean(x * x, axis=-1, keepdims=True) + RMS_EPS) * w


def _mod_in(x, npre, mod_ref, sub):
    shift = mod_ref[3 * sub, 0]
    scale = mod_ref[3 * sub + 1, 0]
    return _rms(x, npre) * (1.0 + scale) + shift


def _tri_masks(n, top):
    i = _iota2((n, n), 0)
    j = _iota2((n, n), 1)
    masks = []
    s = 1
    while s <= top:
        sh = s.bit_length() - 1
        same = (i >> (sh + 1)) == (j >> (sh + 1))
        masks.append(same & (((i >> sh) & 1) == 1) & (((j >> sh) & 1) == 0))
        s *= 2
    return masks


def _tri_inv(Ls, masks):
    n = Ls[0].shape[0]
    eye = jnp.where(_iota2((n, n), 0) == _iota2((n, n), 1), 1.0, 0.0).astype(F32)
    Xs = [eye - jnp.where(masks[0], L, 0.0) for L in Ls]
    for m in masks[1:]:
        Ts = [_mm(X, jnp.where(m, L, 0.0)) for X, L in zip(Xs, Ls)]
        Xs = [X - _mm(T, X) for X, T in zip(Xs, Ts)]
    return Xs


def _to_tiles(x):
    return jnp.stack([x[:, LANES * p:LANES * (p + 1)] for p in range(N_PAIR)], axis=0)


def _pair_headsum(x3, ebd):
    p, n, l = x3.shape
    return _mm_sel_rhs(x3.reshape(p * n, l), ebd, 2).reshape(p, n, l)


def _ebd():
    i = _iota2((LANES, LANES), 0)
    j = _iota2((LANES, LANES), 1)
    return jnp.where((i >> 6) == (j >> 6), 1.0, 0.0).astype(BF16)


def _rwkv_prep(pa, prev, mu, w0, w2p, a0, a2p, g2, k_k, k_a):
    xs = pa + mu * (prev - pa)
    r = xs[:, 0:D_MODEL]
    k = xs[:, D_MODEL:2 * D_MODEL]
    v = xs[:, 2 * D_MODEL:3 * D_MODEL]
    wa = xs[:, 3 * D_MODEL:3 * D_MODEL + LANES]
    gd = xs[:, 3 * D_MODEL + LANES:A_COLS]
    w_lin = _mm(jnp.tanh(wa), w2p, passes=3)
    a_lin = _mm(wa, a2p, passes=3)
    g = _mm(jax.nn.sigmoid(gd), g2, passes=3)
    w_log = -_softplus(-(w0 + w_lin)) - 0.5
    lw = -jnp.exp(w_log)
    a = jax.nn.sigmoid(a0 + a_lin)
    kkr = k * k_k
    km = k * (1.0 + (a - 1.0) * k_a)
    return r, km, v, kkr, a, lw, g


def _rwkv_epilogue(y3, r3, km3, v3, rk3, lnw3, lnb3, ebd):
    mean = _pair_headsum(y3, ebd) * (1.0 / HD_A)
    yc = y3 - mean
    var = _pair_headsum(yc * yc, ebd) * (1.0 / HD_A)
    yn = yc * lax.rsqrt(var + RWKV_GN_EPS) * lnw3 + lnb3
    bonus = _pair_headsum(r3 * km3 * rk3, ebd) * v3
    return yn + bonus


def _gdn_gates(ab, alog, dtb):
    g_all = -jnp.exp(alog) * _softplus(ab + dtb)
    beta_all = jax.nn.sigmoid(ab)
    return g_all, beta_all


def _sel_matrix(offset):
    i = _iota2((LANES, D_MODEL), 0)
    j = _iota2((LANES, D_MODEL), 1)
    return jnp.where(i == (j >> 7) + offset, 1.0, 0.0).astype(BF16)


def _head_l2norm(x, eps):
    outs = []
    for h in range(H_B):
        xh = x[:, LANES * h:LANES * (h + 1)]
        outs.append(xh * lax.rsqrt(jnp.sum(xh * xh, axis=-1, keepdims=True) + eps))
    return outs


def _ada_kernel(c_ref, w_ref, b_ref, o_ref):
    s = _silu(c_ref[...]).astype(BF16)
    o_ref[...] = _dg(s, w_ref[...].astype(BF16)) + b_ref[...]


def _ffn_kernel(x_ref, mod_ref, npre_ref, npost_ref, up_ref, down_ref, o_ref, *, sub):
    x = x_ref[0]
    h = _mod_in(x, npre_ref[...], mod_ref, sub).astype(BF16)
    gate = _dg(h, up_ref[:, 0:D_FF])
    val = _dg(h, up_ref[:, D_FF:2 * D_FF])
    act = (_silu(gate) * val).astype(BF16)
    y = _dg(act, down_ref[...])
    o_ref[0] = x + FFN_RES * mod_ref[3 * sub + 2, 0] * _rms(y, npost_ref[...])


def _inproj_kernel(x_ref, mod_ref, npre_ref, w_ref, pa_ref, qkv_ref, z_ref, pg_ref, ab_ref):
    h = _mod_in(x_ref[0], npre_ref[...], mod_ref, 1).astype(BF16)
    o = 0
    for ref in (pa_ref, qkv_ref, z_ref, pg_ref, ab_ref):
        n = ref.shape[-1]
        ref[0] = _dg(h, w_ref[:, o:o + n])
        o += n


def _outproj_kernel(x_ref, ya_ref, yb_ref, pg_ref, mod_ref, npost_ref, w_ref, o_ref):
    pg = pg_ref[0]
    m = jax.nn.sigmoid(pg[:, 0:D_MODEL]) * ya_ref[0] + jax.nn.sigmoid(pg[:, D_MODEL:]) * yb_ref[0]
    y = _dg(m.astype(BF16), w_ref[...])
    o_ref[0] = x_ref[0] + mod_ref[3 * 1 + 2, 0] * _rms(y, npost_ref[...])


def _rwkv_prompt_kernel(pa_ref, mu_ref, w0_ref, w2_ref, a0_ref, a2_ref, g2_ref, kk_ref, ka_ref,
                        rk3_ref, lnw3_ref, lnb3_ref, ya_ref, s_ref, shift_ref, carry_ref, sbd_ref):
    C = RW_C
    t = pl.program_id(1)

    @pl.when(t == 0)
    def _():
        carry_ref[...] = jnp.zeros_like(carry_ref)
        sbd_ref[...] = jnp.zeros_like(sbd_ref)

    pa = pa_ref[0]
    prev = jnp.where(_iota2((C, 1), 0) == 0, carry_ref[...], pltpu.roll(pa, 1, 0))
    carry_ref[...] = pa[C - 1:C]
    r, km, v, kkr, a, lw, g = _rwkv_prep(pa, prev, mu_ref[...], w0_ref[...], w2_ref[...], a0_ref[...],
                                         a2_ref[...], g2_ref[...], kk_ref[...], ka_ref[...])

    tri = jnp.where(_iota2((C, C), 0) >= _iota2((C, C), 1), 1.0, 0.0).astype(BF16)
    cum = _mm_sel_lhs(tri, lw, 3)
    ebd = _ebd()
    r3, km3, v3, kkr3, a3, lw3, cum3 = (_to_tiles(x) for x in (r, km, v, kkr, a, lw, cum))
    kk3 = kkr3 * lax.rsqrt(_pair_headsum(kkr3 * kkr3, ebd) + 1e-12)
    b3 = kk3 * a3
    cl = cum3[:, C - 1:C, :]
    e_neg = jnp.exp(-cum3)
    e_end = jnp.exp(cl - cum3)
    rh3 = r3 * jnp.exp(cum3)
    ah3 = kk3 * jnp.exp(cum3 - lw3)
    bt3 = b3 * e_neg
    kt3 = km3 * e_neg
    be3 = b3 * e_end
    ke3 = km3 * e_end
    e_last3 = jnp.exp(cl)

    lane0 = _iota2((1, LANES), 1) < HD_A

    def sm(x):
        return jnp.concatenate([jnp.where(lane0, x, 0.0), jnp.where(lane0, 0.0, x)], axis=0)

    ii = _iota2((2 * C, 2 * C), 0)
    jj = _iota2((2 * C, 2 * C), 1)
    strict = jj < ii
    incl = jj <= ii
    masks = _tri_masks(2 * C, C // 2)

    P = range(N_PAIR)
    As, Rs, Bs, Ks, Vs, Bes, Kes = ([sm(x[p]) for p in P] for x in (ah3, rh3, bt3, kt3, v3, be3, ke3))
    Ss = [sbd_ref[p] for p in P]
    Lab = [jnp.where(strict, _mm(As[p], Bs[p], _NT), 0.0) for p in P]
    Lak = [jnp.where(strict, _mm(As[p], Ks[p], _NT), 0.0) for p in P]
    Ms = [_mm(As[p], Ss[p], _NT) + _mm(Lak[p], Vs[p]) for p in P]
    Tinv = _tri_inv(Lab, masks)
    Us = [-_mm(Tinv[p], Ms[p]) for p in P]
    Arb = [jnp.where(incl, _mm(Rs[p], Bs[p], _NT), 0.0) for p in P]
    Ark = [jnp.where(incl, _mm(Rs[p], Ks[p], _NT), 0.0) for p in P]
    Ys = [_mm(Rs[p], Ss[p], _NT) + _mm(Arb[p], Us[p]) + _mm(Ark[p], Vs[p]) for p in P]
    for p in P:
        sbd_ref[p] = Ss[p] * e_last3[p] + _mm(Us[p].T, Bes[p]) + _mm(Vs[p].T, Kes[p])
    y3 = jnp.stack([Y[0:C] + Y[C:2 * C] for Y in Ys], axis=0)

    out3 = _rwkv_epilogue(y3, r3, km3, v3, rk3_ref[...], lnw3_ref[...], lnb3_ref[...], ebd)
    for p in range(N_PAIR):
        ya_ref[0, :, LANES * p:LANES * (p + 1)] = out3[p] * g[:, LANES * p:LANES * (p + 1)]

    @pl.when(t == pl.num_programs(1) - 1)
    def _():
        shift_ref[0] = pa[C - 1:C]
        for p in range(N_PAIR):
            S = sbd_ref[p]
            s_ref[0, 2 * p] = S[0:HD_A, 0:HD_A]
            s_ref[0, 2 * p + 1] = pltpu.roll(S, HD_A, 1)[HD_A:LANES, 0:HD_A]


def _gdn_prompt_kernel(qkv_ref, ab_ref, z_ref, cw_ref, alog_ref, dtb_ref, nw_ref,
                       yb_ref, conv_ref, ssm_ref, carry_ref):
    C = GD_C
    t = pl.program_id(1)

    @pl.when(t == 0)
    def _():
        carry_ref[...] = jnp.zeros_like(carry_ref)
        ssm_ref[...] = jnp.zeros_like(ssm_ref)

    x = qkv_ref[0]
    ext = jnp.concatenate([carry_ref[...], x], axis=0)
    conv = x * cw_ref[CONV_W - 1:CONV_W, :]
    for s in range(1, CONV_W):
        conv = conv + pltpu.roll(ext, s, 0)[8:] * cw_ref[CONV_W - 1 - s:CONV_W - s, :]
    carry_ref[...] = x[C - 8:C]
    qkvc = _silu(conv)
    qs = _head_l2norm(qkvc[:, 0:D_MODEL], 1e-6)
    ks = _head_l2norm(qkvc[:, D_MODEL:2 * D_MODEL], 1e-6)
    vv = qkvc[:, 2 * D_MODEL:3 * D_MODEL]

    g_all, beta_all = _gdn_gates(ab_ref[0], alog_ref[...], dtb_ref[...])
    tri = jnp.where(_iota2((C, C), 0) >= _iota2((C, C), 1), 1.0, 0.0).astype(BF16)
    gc_all = _mm_sel_lhs(tri, g_all, 3)
    gexp = _mm_sel_rhs(gc_all, _sel_matrix(0), 3)
    bexp = _mm_sel_rhs(beta_all, _sel_matrix(H_B), 3)

    ii = _iota2((C, C), 0)
    jj = _iota2((C, C), 1)
    strict = jj < ii
    incl = jj <= ii
    masks = _tri_masks(C, C // 2)
    z = z_ref[0]

    H = range(H_B)
    sls = [slice(LANES * h, LANES * (h + 1)) for h in H]
    qh = [qs[h] * (HD_B ** -0.5) for h in H]
    gh = [gexp[:, sl] for sl in sls]
    bh = [bexp[:, sl] for sl in sls]
    dec = [jnp.exp(g - g.T) for g in gh]
    eg = [jnp.exp(g) for g in gh]
    glast = [g[C - 1:C, :] for g in gh]
    kb = [ks[h] * bh[h] for h in H]
    Ls = [jnp.where(strict, _mm(kb[h], ks[h], _NT) * dec[h], 0.0) for h in H]
    Tinv = _tri_inv(Ls, masks)
    us = [_mm(Tinv[h], vv[:, sls[h]] * bh[h]) for h in H]
    ws = [_mm(Tinv[h], kb[h] * eg[h]) for h in H]
    attn = [jnp.where(incl, _mm(qh[h], ks[h], _NT) * dec[h], 0.0) for h in H]
    Ss = [ssm_ref[0, h] for h in H]
    v_new = [us[h] - _mm(ws[h], Ss[h]) for h in H]
    os_ = [_mm(qh[h] * eg[h], Ss[h]) + _mm(attn[h], v_new[h]) for h in H]
    for h in H:
        k_dec = ks[h] * jnp.exp(glast[h] - gh[h])
        ssm_ref[0, h] = Ss[h] * jnp.exp(glast[h]) + _mm(k_dec.T, v_new[h])
    for h in H:
        o = os_[h]
        on = o * lax.rsqrt(jnp.mean(o * o, axis=-1, keepdims=True) + RMS_EPS) * nw_ref[...]
        yb_ref[0, :, sls[h]] = on * _silu(z[:, sls[h]])

    @pl.when(t == pl.num_programs(1) - 1)
    def _():
        conv_ref[0] = pltpu.roll(x, CONV_W - 1, 0)[0:CONV_W - 1]


def _rwkv_sample_prep_kernel(pa_ref, prev_ref, mu_ref, w0_ref, w2_ref, a0_ref, a2_ref, g2_ref, kk_ref,
                             ka_ref, r_ref, w_ref, km_ref, v_ref, kkn_ref, b_ref, g_ref):
    r, km, v, kkr, a, lw, g = _rwkv_prep(pa_ref[...], prev_ref[...], mu_ref[...], w0_ref[...], w2_ref[...],
                                         a0_ref[...], a2_ref[...], g2_ref[...], kk_ref[...], ka_ref[...])
    ebd = _ebd()
    kkr3 = _to_tiles(kkr)
    kk3 = kkr3 * lax.rsqrt(_pair_headsum(kkr3 * kkr3, ebd) + 1e-12)
    b3 = kk3 * _to_tiles(a)
    for p in range(N_PAIR):
        kkn_ref[:, LANES * p:LANES * (p + 1)] = kk3[p]
        b_ref[:, LANES * p:LANES * (p + 1)] = b3[p]
    r_ref[...] = r
    w_ref[...] = jnp.exp(lw)
    km_ref[...] = km
    v_ref[...] = v
    g_ref[...] = g


def _rwkv_sample_state_kernel(r_ref, w_ref, km_ref, kkn_ref, b_ref, vt_ref, s0_ref, s_ref, yt_ref, *, bb):
    lane = _iota2((HD_A, H_A), 1)

    def body(i, carry):
        rb, wb, kb, kkb, bbv = r_ref[i], w_ref[i], km_ref[i], kkn_ref[i], b_ref[i]
        vt = vt_ref[i]
        yt = jnp.zeros((HD_A, H_A), F32)
        for h in range(H_A):
            S = s0_ref[i, h]
            sa = -jnp.sum(S * kkb[h:h + 1, :], axis=1, keepdims=True)
            Sn = S * wb[h:h + 1, :] + sa * bbv[h:h + 1, :] + vt[:, h:h + 1] * kb[h:h + 1, :]
            s_ref[i, h] = Sn
            yt = jnp.where(lane == h, jnp.sum(Sn * rb[h:h + 1, :], axis=1, keepdims=True), yt)
        yt_ref[i] = yt
        return carry

    lax.fori_loop(0, bb, body, 0)


def _rwkv_sample_post_kernel(y_ref, r_ref, km_ref, v_ref, g_ref, rk3_ref, lnw3_ref, lnb3_ref, ya_ref):
    ebd = _ebd()
    y3, r3, km3, v3 = (_to_tiles(x[...]) for x in (y_ref, r_ref, km_ref, v_ref))
    out3 = _rwkv_epilogue(y3, r3, km3, v3, rk3_ref[...], lnw3_ref[...], lnb3_ref[...], ebd)
    g = g_ref[...]
    for p in range(N_PAIR):
        ya_ref[:, LANES * p:LANES * (p + 1)] = out3[p] * g[:, LANES * p:LANES * (p + 1)]


def _gdn_sample_prep_kernel(qkv_ref, c0_ref, ab_ref, cw_ref, alog_ref, dtb_ref,
                            q_ref, k_ref, v_ref, eg_ref, beta_ref, cnew_ref):
    x = qkv_ref[...]
    conv = x * cw_ref[CONV_W - 1:CONV_W, :]
    for j in range(CONV_W - 1):
        conv = conv + c0_ref[j] * cw_ref[j:j + 1, :]
    for j in range(CONV_W - 2):
        cnew_ref[j] = c0_ref[j + 1]
    cnew_ref[CONV_W - 2] = x
    qkvc = _silu(conv)
    qs = _head_l2norm(qkvc[:, 0:D_MODEL], 1e-6)
    ks = _head_l2norm(qkvc[:, D_MODEL:2 * D_MODEL], 1e-6)
    for h in range(H_B):
        q_ref[:, LANES * h:LANES * (h + 1)] = qs[h] * (HD_B ** -0.5)
        k_ref[:, LANES * h:LANES * (h + 1)] = ks[h]
    v_ref[...] = qkvc[:, 2 * D_MODEL:3 * D_MODEL]
    g_all, beta_all = _gdn_gates(ab_ref[...], alog_ref[...], dtb_ref[...])
    eg_ref[...] = jnp.exp(_mm_sel_rhs(g_all, _sel_matrix(0), 3))
    beta_ref[...] = _mm_sel_rhs(beta_all, _sel_matrix(H_B), 3)


def _gdn_sample_state_kernel(qt_ref, kt_ref, v_ref, eg_ref, beta_ref, s0_ref, s_ref, o_ref, *, bb):
    def body(i, carry):
        qt, kt = qt_ref[i], kt_ref[i]
        vb, egb, btb = v_ref[i], eg_ref[i], beta_ref[i]
        for h in range(H_B):
            Sd = s0_ref[i, h] * egb[h:h + 1, :]
            kcol = kt[:, h:h + 1]
            ks_row = jnp.sum(Sd * kcol, axis=0, keepdims=True)
            v_new = btb[h:h + 1, :] * (vb[h:h + 1, :] - ks_row)
            Sn = Sd + kcol * v_new
            s_ref[i, h] = Sn
            o_ref[i, h:h + 1, :] = jnp.sum(Sn * qt[:, h:h + 1], axis=0, keepdims=True)
        return carry

    lax.fori_loop(0, bb, body, 0)


def _gdn_sample_post_kernel(o_ref, z_ref, nw_ref, yb_ref):
    o = o_ref[...]
    z = z_ref[...]
    for h in range(H_B):
        sl = slice(LANES * h, LANES * (h + 1))
        oh = o[:, sl]
        on = oh * lax.rsqrt(jnp.mean(oh * oh, axis=-1, keepdims=True) + RMS_EPS) * nw_ref[...]
        yb_ref[:, sl] = on * _silu(z[:, sl])


def _params(sem):
    return pltpu.CompilerParams(dimension_semantics=sem, vmem_limit_bytes=VMEM_LIMIT)


def _full(shape):
    return pl.BlockSpec(shape, lambda *_: (0,) * len(shape))


_RESIDENT = pl.BlockSpec(memory_space=pltpu.VMEM)


def _row_tile(t, cap):
    tm = min(t, cap)
    assert t % tm == 0
    return tm


def _mod_spec(mod):
    rows = mod.shape[2]
    if rows == 1:
        return pl.BlockSpec((3 * N_SUB, 1, 1, D_MODEL), lambda b, i: (0, b, 0, 0))
    return pl.BlockSpec((3 * N_SUB, 1, rows, D_MODEL), lambda b, i: (0, 0, i, 0))


def _ada_call(c_all, w_ada, b_ada):
    depth = w_ada.shape[0]
    n = c_all.shape[0]
    return pl.pallas_call(
        _ada_kernel,
        grid=(depth, 3 * N_SUB),
        in_specs=[pl.BlockSpec((n, D_MODEL), lambda l, j: (0, 0)),
                  pl.BlockSpec((None, D_MODEL, D_MODEL), lambda l, j: (l, 0, j)),
                  pl.BlockSpec((None, 1, D_MODEL), lambda l, j: (l, 0, j))],
        out_specs=pl.BlockSpec((None, None, n, D_MODEL), lambda l, j: (l, j, 0, 0)),
        out_shape=jax.ShapeDtypeStruct((depth, 3 * N_SUB, n, D_MODEL), F32),
        compiler_params=_params(("arbitrary", "arbitrary")),
        name="ada_mod",
    )(c_all, w_ada, b_ada.reshape(depth, 1, 3 * N_SUB * D_MODEL))


def _ffn_call(x, mod, npre, npost, up, down, sub):
    b, t, _ = x.shape
    tm = _row_tile(t, 512)
    xspec = pl.BlockSpec((1, tm, D_MODEL), lambda b, i: (b, i, 0))
    return pl.pallas_call(
        functools.partial(_ffn_kernel, sub=sub),
        grid=(b, t // tm),
        in_specs=[xspec, _mod_spec(mod), _full((1, D_MODEL)), _full((1, D_MODEL)), _RESIDENT, _RESIDENT],
        out_specs=xspec,
        out_shape=jax.ShapeDtypeStruct(x.shape, F32),
        compiler_params=_params(("arbitrary", "arbitrary")),
        name="ffn",
    )(x, mod, npre, npost, up, down)


def _inproj_call(x, mod, npre, w):
    b, t, _ = x.shape
    tm = _row_tile(t, 256)
    widths = (A_COLS, 3 * D_MODEL, D_MODEL, 2 * D_MODEL, AB_COLS)
    spec = lambda n: pl.BlockSpec((1, tm, n), lambda b, i: (b, i, 0))
    return pl.pallas_call(
        _inproj_kernel,
        grid=(b, t // tm),
        in_specs=[spec(D_MODEL), _mod_spec(mod), _full((1, D_MODEL)), _RESIDENT],
        out_specs=[spec(n) for n in widths],
        out_shape=[jax.ShapeDtypeStruct((b, t, n), F32) for n in widths],
        compiler_params=_params(("arbitrary", "arbitrary")),
        name="in_proj",
    )(x, mod, npre, w)


def _outproj_call(x, ya, yb, pg, mod, npost, w):
    b, t, _ = x.shape
    tm = _row_tile(t, 512)
    spec = lambda n: pl.BlockSpec((1, tm, n), lambda b, i: (b, i, 0))
    return pl.pallas_call(
        _outproj_kernel,
        grid=(b, t // tm),
        in_specs=[spec(D_MODEL), spec(D_MODEL), spec(D_MODEL), spec(2 * D_MODEL), _mod_spec(mod),
                  _full((1, D_MODEL)), _RESIDENT],
        out_specs=spec(D_MODEL),
        out_shape=jax.ShapeDtypeStruct(x.shape, F32),
        compiler_params=_params(("arbitrary", "arbitrary")),
        name="out_proj",
    )(x, ya, yb, pg, mod, npost, w)


def _rwkv_prompt_call(pa, rp):
    b, t, _ = pa.shape
    C = RW_C
    assert t % C == 0
    vec = _full((1, D_MODEL))
    lora = _full((LANES, D_MODEL))
    tile3 = _full((N_PAIR, 1, LANES))
    return pl.pallas_call(
        _rwkv_prompt_kernel,
        grid=(b, t // C),
        in_specs=[pl.BlockSpec((1, C, A_COLS), lambda b, i: (b, i, 0)), _full((1, A_COLS)),
                  vec, lora, vec, lora, lora, vec, vec, tile3, tile3, tile3],
        out_specs=[pl.BlockSpec((1, C, D_MODEL), lambda b, i: (b, i, 0)),
                   pl.BlockSpec((1, H_A, HD_A, HD_A), lambda b, i: (b, 0, 0, 0)),
                   pl.BlockSpec((1, 1, A_COLS), lambda b, i: (b, 0, 0))],
        out_shape=[jax.ShapeDtypeStruct((b, t, D_MODEL), F32),
                   jax.ShapeDtypeStruct((b, H_A, HD_A, HD_A), F32),
                   jax.ShapeDtypeStruct((b, 1, A_COLS), F32)],
        scratch_shapes=[pltpu.VMEM((1, A_COLS), F32), pltpu.VMEM((N_PAIR, LANES, LANES), F32)],
        compiler_params=_params(("arbitrary", "arbitrary")),
        name="rwkv_prompt",
    )(pa, rp["mu"], rp["w0"], rp["w2p"], rp["a0"], rp["a2p"], rp["g2"], rp["k_k"], rp["k_a"],
      rp["rk3"], rp["lnw3"], rp["lnb3"])


def _gdn_prompt_call(qkv, ab, z, gp):
    b, t, _ = qkv.shape
    C = GD_C
    assert t % C == 0
    spec = lambda n: pl.BlockSpec((1, C, n), lambda b, i: (b, i, 0))
    return pl.pallas_call(
        _gdn_prompt_kernel,
        grid=(b, t // C),
        in_specs=[spec(3 * D_MODEL), spec(AB_COLS), spec(D_MODEL), _full((CONV_W, 3 * D_MODEL)),
                  _full((1, LANES)), _full((1, LANES)), _full((1, HD_B))],
        out_specs=[spec(D_MODEL),
                   pl.BlockSpec((1, CONV_W - 1, 3 * D_MODEL), lambda b, i: (b, 0, 0)),
                   pl.BlockSpec((1, H_B, HD_B, HD_B), lambda b, i: (b, 0, 0, 0))],
        out_shape=[jax.ShapeDtypeStruct((b, t, D_MODEL), F32),
                   jax.ShapeDtypeStruct((b, CONV_W - 1, 3 * D_MODEL), F32),
                   jax.ShapeDtypeStruct((b, H_B, HD_B, HD_B), F32)],
        scratch_shapes=[pltpu.VMEM((8, 3 * D_MODEL), F32)],
        compiler_params=_params(("arbitrary", "arbitrary")),
        name="gdn_prompt",
    )(qkv, ab, z, gp["conv_w"], gp["alog"], gp["dtb"], gp["norm_w"])


def _rwkv_sample(pa, shift0, wkv0, rp):
    n = pa.shape[0]
    vec = _full((1, D_MODEL))
    lora = _full((LANES, D_MODEL))
    flat = _full((n, D_MODEL))
    r, w, km, v, kkn, bb_, g = pl.pallas_call(
        _rwkv_sample_prep_kernel,
        grid=(1,),
        in_specs=[_full((n, A_COLS)), _full((n, A_COLS)), _full((1, A_COLS)), vec, lora, vec, lora, lora,
                  vec, vec],
        out_specs=[flat] * 7,
        out_shape=[jax.ShapeDtypeStruct((n, D_MODEL), F32)] * 7,
        compiler_params=_params(("arbitrary",)),
        name="rwkv_sample_prep",
    )(pa, shift0, rp["mu"], rp["w0"], rp["w2p"], rp["a0"], rp["a2p"], rp["g2"], rp["k_k"], rp["k_a"])

    bb = 8
    assert n % bb == 0
    heads = lambda x: x.reshape(n, H_A, HD_A)
    hspec = pl.BlockSpec((bb, H_A, HD_A), lambda i: (i, 0, 0))
    tspec = pl.BlockSpec((bb, HD_A, H_A), lambda i: (i, 0, 0))
    sspec = pl.BlockSpec((bb, H_A, HD_A, HD_A), lambda i: (i, 0, 0, 0))
    s_new, yt = pl.pallas_call(
        functools.partial(_rwkv_sample_state_kernel, bb=bb),
        grid=(n // bb,),
        in_specs=[hspec] * 5 + [tspec, sspec],
        out_specs=[sspec, tspec],
        out_shape=[jax.ShapeDtypeStruct(wkv0.shape, F32), jax.ShapeDtypeStruct((n, HD_A, H_A), F32)],
        compiler_params=_params(("arbitrary",)),
        name="rwkv_sample_state",
    )(heads(r), heads(w), heads(km), heads(kkn), heads(bb_), jnp.swapaxes(heads(v), 1, 2), wkv0)
    y = jnp.swapaxes(yt, 1, 2).reshape(n, D_MODEL)

    tile3 = _full((N_PAIR, 1, LANES))
    ya = pl.pallas_call(
        _rwkv_sample_post_kernel,
        grid=(1,),
        in_specs=[flat] * 5 + [tile3] * 3,
        out_specs=flat,
        out_shape=jax.ShapeDtypeStruct((n, D_MODEL), F32),
        compiler_params=_params(("arbitrary",)),
        name="rwkv_sample_post",
    )(y, r, km, v, g, rp["rk3"], rp["lnw3"], rp["lnb3"])
    return ya, s_new


def _gdn_sample(qkv, conv0, ssm0, ab, z, gp):
    n = qkv.shape[0]
    flat = _full((n, D_MODEL))
    cspec = _full((CONV_W - 1, n, 3 * D_MODEL))
    q, k, v, eg, beta, cnew = pl.pallas_call(
        _gdn_sample_prep_kernel,
        grid=(1,),
        in_specs=[_full((n, 3 * D_MODEL)), cspec, _full((n, AB_COLS)), _full((CONV_W, 3 * D_MODEL)),
                  _full((1, LANES)), _full((1, LANES))],
        out_specs=[flat] * 5 + [cspec],
        out_shape=[jax.ShapeDtypeStruct((n, D_MODEL), F32)] * 5
        + [jax.ShapeDtypeStruct((CONV_W - 1, n, 3 * D_MODEL), F32)],
        compiler_params=_params(("arbitrary",)),
        name="gdn_sample_prep",
    )(qkv, jnp.swapaxes(conv0, 0, 1), ab, gp["conv_w"], gp["alog"], gp["dtb"])

    bb = 4
    assert n % bb == 0
    heads = lambda x: x.reshape(n, H_B, HD_B)
    hspec = pl.BlockSpec((bb, H_B, HD_B), lambda i: (i, 0, 0))
    tspec = pl.BlockSpec((bb, HD_B, H_B), lambda i: (i, 0, 0))
    sspec = pl.BlockSpec((bb, H_B, HD_B, HD_B), lambda i: (i, 0, 0, 0))
    s_new, o = pl.pallas_call(
        functools.partial(_gdn_sample_state_kernel, bb=bb),
        grid=(n // bb,),
        in_specs=[tspec, tspec, hspec, hspec, hspec, sspec],
        out_specs=[sspec, hspec],
        out_shape=[jax.ShapeDtypeStruct(ssm0.shape, F32), jax.ShapeDtypeStruct((n, H_B, HD_B), F32)],
        compiler_params=_params(("arbitrary",)),
        name="gdn_sample_state",
    )(jnp.swapaxes(heads(q), 1, 2), jnp.swapaxes(heads(k), 1, 2), heads(v), heads(eg), heads(beta), ssm0)

    yb = pl.pallas_call(
        _gdn_sample_post_kernel,
        grid=(1,),
        in_specs=[flat, flat, _full((1, HD_B))],
        out_specs=flat,
        out_shape=jax.ShapeDtypeStruct((n, D_MODEL), F32),
        compiler_params=_params(("arbitrary",)),
        name="gdn_sample_post",
    )(o.reshape(n, D_MODEL), z, gp["norm_w"])
    return yb, jnp.swapaxes(cnew, 0, 1), s_new


def _pad_rows(x, rows, at):
    out = jnp.zeros((rows, x.shape[1]), x.dtype)
    return lax.dynamic_update_slice(out, x, (at, 0))


def kernel(x_prompt, x_sample, c_prompt, c_sample, state_rwkv_shift, state_rwkv_wkv, state_gdn_conv, state_gdn_ssm, w_ada, b_ada, norm_pre, norm_post, ffn_up, ffn_down, w_in, w_out, rwkv_mu, rwkv_w0, rwkv_w2, rwkv_a0, rwkv_a2, rwkv_g2, rwkv_k_k, rwkv_k_a, rwkv_r_k, rwkv_ln_w, rwkv_ln_b, gdn_conv, gdn_a_log, gdn_dt_bias, gdn_norm_w):
    depth = w_ada.shape[0]
    bp = x_prompt.shape[0]
    ns = x_sample.shape[0]

    n_c = bp + ns
    n_cp = -(-n_c // 16) * 16
    c_all = jnp.concatenate([c_prompt, c_sample, jnp.zeros((n_cp - n_c, D_MODEL), F32)], axis=0)
    mod = _ada_call(c_all, w_ada, b_ada)

    o_b = A_COLS
    o_ab = o_b + 3 * D_MODEL
    o_z = o_ab + 2 * H_B
    o_g = o_z + D_MODEL
    w_in_r = jnp.concatenate(
        [w_in[:, :, 0:o_b], w_in[:, :, o_b:o_ab], w_in[:, :, o_z:o_g], w_in[:, :, o_g:],
         w_in[:, :, o_ab:o_z], jnp.zeros((depth, D_MODEL, AB_COLS - 2 * H_B), F32)], axis=-1).astype(BF16)
    up_b = ffn_up.astype(BF16)
    down_b = ffn_down.astype(BF16)
    w_out_b = w_out.astype(BF16)

    yp = x_prompt
    ys = x_sample.reshape(1, ns, D_MODEL)
    outs = [[] for _ in range(8)]
    for l in range(depth):
        mod_p = mod[l, :, 0:bp].reshape(3 * N_SUB, bp, 1, D_MODEL)
        mod_s = mod[l, :, bp:bp + ns].reshape(3 * N_SUB, 1, ns, D_MODEL)
        npre = [norm_pre[l, i][None] for i in range(N_SUB)]
        npost = [norm_post[l, i][None] for i in range(N_SUB)]
        row = lambda x: x[l][None]
        tile3 = lambda x: x[l].reshape(N_PAIR, 1, LANES)
        rp = dict(mu=row(rwkv_mu), w0=row(rwkv_w0), a0=row(rwkv_a0), k_k=row(rwkv_k_k), k_a=row(rwkv_k_a),
                  w2p=_pad_rows(rwkv_w2[l], LANES, 0), a2p=_pad_rows(rwkv_a2[l], LANES, W_LORA),
                  g2=rwkv_g2[l], rk3=tile3(rwkv_r_k), lnw3=tile3(rwkv_ln_w), lnb3=tile3(rwkv_ln_b))
        lane_row = lambda x: jnp.zeros((1, LANES), F32).at[0, 0:H_B].set(x[l])
        gp = dict(conv_w=gdn_conv[l], norm_w=row(gdn_norm_w), alog=lane_row(gdn_a_log),
                  dtb=lane_row(gdn_dt_bias))

        yp = _ffn_call(yp, mod_p, npre[0], npost[0], up_b[l, 0], down_b[l, 0], 0)
        pa, qkv, z, pg, ab = _inproj_call(yp, mod_p, npre[1], w_in_r[l])
        ya, p_wkv, p_shift = _rwkv_prompt_call(pa, rp)
        yb, p_conv, p_ssm = _gdn_prompt_call(qkv, ab, z, gp)
        yp = _outproj_call(yp, ya, yb, pg, mod_p, npost[1], w_out_b[l])
        yp = _ffn_call(yp, mod_p, npre[2], npost[2], up_b[l, 1], down_b[l, 1], 2)

        ys = _ffn_call(ys, mod_s, npre[0], npost[0], up_b[l, 0], down_b[l, 0], 0)
        pa, qkv, z, pg, ab = _inproj_call(ys, mod_s, npre[1], w_in_r[l])
        ya, s_wkv = _rwkv_sample(pa[0], state_rwkv_shift[l], state_rwkv_wkv[l], rp)
        yb, s_conv, s_ssm = _gdn_sample(qkv[0], state_gdn_conv[l], state_gdn_ssm[l], ab[0], z[0], gp)
        ys = _outproj_call(ys, ya[None], yb[None], pg, mod_s, npost[1], w_out_b[l])
        ys = _ffn_call(ys, mod_s, npre[2], npost[2], up_b[l, 1], down_b[l, 1], 2)

        for lst, val in zip(outs, (p_shift[:, 0], p_wkv, p_conv, p_ssm, pa[0], s_wkv, s_conv, s_ssm)):
            lst.append(val)

    return (yp, ys.reshape(ns, 1, D_MODEL)) + tuple(jnp.stack(o) for o in outs)
```

```python
import functools

import jax
import jax.numpy as jnp
from jax import lax
from jax.experimental import pallas as pl
from jax.experimental.pallas import tpu as pltpu

F32 = jnp.float32
BF16 = jnp.bfloat16

D_MODEL = 1024
HD_A = 64
H_A = D_MODEL // HD_A
W_LORA = 64
A_LORA = 64
G_LORA = 128
A_COLS = 3 * D_MODEL + W_LORA + A_LORA + G_LORA
RWKV_GN_EPS = 6.4e-4
HD_B = 128
H_B = D_MODEL // HD_B
CONV_W = 4
D_FF = 2816
N_SUB = 3
FFN_RES = 0.5
RMS_EPS = 1e-6

LANES = 128
N_PAIR = D_MODEL // LANES
RW_C = 64
GD_C = 128
AB_COLS = LANES
IN_COLS_R = A_COLS + 3 * D_MODEL + D_MODEL + 2 * D_MODEL + AB_COLS
VMEM_LIMIT = 56 * 1024 * 1024
REC_PASSES = 1

_NN = (((1,), (0,)), ((), ()))
_NT = (((1,), (1,)), ((), ()))


def _dg(a, b, dn=_NN):
    return lax.dot_general(a, b, dn, preferred_element_type=F32)


def _split2(x):
    hi = x.astype(BF16)
    lo = (x - hi.astype(F32)).astype(BF16)
    return hi, lo


def _split3(x):
    h1 = x.astype(BF16)
    r1 = x - h1.astype(F32)
    h2 = r1.astype(BF16)
    h3 = (r1 - h2.astype(F32)).astype(BF16)
    return h1, h2, h3


def _mm(a, b, dn=_NN, passes=REC_PASSES):
    if passes == 1:
        return _dg(a.astype(BF16), b.astype(BF16), dn)
    ah, al = _split2(a)
    bh, bl = _split2(b)
    return _dg(ah, bh, dn) + (_dg(ah, bl, dn) + _dg(al, bh, dn))


def _mm_sel_rhs(x, e, pieces):
    parts = _split3(x) if pieces == 3 else _split2(x)
    out = _dg(parts[0], e)
    for p in parts[1:]:
        out = out + _dg(p, e)
    return out


def _mm_sel_lhs(e, x, pieces):
    parts = _split3(x) if pieces == 3 else _split2(x)
    out = _dg(e, parts[0])
    for p in parts[1:]:
        out = out + _dg(e, p)
    return out


def _iota2(shape, dim):
    return lax.broadcasted_iota(jnp.int32, shape, dim)


def _softplus(x):
    return jnp.maximum(x, 0.0) + jnp.log(1.0 + jnp.exp(-jnp.abs(x)))


def _silu(x):
    return x * jax.nn.sigmoid(x)


def _rms(x, w):
    return x * lax.rsqrt(jnp.mean(x * x, axis=-1, keepdims=True) + RMS_EPS) * w


def _mod_in(x, npre, mod_ref, sub):
    shift = mod_ref[3 * sub, 0]
    scale = mod_ref[3 * sub + 1, 0]
    return _rms(x, npre) * (1.0 + scale) + shift


def _tri_masks(n, top):
    i = _iota2((n, n), 0)
    j = _iota2((n, n), 1)
    masks = []
    s = 1
    while s <= top:
        sh = s.bit_length() - 1
        same = (i >> (sh + 1)) == (j >> (sh + 1))
        masks.append(same & (((i >> sh) & 1) == 1) & (((j >> sh) & 1) == 0))
        s *= 2
    return masks


def _tri_inv(Ls, masks):
    n = Ls[0].shape[0]
    eye = jnp.where(_iota2((n, n), 0) == _iota2((n, n), 1), 1.0, 0.0).astype(F32)
    Xs = [eye - jnp.where(masks[0], L, 0.0) for L in Ls]
    Lb = [L.astype(BF16) for L in Ls]
    zero = jnp.zeros((), BF16)
    for m in masks[1:]:
        Xb = [X.astype(BF16) for X in Xs]
        Ts = [_dg(xb, jnp.where(m, lb, zero)) for xb, lb in zip(Xb, Lb)]
        Xs = [X - _dg(T.astype(BF16), xb) for X, T, xb in zip(Xs, Ts, Xb)]
    return Xs


def _to_tiles(x):
    return jnp.stack([x[:, LANES * p:LANES * (p + 1)] for p in range(N_PAIR)], axis=0)


def _pair_headsum(x3, ebd):
    p, n, l = x3.shape
    return _mm_sel_rhs(x3.reshape(p * n, l), ebd, 2).reshape(p, n, l)


def _ebd():
    i = _iota2((LANES, LANES), 0)
    j = _iota2((LANES, LANES), 1)
    return jnp.where((i >> 6) == (j >> 6), 1.0, 0.0).astype(BF16)


def _mm_lora(a, w_ref):
    ah, al = _split2(a)
    return _dg(ah, w_ref[0]) + (_dg(ah, w_ref[1]) + _dg(al, w_ref[0]))


def _rwkv_prep(pa, prev, mu, w0, w2p, a0, a2p, g2, k_k, k_a):
    xs = pa + mu * (prev - pa)
    r = xs[:, 0:D_MODEL]
    k = xs[:, D_MODEL:2 * D_MODEL]
    v = xs[:, 2 * D_MODEL:3 * D_MODEL]
    wa = xs[:, 3 * D_MODEL:3 * D_MODEL + LANES]
    gd = xs[:, 3 * D_MODEL + LANES:A_COLS]
    w_lin = _mm_lora(jnp.tanh(wa), w2p)
    a_lin = _mm_lora(wa, a2p)
    g = _mm_lora(jax.nn.sigmoid(gd), g2)
    w_log = -_softplus(-(w0 + w_lin)) - 0.5
    lw = -jnp.exp(w_log)
    a = jax.nn.sigmoid(a0 + a_lin)
    kkr = k * k_k
    km = k * (1.0 + (a - 1.0) * k_a)
    return r, km, v, kkr, a, lw, g


def _rwkv_epilogue(y3, r3, km3, v3, rk3, lnw3, lnb3, ebd):
    mean = _pair_headsum(y3, ebd) * (1.0 / HD_A)
    yc = y3 - mean
    var = _pair_headsum(yc * yc, ebd) * (1.0 / HD_A)
    yn = yc * lax.rsqrt(var + RWKV_GN_EPS) * lnw3 + lnb3
    bonus = _pair_headsum(r3 * km3 * rk3, ebd) * v3
    return yn + bonus


def _gdn_gates(ab, alog, dtb):
    g_all = -jnp.exp(alog) * _softplus(ab + dtb)
    beta_all = jax.nn.sigmoid(ab)
    return g_all, beta_all


def _sel_matrix(offset):
    i = _iota2((LANES, D_MODEL), 0)
    j = _iota2((LANES, D_MODEL), 1)
    return jnp.where(i == (j >> 7) + offset, 1.0, 0.0).astype(BF16)


def _head_l2norm(x, eps):
    outs = []
    for h in range(H_B):
        xh = x[:, LANES * h:LANES * (h + 1)]
        outs.append(xh * lax.rsqrt(jnp.sum(xh * xh, axis=-1, keepdims=True) + eps))
    return outs


def _ada_kernel(c_ref, w_ref, b_ref, o_ref):
    s = _silu(c_ref[...]).astype(BF16)
    o_ref[...] = _dg(s, w_ref[...].astype(BF16)) + b_ref[...]


def _ffn_kernel(x_ref, mod_ref, npre_ref, npost_ref, up_ref, down_ref, o_ref, *, sub):
    x = x_ref[0]
    h = _mod_in(x, npre_ref[...], mod_ref, sub).astype(BF16)
    gate = _dg(h, up_ref[:, 0:D_FF])
    val = _dg(h, up_ref[:, D_FF:2 * D_FF])
    act = (_silu(gate) * val).astype(BF16)
    y = _dg(act, down_ref[...])
    o_ref[0] = x + FFN_RES * mod_ref[3 * sub + 2, 0] * _rms(y, npost_ref[...])


def _inproj_kernel(x_ref, mod_ref, npre_ref, w_ref, pa_ref, qkv_ref, z_ref, pg_ref, ab_ref):
    h = _mod_in(x_ref[0], npre_ref[...], mod_ref, 1).astype(BF16)
    o = 0
    for ref in (pa_ref, qkv_ref, z_ref, pg_ref, ab_ref):
        n = ref.shape[-1]
        ref[0] = _dg(h, w_ref[:, o:o + n])
        o += n


def _outproj_kernel(x_ref, ya_ref, yb_ref, pg_ref, mod_ref, npost_ref, w_ref, o_ref):
    pg = pg_ref[0]
    m = jax.nn.sigmoid(pg[:, 0:D_MODEL]) * ya_ref[0] + jax.nn.sigmoid(pg[:, D_MODEL:]) * yb_ref[0]
    y = _dg(m.astype(BF16), w_ref[...])
    o_ref[0] = x_ref[0] + mod_ref[3 * 1 + 2, 0] * _rms(y, npost_ref[...])


def _rwkv_prompt_kernel(pa_ref, mu_ref, w0_ref, w2_ref, a0_ref, a2_ref, g2_ref, kk_ref, ka_ref,
                        rk3_ref, lnw3_ref, lnb3_ref, ya_ref, s_ref, shift_ref, carry_ref, sbd_ref):
    C = RW_C
    t = pl.program_id(1)

    @pl.when(t == 0)
    def _():
        carry_ref[...] = jnp.zeros_like(carry_ref)
        sbd_ref[...] = jnp.zeros_like(sbd_ref)

    pa = pa_ref[0]
    prev = jnp.where(_iota2((C, 1), 0) == 0, carry_ref[...], pltpu.roll(pa, 1, 0))
    carry_ref[...] = pa[C - 1:C]
    r, km, v, kkr, a, lw, g = _rwkv_prep(pa, prev, mu_ref[...], w0_ref[...], w2_ref, a0_ref[...],
                                         a2_ref, g2_ref, kk_ref[...], ka_ref[...])

    tri = jnp.where(_iota2((C, C), 0) >= _iota2((C, C), 1), 1.0, 0.0).astype(BF16)
    cum = _mm_sel_lhs(tri, lw, 3)
    ebd = _ebd()
    r3, km3, v3, kkr3, a3, lw3, cum3 = (_to_tiles(x) for x in (r, km, v, kkr, a, lw, cum))
    kk3 = kkr3 * lax.rsqrt(_pair_headsum(kkr3 * kkr3, ebd) + 1e-12)
    b3 = kk3 * a3
    cl = cum3[:, C - 1:C, :]
    e_neg = jnp.exp(-cum3)
    e_end = jnp.exp(cl - cum3)
    rh3 = r3 * jnp.exp(cum3)
    ah3 = kk3 * jnp.exp(cum3 - lw3)
    bt3 = b3 * e_neg
    kt3 = km3 * e_neg
    be3 = b3 * e_end
    ke3 = km3 * e_end
    e_last3 = jnp.exp(cl)

    lane0 = _iota2((1, LANES), 1) < HD_A

    def sm(x):
        return jnp.concatenate([jnp.where(lane0, x, 0.0), jnp.where(lane0, 0.0, x)], axis=0)

    ii = _iota2((2 * C, 2 * C), 0)
    jj = _iota2((2 * C, 2 * C), 1)
    strict = jj < ii
    incl = jj <= ii
    masks = _tri_masks(2 * C, C // 2)

    P = range(N_PAIR)
    T2 = 2 * C
    bsm = lambda x: sm(x).astype(BF16)
    ARs = [jnp.concatenate([bsm(ah3[p]), bsm(rh3[p])], axis=0) for p in P]
    BKs = [jnp.concatenate([bsm(bt3[p]), bsm(kt3[p])], axis=0) for p in P]
    BKe = [jnp.concatenate([bsm(be3[p]), bsm(ke3[p])], axis=0) for p in P]
    Vs = [sm(v3[p]) for p in P]
    Vb = [V.astype(BF16) for V in Vs]
    Ss = [sbd_ref[p] for p in P]
    QQ = [_dg(ARs[p], BKs[p], _NT) for p in P]
    QS = [_dg(ARs[p], Ss[p].astype(BF16), _NT) for p in P]
    Lab = [jnp.where(strict, Q[0:T2, 0:T2], 0.0) for Q in QQ]
    Lak = [jnp.where(strict, Q[0:T2, T2:2 * T2], 0.0).astype(BF16) for Q in QQ]
    Ms = [QS[p][0:T2] + _dg(Lak[p], Vb[p]) for p in P]
    Tinv = _tri_inv(Lab, masks)
    Us = [-_dg(Tinv[p].astype(BF16), Ms[p].astype(BF16)) for p in P]
    UVb = [jnp.concatenate([Us[p].astype(BF16), Vb[p]], axis=0) for p in P]
    incl2 = jnp.concatenate([incl, incl], axis=1)
    Arbk = [jnp.where(incl2, Q[T2:2 * T2, :], 0.0).astype(BF16) for Q in QQ]
    Ys = [QS[p][T2:2 * T2] + _dg(Arbk[p], UVb[p]) for p in P]
    for p in P:
        UVt = jnp.concatenate([Us[p].T, Vs[p].T], axis=1).astype(BF16)
        sbd_ref[p] = Ss[p] * e_last3[p] + _dg(UVt, BKe[p])
    y3 = jnp.stack([Y[0:C] + Y[C:2 * C] for Y in Ys], axis=0)

    out3 = _rwkv_epilogue(y3, r3, km3, v3, rk3_ref[...], lnw3_ref[...], lnb3_ref[...], ebd)
    for p in range(N_PAIR):
        ya_ref[0, :, LANES * p:LANES * (p + 1)] = out3[p] * g[:, LANES * p:LANES * (p + 1)]

    @pl.when(t == pl.num_programs(1) - 1)
    def _():
        shift_ref[0] = pa[C - 1:C]
        for p in range(N_PAIR):
            S = sbd_ref[p]
            s_ref[0, 2 * p] = S[0:HD_A, 0:HD_A]
            s_ref[0, 2 * p + 1] = pltpu.roll(S, HD_A, 1)[HD_A:LANES, 0:HD_A]


def _gdn_prompt_kernel(qkv_ref, ab_ref, z_ref, cw_ref, alog_ref, dtb_ref, nw_ref,
                       yb_ref, conv_ref, ssm_ref, carry_ref):
    C = GD_C
    t = pl.program_id(1)

    @pl.when(t == 0)
    def _():
        carry_ref[...] = jnp.zeros_like(carry_ref)
        ssm_ref[...] = jnp.zeros_like(ssm_ref)

    x = qkv_ref[0]
    ext = jnp.concatenate([carry_ref[...], x], axis=0)
    conv = x * cw_ref[CONV_W - 1:CONV_W, :]
    for s in range(1, CONV_W):
        conv = conv + pltpu.roll(ext, s, 0)[8:] * cw_ref[CONV_W - 1 - s:CONV_W - s, :]
    carry_ref[...] = x[C - 8:C]
    qkvc = _silu(conv)
    qs = _head_l2norm(qkvc[:, 0:D_MODEL], 1e-6)
    ks = _head_l2norm(qkvc[:, D_MODEL:2 * D_MODEL], 1e-6)
    vv = qkvc[:, 2 * D_MODEL:3 * D_MODEL]

    g_all, beta_all = _gdn_gates(ab_ref[0], alog_ref[...], dtb_ref[...])
    tri = jnp.where(_iota2((C, C), 0) >= _iota2((C, C), 1), 1.0, 0.0).astype(BF16)
    gc_all = _mm_sel_lhs(tri, g_all, 3)
    gexp = _mm_sel_rhs(gc_all, _sel_matrix(0), 3)
    bexp = _mm_sel_rhs(beta_all, _sel_matrix(H_B), 3)

    ii = _iota2((C, C), 0)
    jj = _iota2((C, C), 1)
    strict = jj < ii
    incl = jj <= ii
    masks = _tri_masks(C, C // 2)
    z = z_ref[0]

    H = range(H_B)
    sls = [slice(LANES * h, LANES * (h + 1)) for h in H]
    qh = [qs[h] * (HD_B ** -0.5) for h in H]
    gh = [gexp[:, sl] for sl in sls]
    bh = [bexp[:, sl] for sl in sls]
    gct = gc_all.T
    dec = [jnp.exp(gh[h] - gct[h:h + 1, :]) for h in H]
    eg = [jnp.exp(g) for g in gh]
    glast = [g[C - 1:C, :] for g in gh]
    kb = [ks[h] * bh[h] for h in H]
    KQ = [_dg(jnp.concatenate([kb[h], qh[h]], axis=0).astype(BF16), ks[h].astype(BF16), _NT) for h in H]
    Ls = [jnp.where(strict, KQ[h][0:C] * dec[h], 0.0) for h in H]
    attn = [jnp.where(incl, KQ[h][C:2 * C] * dec[h], 0.0) for h in H]
    Tinv = _tri_inv(Ls, masks)
    rhs = [jnp.concatenate([vv[:, sls[h]] * bh[h], kb[h] * eg[h]], axis=1).astype(BF16) for h in H]
    UW = [_dg(Tinv[h].astype(BF16), rhs[h]) for h in H]
    Ss = [ssm_ref[0, h] for h in H]
    Sb = [S.astype(BF16) for S in Ss]
    v_new = [UW[h][:, 0:LANES] - _dg(UW[h][:, LANES:2 * LANES].astype(BF16), Sb[h]) for h in H]
    vnb = [v.astype(BF16) for v in v_new]
    os_ = [_dg(jnp.concatenate([qh[h] * eg[h], attn[h]], axis=1).astype(BF16),
               jnp.concatenate([Sb[h], vnb[h]], axis=0)) for h in H]
    for h in H:
        k_dec = ks[h] * jnp.exp(glast[h] - gh[h])
        ssm_ref[0, h] = Ss[h] * jnp.exp(glast[h]) + _dg(k_dec.T.astype(BF16), vnb[h])
    for h in H:
        o = os_[h]
        on = o * lax.rsqrt(jnp.mean(o * o, axis=-1, keepdims=True) + RMS_EPS) * nw_ref[...]
        yb_ref[0, :, sls[h]] = on * _silu(z[:, sls[h]])

    @pl.when(t == pl.num_programs(1) - 1)
    def _():
        conv_ref[0] = pltpu.roll(x, CONV_W - 1, 0)[0:CONV_W - 1]


def _rwkv_sample_prep_kernel(pa_ref, prev_ref, mu_ref, w0_ref, w2_ref, a0_ref, a2_ref, g2_ref, kk_ref,
                             ka_ref, r_ref, w_ref, km_ref, v_ref, kkn_ref, b_ref, g_ref):
    r, km, v, kkr, a, lw, g = _rwkv_prep(pa_ref[...], prev_ref[...], mu_ref[...], w0_ref[...], w2_ref,
                                         a0_ref[...], a2_ref, g2_ref, kk_ref[...], ka_ref[...])
    ebd = _ebd()
    kkr3 = _to_tiles(kkr)
    kk3 = kkr3 * lax.rsqrt(_pair_headsum(kkr3 * kkr3, ebd) + 1e-12)
    b3 = kk3 * _to_tiles(a)
    for p in range(N_PAIR):
        kkn_ref[:, LANES * p:LANES * (p + 1)] = kk3[p]
        b_ref[:, LANES * p:LANES * (p + 1)] = b3[p]
    r_ref[...] = r
    w_ref[...] = jnp.exp(lw)
    km_ref[...] = km
    v_ref[...] = v
    g_ref[...] = g


def _rwkv_sample_state_kernel(r_ref, w_ref, km_ref, kkn_ref, b_ref, vt_ref, s0_ref, *rest, bb):
    s_ref, yt_ref = rest[-2:]
    lane = _iota2((HD_A, H_A), 1)

    def body(i, carry):
        rb, wb, kb, kkb, bbv = r_ref[i], w_ref[i], km_ref[i], kkn_ref[i], b_ref[i]
        vt = vt_ref[i]
        H = range(H_A)
        Ss = [s0_ref[i, h] for h in H]
        sas = [-jnp.sum(Ss[h] * kkb[h:h + 1, :], axis=1, keepdims=True) for h in H]
        Sn = [Ss[h] * wb[h:h + 1, :] + sas[h] * bbv[h:h + 1, :] + vt[:, h:h + 1] * kb[h:h + 1, :] for h in H]
        for h in H:
            s_ref[i, h] = Sn[h]
        ycols = [jnp.sum(Sn[h] * rb[h:h + 1, :], axis=1, keepdims=True) for h in H]
        yt = jnp.zeros((HD_A, H_A), F32)
        for h in H:
            yt = jnp.where(lane == h, ycols[h], yt)
        yt_ref[i] = yt
        return carry

    lax.fori_loop(0, bb, body, 0)


def _rwkv_sample_post_kernel(y_ref, r_ref, km_ref, v_ref, g_ref, rk3_ref, lnw3_ref, lnb3_ref, ya_ref):
    ebd = _ebd()
    y3, r3, km3, v3 = (_to_tiles(x[...]) for x in (y_ref, r_ref, km_ref, v_ref))
    out3 = _rwkv_epilogue(y3, r3, km3, v3, rk3_ref[...], lnw3_ref[...], lnb3_ref[...], ebd)
    g = g_ref[...]
    for p in range(N_PAIR):
        ya_ref[:, LANES * p:LANES * (p + 1)] = out3[p] * g[:, LANES * p:LANES * (p + 1)]


def _gdn_sample_prep_kernel(qkv_ref, c0_ref, ab_ref, cw_ref, alog_ref, dtb_ref,
                            q_ref, k_ref, v_ref, eg_ref, beta_ref, cnew_ref):
    x = qkv_ref[...]
    conv = x * cw_ref[CONV_W - 1:CONV_W, :]
    for j in range(CONV_W - 1):
        conv = conv + c0_ref[j] * cw_ref[j:j + 1, :]
    for j in range(CONV_W - 2):
        cnew_ref[j] = c0_ref[j + 1]
    cnew_ref[CONV_W - 2] = x
    qkvc = _silu(conv)
    qs = _head_l2norm(qkvc[:, 0:D_MODEL], 1e-6)
    ks = _head_l2norm(qkvc[:, D_MODEL:2 * D_MODEL], 1e-6)
    for h in range(H_B):
        q_ref[:, LANES * h:LANES * (h + 1)] = qs[h] * (HD_B ** -0.5)
        k_ref[:, LANES * h:LANES * (h + 1)] = ks[h]
    v_ref[...] = qkvc[:, 2 * D_MODEL:3 * D_MODEL]
    g_all, beta_all = _gdn_gates(ab_ref[...], alog_ref[...], dtb_ref[...])
    eg_ref[...] = jnp.exp(_mm_sel_rhs(g_all, _sel_matrix(0), 3))
    beta_ref[...] = _mm_sel_rhs(beta_all, _sel_matrix(H_B), 3)


def _gdn_sample_state_kernel(qt_ref, kt_ref, v_ref, eg_ref, beta_ref, s0_ref, *rest, bb):
    s_ref, o_ref = rest[-2:]

    def body(i, carry):
        qt, kt = qt_ref[i], kt_ref[i]
        vb, egb, btb = v_ref[i], eg_ref[i], beta_ref[i]
        H = range(H_B)
        Sd = [s0_ref[i, h] * egb[h:h + 1, :] for h in H]
        ksr = [jnp.sum(Sd[h] * kt[:, h:h + 1], axis=0, keepdims=True) for h in H]
        v_new = [btb[h:h + 1, :] * (vb[h:h + 1, :] - ksr[h]) for h in H]
        Sn = [Sd[h] + kt[:, h:h + 1] * v_new[h] for h in H]
        for h in H:
            s_ref[i, h] = Sn[h]
        for h in H:
            o_ref[i, h:h + 1, :] = jnp.sum(Sn[h] * qt[:, h:h + 1], axis=0, keepdims=True)
        return carry

    lax.fori_loop(0, bb, body, 0)


def _gdn_sample_post_kernel(o_ref, z_ref, nw_ref, yb_ref):
    o = o_ref[...]
    z = z_ref[...]
    for h in range(H_B):
        sl = slice(LANES * h, LANES * (h + 1))
        oh = o[:, sl]
        on = oh * lax.rsqrt(jnp.mean(oh * oh, axis=-1, keepdims=True) + RMS_EPS) * nw_ref[...]
        yb_ref[:, sl] = on * _silu(z[:, sl])


def _params(sem):
    return pltpu.CompilerParams(dimension_semantics=sem, vmem_limit_bytes=VMEM_LIMIT)


def _full(shape):
    return pl.BlockSpec(shape, lambda *_: (0,) * len(shape))


_RESIDENT = pl.BlockSpec(memory_space=pltpu.VMEM)


def _row_tile(t, cap):
    tm = min(t, cap)
    assert t % tm == 0
    return tm


def _mod_spec(mod):
    rows = mod.shape[2]
    if rows == 1:
        return pl.BlockSpec((3 * N_SUB, 1, 1, D_MODEL), lambda b, i: (0, b, 0, 0))
    return pl.BlockSpec((3 * N_SUB, 1, rows, D_MODEL), lambda b, i: (0, 0, i, 0))


def _ada_call(c_all, w_ada, b_ada):
    depth = w_ada.shape[0]
    n = c_all.shape[0]
    return pl.pallas_call(
        _ada_kernel,
        grid=(depth, 3 * N_SUB),
        in_specs=[pl.BlockSpec((n, D_MODEL), lambda l, j: (0, 0)),
                  pl.BlockSpec((None, D_MODEL, D_MODEL), lambda l, j: (l, 0, j)),
                  pl.BlockSpec((None, 1, D_MODEL), lambda l, j: (l, 0, j))],
        out_specs=pl.BlockSpec((None, None, n, D_MODEL), lambda l, j: (l, j, 0, 0)),
        out_shape=jax.ShapeDtypeStruct((depth, 3 * N_SUB, n, D_MODEL), F32),
        compiler_params=_params(("arbitrary", "arbitrary")),
        name="ada_mod",
    )(c_all, w_ada, b_ada.reshape(depth, 1, 3 * N_SUB * D_MODEL))


def _ffn_call(x, mod, npre, npost, up, down, sub):
    b, t, _ = x.shape
    tm = _row_tile(t, 512)
    xspec = pl.BlockSpec((1, tm, D_MODEL), lambda b, i: (b, i, 0))
    return pl.pallas_call(
        functools.partial(_ffn_kernel, sub=sub),
        grid=(b, t // tm),
        in_specs=[xspec, _mod_spec(mod), _full((1, D_MODEL)), _full((1, D_MODEL)), _RESIDENT, _RESIDENT],
        out_specs=xspec,
        out_shape=jax.ShapeDtypeStruct(x.shape, F32),
        compiler_params=_params(("arbitrary", "arbitrary")),
        name="ffn",
    )(x, mod, npre, npost, up, down)


def _inproj_call(x, mod, npre, w):
    b, t, _ = x.shape
    tm = _row_tile(t, 256)
    widths = (A_COLS, 3 * D_MODEL, D_MODEL, 2 * D_MODEL, AB_COLS)
    spec = lambda n: pl.BlockSpec((1, tm, n), lambda b, i: (b, i, 0))
    return pl.pallas_call(
        _inproj_kernel,
        grid=(b, t // tm),
        in_specs=[spec(D_MODEL), _mod_spec(mod), _full((1, D_MODEL)), _RESIDENT],
        out_specs=[spec(n) for n in widths],
        out_shape=[jax.ShapeDtypeStruct((b, t, n), F32) for n in widths],
        compiler_params=_params(("arbitrary", "arbitrary")),
        name="in_proj",
    )(x, mod, npre, w)


def _outproj_call(x, ya, yb, pg, mod, npost, w):
    b, t, _ = x.shape
    tm = _row_tile(t, 512)
    spec = lambda n: pl.BlockSpec((1, tm, n), lambda b, i: (b, i, 0))
    return pl.pallas_call(
        _outproj_kernel,
        grid=(b, t // tm),
        in_specs=[spec(D_MODEL), spec(D_MODEL), spec(D_MODEL), spec(2 * D_MODEL), _mod_spec(mod),
                  _full((1, D_MODEL)), _RESIDENT],
        out_specs=spec(D_MODEL),
        out_shape=jax.ShapeDtypeStruct(x.shape, F32),
        compiler_params=_params(("arbitrary", "arbitrary")),
        name="out_proj",
    )(x, ya, yb, pg, mod, npost, w)


def _rwkv_prompt_call(pa, rp):
    b, t, _ = pa.shape
    C = RW_C
    assert t % C == 0
    vec = _full((1, D_MODEL))
    lora = _full((2, LANES, D_MODEL))
    tile3 = _full((N_PAIR, 1, LANES))
    return pl.pallas_call(
        _rwkv_prompt_kernel,
        grid=(b, t // C),
        in_specs=[pl.BlockSpec((1, C, A_COLS), lambda b, i: (b, i, 0)), _full((1, A_COLS)),
                  vec, lora, vec, lora, lora, vec, vec, tile3, tile3, tile3],
        out_specs=[pl.BlockSpec((1, C, D_MODEL), lambda b, i: (b, i, 0)),
                   pl.BlockSpec((1, H_A, HD_A, HD_A), lambda b, i: (b, 0, 0, 0)),
                   pl.BlockSpec((1, 1, A_COLS), lambda b, i: (b, 0, 0))],
        out_shape=[jax.ShapeDtypeStruct((b, t, D_MODEL), F32),
                   jax.ShapeDtypeStruct((b, H_A, HD_A, HD_A), F32),
                   jax.ShapeDtypeStruct((b, 1, A_COLS), F32)],
        scratch_shapes=[pltpu.VMEM((1, A_COLS), F32), pltpu.VMEM((N_PAIR, LANES, LANES), F32)],
        compiler_params=_params(("arbitrary", "arbitrary")),
        name="rwkv_prompt",
    )(pa, rp["mu"], rp["w0"], rp["w2p"], rp["a0"], rp["a2p"], rp["g2"], rp["k_k"], rp["k_a"],
      rp["rk3"], rp["lnw3"], rp["lnb3"])


def _gdn_prompt_call(qkv, ab, z, gp):
    b, t, _ = qkv.shape
    C = GD_C
    assert t % C == 0
    spec = lambda n: pl.BlockSpec((1, C, n), lambda b, i: (b, i, 0))
    return pl.pallas_call(
        _gdn_prompt_kernel,
        grid=(b, t // C),
        in_specs=[spec(3 * D_MODEL), spec(AB_COLS), spec(D_MODEL), _full((CONV_W, 3 * D_MODEL)),
                  _full((1, LANES)), _full((1, LANES)), _full((1, HD_B))],
        out_specs=[spec(D_MODEL),
                   pl.BlockSpec((1, CONV_W - 1, 3 * D_MODEL), lambda b, i: (b, 0, 0)),
                   pl.BlockSpec((1, H_B, HD_B, HD_B), lambda b, i: (b, 0, 0, 0))],
        out_shape=[jax.ShapeDtypeStruct((b, t, D_MODEL), F32),
                   jax.ShapeDtypeStruct((b, CONV_W - 1, 3 * D_MODEL), F32),
                   jax.ShapeDtypeStruct((b, H_B, HD_B, HD_B), F32)],
        scratch_shapes=[pltpu.VMEM((8, 3 * D_MODEL), F32)],
        compiler_params=_params(("arbitrary", "arbitrary")),
        name="gdn_prompt",
    )(qkv, ab, z, gp["conv_w"], gp["alog"], gp["dtb"], gp["norm_w"])


def _rwkv_sample(pa, shift0, wkv_all, wkv_new, l, rp):
    n = pa.shape[0]
    vec = _full((1, D_MODEL))
    lora = _full((2, LANES, D_MODEL))
    flat = _full((n, D_MODEL))
    r, w, km, v, kkn, bb_, g = pl.pallas_call(
        _rwkv_sample_prep_kernel,
        grid=(1,),
        in_specs=[_full((n, A_COLS)), _full((n, A_COLS)), _full((1, A_COLS)), vec, lora, vec, lora, lora,
                  vec, vec],
        out_specs=[flat] * 7,
        out_shape=[jax.ShapeDtypeStruct((n, D_MODEL), F32)] * 7,
        compiler_params=_params(("arbitrary",)),
        name="rwkv_sample_prep",
    )(pa, shift0, rp["mu"], rp["w0"], rp["w2p"], rp["a0"], rp["a2p"], rp["g2"], rp["k_k"], rp["k_a"])

    bb = 8
    assert n % bb == 0
    heads = lambda x: x.reshape(n, H_A, HD_A)
    hspec = pl.BlockSpec((bb, H_A, HD_A), lambda i: (i, 0, 0))
    tspec = pl.BlockSpec((bb, HD_A, H_A), lambda i: (i, 0, 0))
    sspec = pl.BlockSpec((None, bb, H_A, HD_A, HD_A), lambda i: (l, i, 0, 0, 0))
    carried = [] if wkv_new is None else [wkv_new]
    s_new, yt = pl.pallas_call(
        functools.partial(_rwkv_sample_state_kernel, bb=bb),
        grid=(n // bb,),
        in_specs=[hspec] * 5 + [tspec, sspec] + [pl.BlockSpec(memory_space=pl.ANY)] * len(carried),
        out_specs=[sspec, tspec],
        out_shape=[jax.ShapeDtypeStruct(wkv_all.shape, F32), jax.ShapeDtypeStruct((n, HD_A, H_A), F32)],
        input_output_aliases={7: 0} if carried else {},
        compiler_params=_params(("arbitrary",)),
        name="rwkv_sample_state",
    )(heads(r), heads(w), heads(km), heads(kkn), heads(bb_), jnp.swapaxes(heads(v), 1, 2), wkv_all, *carried)
    y = jnp.swapaxes(yt, 1, 2).reshape(n, D_MODEL)

    tile3 = _full((N_PAIR, 1, LANES))
    ya = pl.pallas_call(
        _rwkv_sample_post_kernel,
        grid=(1,),
        in_specs=[flat] * 5 + [tile3] * 3,
        out_specs=flat,
        out_shape=jax.ShapeDtypeStruct((n, D_MODEL), F32),
        compiler_params=_params(("arbitrary",)),
        name="rwkv_sample_post",
    )(y, r, km, v, g, rp["rk3"], rp["lnw3"], rp["lnb3"])
    return ya, s_new


def _gdn_sample(qkv, conv0, ssm_all, ssm_new, l, ab, z, gp):
    n = qkv.shape[0]
    flat = _full((n, D_MODEL))
    cspec = _full((CONV_W - 1, n, 3 * D_MODEL))
    q, k, v, eg, beta, cnew = pl.pallas_call(
        _gdn_sample_prep_kernel,
        grid=(1,),
        in_specs=[_full((n, 3 * D_MODEL)), cspec, _full((n, AB_COLS)), _full((CONV_W, 3 * D_MODEL)),
                  _full((1, LANES)), _full((1, LANES))],
        out_specs=[flat] * 5 + [cspec],
        out_shape=[jax.ShapeDtypeStruct((n, D_MODEL), F32)] * 5
        + [jax.ShapeDtypeStruct((CONV_W - 1, n, 3 * D_MODEL), F32)],
        compiler_params=_params(("arbitrary",)),
        name="gdn_sample_prep",
    )(qkv, jnp.swapaxes(conv0, 0, 1), ab, gp["conv_w"], gp["alog"], gp["dtb"])

    bb = 4
    assert n % bb == 0
    heads = lambda x: x.reshape(n, H_B, HD_B)
    hspec = pl.BlockSpec((bb, H_B, HD_B), lambda i: (i, 0, 0))
    tspec = pl.BlockSpec((bb, HD_B, H_B), lambda i: (i, 0, 0))
    sspec = pl.BlockSpec((None, bb, H_B, HD_B, HD_B), lambda i: (l, i, 0, 0, 0))
    carried = [] if ssm_new is None else [ssm_new]
    s_new, o = pl.pallas_call(
        functools.partial(_gdn_sample_state_kernel, bb=bb),
        grid=(n // bb,),
        in_specs=[tspec, tspec, hspec, hspec, hspec, sspec] + [pl.BlockSpec(memory_space=pl.ANY)] * len(carried),
        out_specs=[sspec, hspec],
        out_shape=[jax.ShapeDtypeStruct(ssm_all.shape, F32), jax.ShapeDtypeStruct((n, H_B, HD_B), F32)],
        input_output_aliases={6: 0} if carried else {},
        compiler_params=_params(("arbitrary",)),
        name="gdn_sample_state",
    )(jnp.swapaxes(heads(q), 1, 2), jnp.swapaxes(heads(k), 1, 2), heads(v), heads(eg), heads(beta), ssm_all,
      *carried)

    yb = pl.pallas_call(
        _gdn_sample_post_kernel,
        grid=(1,),
        in_specs=[flat, flat, _full((1, HD_B))],
        out_specs=flat,
        out_shape=jax.ShapeDtypeStruct((n, D_MODEL), F32),
        compiler_params=_params(("arbitrary",)),
        name="gdn_sample_post",
    )(o.reshape(n, D_MODEL), z, gp["norm_w"])
    return yb, jnp.swapaxes(cnew, 0, 1), s_new


def _hi_lo(w):
    hi = w.astype(BF16)
    return jnp.stack([hi, (w - hi.astype(F32)).astype(BF16)])


def _pad_rows(x, rows, at):
    out = jnp.zeros((rows, x.shape[1]), x.dtype)
    return lax.dynamic_update_slice(out, x, (at, 0))


def kernel(x_prompt, x_sample, c_prompt, c_sample, state_rwkv_shift, state_rwkv_wkv, state_gdn_conv, state_gdn_ssm, w_ada, b_ada, norm_pre, norm_post, ffn_up, ffn_down, w_in, w_out, rwkv_mu, rwkv_w0, rwkv_w2, rwkv_a0, rwkv_a2, rwkv_g2, rwkv_k_k, rwkv_k_a, rwkv_r_k, rwkv_ln_w, rwkv_ln_b, gdn_conv, gdn_a_log, gdn_dt_bias, gdn_norm_w):
    depth = w_ada.shape[0]
    bp = x_prompt.shape[0]
    ns = x_sample.shape[0]

    n_c = bp + ns
    n_cp = -(-n_c // 16) * 16
    c_all = jnp.concatenate([c_prompt, c_sample, jnp.zeros((n_cp - n_c, D_MODEL), F32)], axis=0)
    mod = _ada_call(c_all, w_ada, b_ada)

    o_b = A_COLS
    o_ab = o_b + 3 * D_MODEL
    o_z = o_ab + 2 * H_B
    o_g = o_z + D_MODEL
    w_in_r = jnp.concatenate(
        [w_in[:, :, 0:o_b], w_in[:, :, o_b:o_ab], w_in[:, :, o_z:o_g], w_in[:, :, o_g:],
         w_in[:, :, o_ab:o_z], jnp.zeros((depth, D_MODEL, AB_COLS - 2 * H_B), F32)], axis=-1).astype(BF16)
    up_b = ffn_up.astype(BF16)
    down_b = ffn_down.astype(BF16)
    w_out_b = w_out.astype(BF16)

    yp = x_prompt
    ys = x_sample.reshape(1, ns, D_MODEL)
    outs = [[] for _ in range(6)]
    s_wkv = s_ssm = None
    for l in range(depth):
        mod_p = mod[l, :, 0:bp].reshape(3 * N_SUB, bp, 1, D_MODEL)
        mod_s = mod[l, :, bp:bp + ns].reshape(3 * N_SUB, 1, ns, D_MODEL)
        npre = [norm_pre[l, i][None] for i in range(N_SUB)]
        npost = [norm_post[l, i][None] for i in range(N_SUB)]
        row = lambda x: x[l][None]
        tile3 = lambda x: x[l].reshape(N_PAIR, 1, LANES)
        rp = dict(mu=row(rwkv_mu), w0=row(rwkv_w0), a0=row(rwkv_a0), k_k=row(rwkv_k_k), k_a=row(rwkv_k_a),
                  w2p=_hi_lo(_pad_rows(rwkv_w2[l], LANES, 0)), a2p=_hi_lo(_pad_rows(rwkv_a2[l], LANES, W_LORA)),
                  g2=_hi_lo(rwkv_g2[l]), rk3=tile3(rwkv_r_k), lnw3=tile3(rwkv_ln_w), lnb3=tile3(rwkv_ln_b))
        lane_row = lambda x: jnp.zeros((1, LANES), F32).at[0, 0:H_B].set(x[l])
        gp = dict(conv_w=gdn_conv[l], norm_w=row(gdn_norm_w), alog=lane_row(gdn_a_log),
                  dtb=lane_row(gdn_dt_bias))

        yp = _ffn_call(yp, mod_p, npre[0], npost[0], up_b[l, 0], down_b[l, 0], 0)
        pa, qkv, z, pg, ab = _inproj_call(yp, mod_p, npre[1], w_in_r[l])
        ya, p_wkv, p_shift = _rwkv_prompt_call(pa, rp)
        yb, p_conv, p_ssm = _gdn_prompt_call(qkv, ab, z, gp)
        yp = _outproj_call(yp, ya, yb, pg, mod_p, npost[1], w_out_b[l])
        yp = _ffn_call(yp, mod_p, npre[2], npost[2], up_b[l, 1], down_b[l, 1], 2)

        ys = _ffn_call(ys, mod_s, npre[0], npost[0], up_b[l, 0], down_b[l, 0], 0)
        pa, qkv, z, pg, ab = _inproj_call(ys, mod_s, npre[1], w_in_r[l])
        ya, s_wkv = _rwkv_sample(pa[0], state_rwkv_shift[l], state_rwkv_wkv, s_wkv, l, rp)
        yb, s_conv, s_ssm = _gdn_sample(qkv[0], state_gdn_conv[l], state_gdn_ssm, s_ssm, l, ab[0], z[0], gp)
        ys = _outproj_call(ys, ya[None], yb[None], pg, mod_s, npost[1], w_out_b[l])
        ys = _ffn_call(ys, mod_s, npre[2], npost[2], up_b[l, 1], down_b[l, 1], 2)

        for lst, val in zip(outs, (p_shift[:, 0], p_wkv, p_conv, p_ssm, pa[0], s_conv)):
            lst.append(val)

    st = [jnp.stack(o) for o in outs]
    return (yp, ys.reshape(ns, 1, D_MODEL), st[0], st[1], st[2], st[3], st[4], s_wkv, st[5], s_ssm)
```

```python
import functools

import jax
import jax.numpy as jnp
from jax import lax
from jax.experimental import pallas as pl
from jax.experimental.pallas import tpu as pltpu

F32 = jnp.float32
BF16 = jnp.bfloat16

D_MODEL = 1024
HD_A = 64
H_A = D_MODEL // HD_A
W_LORA = 64
A_LORA = 64
G_LORA = 128
A_COLS = 3 * D_MODEL + W_LORA + A_LORA + G_LORA
RWKV_GN_EPS = 6.4e-4
HD_B = 128
H_B = D_MODEL // HD_B
CONV_W = 4
D_FF = 2816
N_SUB = 3
FFN_RES = 0.5
RMS_EPS = 1e-6

LANES = 128
N_PAIR = D_MODEL // LANES
RW_C = 64
RW_ROWS = 2
GD_C = 128
AB_COLS = LANES
IN_COLS_R = A_COLS + 3 * D_MODEL + D_MODEL + 2 * D_MODEL + AB_COLS
VMEM_LIMIT = 56 * 1024 * 1024
REC_PASSES = 1

_NN = (((1,), (0,)), ((), ()))
_NT = (((1,), (1,)), ((), ()))


def _dg(a, b, dn=_NN):
    return lax.dot_general(a, b, dn, preferred_element_type=F32)


def _split2(x):
    hi = x.astype(BF16)
    lo = (x - hi.astype(F32)).astype(BF16)
    return hi, lo


def _split3(x):
    h1 = x.astype(BF16)
    r1 = x - h1.astype(F32)
    h2 = r1.astype(BF16)
    h3 = (r1 - h2.astype(F32)).astype(BF16)
    return h1, h2, h3


def _mm(a, b, dn=_NN, passes=REC_PASSES):
    if passes == 1:
        return _dg(a.astype(BF16), b.astype(BF16), dn)
    ah, al = _split2(a)
    bh, bl = _split2(b)
    return _dg(ah, bh, dn) + (_dg(ah, bl, dn) + _dg(al, bh, dn))


def _mm_sel_rhs(x, e, pieces):
    parts = _split3(x) if pieces == 3 else _split2(x)
    out = _dg(parts[0], e)
    for p in parts[1:]:
        out = out + _dg(p, e)
    return out


def _mm_sel_lhs(e, x, pieces):
    parts = _split3(x) if pieces == 3 else _split2(x)
    out = _dg(e, parts[0])
    for p in parts[1:]:
        out = out + _dg(e, p)
    return out


def _iota2(shape, dim):
    return lax.broadcasted_iota(jnp.int32, shape, dim)


def _softplus(x):
    return jnp.maximum(x, 0.0) + jnp.log(1.0 + jnp.exp(-jnp.abs(x)))


def _silu(x):
    return x * jax.nn.sigmoid(x)


def _rms(x, w):
    return x * lax.rsqrt(jnp.mean(x * x, axis=-1, keepdims=True) + RMS_EPS) * w


def _mod_in(x, npre, mod_ref, sub):
    shift = mod_ref[3 * sub, 0]
    scale = mod_ref[3 * sub + 1, 0]
    return _rms(x, npre) * (1.0 + scale) + shift


def _tri_masks(n, top):
    i = _iota2((n, n), 0)
    j = _iota2((n, n), 1)
    masks = []
    s = 1
    while s <= top:
        sh = s.bit_length() - 1
        same = (i >> (sh + 1)) == (j >> (sh + 1))
        masks.append(same & (((i >> sh) & 1) == 1) & (((j >> sh) & 1) == 0))
        s *= 2
    return masks


def _tri_inv(Ls, masks):
    n = Ls[0].shape[0]
    eye = jnp.where(_iota2((n, n), 0) == _iota2((n, n), 1), 1.0, 0.0).astype(F32)
    Xs = [eye - jnp.where(masks[0], L, 0.0) for L in Ls]
    Lb = [L.astype(BF16) for L in Ls]
    zero = jnp.zeros((), BF16)
    for m in masks[1:]:
        Xb = [X.astype(BF16) for X in Xs]
        Ts = [_dg(xb, jnp.where(m, lb, zero)) for xb, lb in zip(Xb, Lb)]
        Xs = [X - _dg(T.astype(BF16), xb) for X, T, xb in zip(Xs, Ts, Xb)]
    return Xs


def _to_tiles(x):
    return jnp.stack([x[:, LANES * p:LANES * (p + 1)] for p in range(N_PAIR)], axis=0)


def _pair_headsum(x3, ebd):
    p, n, l = x3.shape
    return _mm_sel_rhs(x3.reshape(p * n, l), ebd, 2).reshape(p, n, l)


def _ebd():
    i = _iota2((LANES, LANES), 0)
    j = _iota2((LANES, LANES), 1)
    return jnp.where((i >> 6) == (j >> 6), 1.0, 0.0).astype(BF16)


def _mm_lora(a, w_ref):
    ah, al = _split2(a)
    return _dg(ah, w_ref[0]) + (_dg(ah, w_ref[1]) + _dg(al, w_ref[0]))


def _rwkv_prep(pa, prev, mu, w0, w2p, a0, a2p, g2, k_k, k_a):
    xs = pa + mu * (prev - pa)
    r = xs[:, 0:D_MODEL]
    k = xs[:, D_MODEL:2 * D_MODEL]
    v = xs[:, 2 * D_MODEL:3 * D_MODEL]
    wa = xs[:, 3 * D_MODEL:3 * D_MODEL + LANES]
    gd = xs[:, 3 * D_MODEL + LANES:A_COLS]
    w_lin = _mm_lora(jnp.tanh(wa), w2p)
    a_lin = _mm_lora(wa, a2p)
    g = _mm_lora(jax.nn.sigmoid(gd), g2)
    w_log = -_softplus(-(w0 + w_lin)) - 0.5
    lw = -jnp.exp(w_log)
    a = jax.nn.sigmoid(a0 + a_lin)
    kkr = k * k_k
    km = k * (1.0 + (a - 1.0) * k_a)
    return r, km, v, kkr, a, lw, g


def _rwkv_epilogue(y3, r3, km3, v3, rk3, lnw3, lnb3, ebd):
    mean = _pair_headsum(y3, ebd) * (1.0 / HD_A)
    yc = y3 - mean
    var = _pair_headsum(yc * yc, ebd) * (1.0 / HD_A)
    yn = yc * lax.rsqrt(var + RWKV_GN_EPS) * lnw3 + lnb3
    bonus = _pair_headsum(r3 * km3 * rk3, ebd) * v3
    return yn + bonus


def _gdn_gates(ab, alog, dtb):
    g_all = -jnp.exp(alog) * _softplus(ab + dtb)
    beta_all = jax.nn.sigmoid(ab)
    return g_all, beta_all


def _sel_matrix(offset):
    i = _iota2((LANES, D_MODEL), 0)
    j = _iota2((LANES, D_MODEL), 1)
    return jnp.where(i == (j >> 7) + offset, 1.0, 0.0).astype(BF16)


def _head_l2norm(x, eps):
    outs = []
    for h in range(H_B):
        xh = x[:, LANES * h:LANES * (h + 1)]
        outs.append(xh * lax.rsqrt(jnp.sum(xh * xh, axis=-1, keepdims=True) + eps))
    return outs


def _ada_kernel(c_ref, w_ref, b_ref, o_ref):
    s = _silu(c_ref[...]).astype(BF16)
    o_ref[...] = _dg(s, w_ref[...].astype(BF16)) + b_ref[...]


def _ffn_kernel(x_ref, mod_ref, npre_ref, npost_ref, up_ref, down_ref, o_ref, *, sub):
    x = x_ref[0]
    h = _mod_in(x, npre_ref[...], mod_ref, sub).astype(BF16)
    gate = _dg(h, up_ref[:, 0:D_FF])
    val = _dg(h, up_ref[:, D_FF:2 * D_FF])
    act = (_silu(gate) * val).astype(BF16)
    y = _dg(act, down_ref[...])
    o_ref[0] = x + FFN_RES * mod_ref[3 * sub + 2, 0] * _rms(y, npost_ref[...])


def _inproj_kernel(x_ref, mod_ref, npre_ref, w_ref, pa_ref, qkv_ref, z_ref, pg_ref, ab_ref):
    h = _mod_in(x_ref[0], npre_ref[...], mod_ref, 1).astype(BF16)
    o = 0
    for ref in (pa_ref, qkv_ref, z_ref, pg_ref, ab_ref):
        n = ref.shape[-1]
        ref[0] = _dg(h, w_ref[:, o:o + n])
        o += n


def _outproj_kernel(x_ref, ya_ref, yb_ref, pg_ref, mod_ref, npost_ref, w_ref, o_ref):
    pg = pg_ref[0]
    m = jax.nn.sigmoid(pg[:, 0:D_MODEL]) * ya_ref[0] + jax.nn.sigmoid(pg[:, D_MODEL:]) * yb_ref[0]
    y = _dg(m.astype(BF16), w_ref[...])
    o_ref[0] = x_ref[0] + mod_ref[3 * 1 + 2, 0] * _rms(y, npost_ref[...])


def _rwkv_prompt_kernel(pa_ref, mu_ref, w0_ref, w2_ref, a0_ref, a2_ref, g2_ref, kk_ref, ka_ref,
                        rk3_ref, lnw3_ref, lnb3_ref, ya_ref, s_ref, shift_ref, carry_ref, sbd_ref):
    C = RW_C
    R = RW_ROWS
    N = R * C
    t = pl.program_id(1)

    @pl.when(t == 0)
    def _():
        carry_ref[...] = jnp.zeros_like(carry_ref)
        sbd_ref[...] = jnp.zeros_like(sbd_ref)

    pa = pa_ref[...].reshape(N, A_COLS)
    last = jnp.concatenate([jnp.broadcast_to(carry_ref[s:s + 1, :], (C, A_COLS)) for s in range(R)], axis=0)
    prev = jnp.where((_iota2((N, 1), 0) & (C - 1)) == 0, last, pltpu.roll(pa, 1, 0))
    for s in range(R):
        carry_ref[s:s + 1, :] = pa[s * C + C - 1:s * C + C]
    r, km, v, kkr, a, lw, g = _rwkv_prep(pa, prev, mu_ref[...], w0_ref[...], w2_ref, a0_ref[...],
                                         a2_ref, g2_ref, kk_ref[...], ka_ref[...])

    ti = _iota2((N, N), 0)
    tj = _iota2((N, N), 1)
    csh = C.bit_length() - 1
    tri = jnp.where((ti >= tj) & ((ti >> csh) == (tj >> csh)), 1.0, 0.0).astype(BF16)
    cum = _mm_sel_lhs(tri, lw, 3)
    ebd = _ebd()
    NQ = N_PAIR * R
    tiles = lambda x: _to_tiles(x).reshape(NQ, C, LANES)
    r3, km3, v3, kkr3, a3, lw3, cum3 = (tiles(x) for x in (r, km, v, kkr, a, lw, cum))
    kk3 = kkr3 * lax.rsqrt(_pair_headsum(kkr3 * kkr3, ebd) + 1e-12)
    b3 = kk3 * a3
    cl = cum3[:, C - 1:C, :]
    e_neg = jnp.exp(-cum3)
    e_end = jnp.exp(cl - cum3)
    rh3 = r3 * jnp.exp(cum3)
    ah3 = kk3 * jnp.exp(cum3 - lw3)
    bt3 = b3 * e_neg
    kt3 = km3 * e_neg
    be3 = b3 * e_end
    ke3 = km3 * e_end
    e_last3 = jnp.exp(cl)

    lane0 = _iota2((1, LANES), 1) < HD_A

    def sm(x):
        return jnp.concatenate([jnp.where(lane0, x, 0.0), jnp.where(lane0, 0.0, x)], axis=0)

    ii = _iota2((2 * C, 2 * C), 0)
    jj = _iota2((2 * C, 2 * C), 1)
    strict = jj < ii
    incl = jj <= ii
    masks = _tri_masks(2 * C, C // 2)

    P = range(NQ)
    T2 = 2 * C
    bsm = lambda x: sm(x).astype(BF16)
    ARs = [jnp.concatenate([bsm(ah3[p]), bsm(rh3[p])], axis=0) for p in P]
    BKs = [jnp.concatenate([bsm(bt3[p]), bsm(kt3[p])], axis=0) for p in P]
    BKe = [jnp.concatenate([bsm(be3[p]), bsm(ke3[p])], axis=0) for p in P]
    Vs = [sm(v3[p]) for p in P]
    Vb = [V.astype(BF16) for V in Vs]
    Ss = [sbd_ref[p] for p in P]
    QQ = [_dg(ARs[p], BKs[p], _NT) for p in P]
    QS = [_dg(ARs[p], Ss[p].astype(BF16), _NT) for p in P]
    Lab = [jnp.where(strict, Q[0:T2, 0:T2], 0.0) for Q in QQ]
    Lak = [jnp.where(strict, Q[0:T2, T2:2 * T2], 0.0).astype(BF16) for Q in QQ]
    Ms = [QS[p][0:T2] + _dg(Lak[p], Vb[p]) for p in P]
    Tinv = _tri_inv(Lab, masks)
    Us = [-_dg(Tinv[p].astype(BF16), Ms[p].astype(BF16)) for p in P]
    UVb = [jnp.concatenate([Us[p].astype(BF16), Vb[p]], axis=0) for p in P]
    incl2 = jnp.concatenate([incl, incl], axis=1)
    Arbk = [jnp.where(incl2, Q[T2:2 * T2, :], 0.0).astype(BF16) for Q in QQ]
    Ys = [QS[p][T2:2 * T2] + _dg(Arbk[p], UVb[p]) for p in P]
    for p in P:
        UVt = jnp.concatenate([Us[p].T, Vs[p].T], axis=1).astype(BF16)
        sbd_ref[p] = Ss[p] * e_last3[p] + _dg(UVt, BKe[p])
    y3 = jnp.stack([Y[0:C] + Y[C:2 * C] for Y in Ys], axis=0)

    out3 = _rwkv_epilogue(y3, r3, km3, v3, rk3_ref[...], lnw3_ref[...], lnb3_ref[...], ebd)
    for q in P:
        p, s = divmod(q, R)
        ya_ref[s, :, LANES * p:LANES * (p + 1)] = out3[q] * g[s * C:(s + 1) * C, LANES * p:LANES * (p + 1)]

    @pl.when(t == pl.num_programs(1) - 1)
    def _():
        for s in range(R):
            shift_ref[s] = pa[s * C + C - 1:s * C + C]
        for q in P:
            p, s = divmod(q, R)
            S = sbd_ref[q]
            s_ref[s, 2 * p] = S[0:HD_A, 0:HD_A]
            s_ref[s, 2 * p + 1] = pltpu.roll(S, HD_A, 1)[HD_A:LANES, 0:HD_A]


def _gdn_prompt_kernel(qkv_ref, ab_ref, z_ref, cw_ref, alog_ref, dtb_ref, nw_ref,
                       yb_ref, conv_ref, ssm_ref, carry_ref):
    C = GD_C
    t = pl.program_id(1)

    @pl.when(t == 0)
    def _():
        carry_ref[...] = jnp.zeros_like(carry_ref)
        ssm_ref[...] = jnp.zeros_like(ssm_ref)

    x = qkv_ref[0]
    ext = jnp.concatenate([carry_ref[...], x], axis=0)
    conv = x * cw_ref[CONV_W - 1:CONV_W, :]
    for s in range(1, CONV_W):
        conv = conv + pltpu.roll(ext, s, 0)[8:] * cw_ref[CONV_W - 1 - s:CONV_W - s, :]
    carry_ref[...] = x[C - 8:C]
    qkvc = _silu(conv)
    qs = _head_l2norm(qkvc[:, 0:D_MODEL], 1e-6)
    ks = _head_l2norm(qkvc[:, D_MODEL:2 * D_MODEL], 1e-6)
    vv = qkvc[:, 2 * D_MODEL:3 * D_MODEL]

    g_all, beta_all = _gdn_gates(ab_ref[0], alog_ref[...], dtb_ref[...])
    tri = jnp.where(_iota2((C, C), 0) >= _iota2((C, C), 1), 1.0, 0.0).astype(BF16)
    gc_all = _mm_sel_lhs(tri, g_all, 3)

    ii = _iota2((C, C), 0)
    jj = _iota2((C, C), 1)
    strict = jj < ii
    incl = jj <= ii
    masks = _tri_masks(C, C // 2)
    z = z_ref[0]

    H = range(H_B)
    sls = [slice(LANES * h, LANES * (h + 1)) for h in H]
    qh = [qs[h] * (HD_B ** -0.5) for h in H]
    bcast = lambda col: jnp.broadcast_to(col, (C, LANES))
    gh = [bcast(gc_all[:, h:h + 1]) for h in H]
    bh = [bcast(beta_all[:, H_B + h:H_B + h + 1]) for h in H]
    gct = gc_all.T
    dec = [jnp.exp(gh[h] - gct[h:h + 1, :]) for h in H]
    eg = [jnp.exp(g) for g in gh]
    glast = [g[C - 1:C, :] for g in gh]
    kb = [ks[h] * bh[h] for h in H]
    KQ = [_dg(jnp.concatenate([kb[h], qh[h]], axis=0).astype(BF16), ks[h].astype(BF16), _NT) for h in H]
    Ls = [jnp.where(strict, KQ[h][0:C] * dec[h], 0.0) for h in H]
    attn = [jnp.where(incl, KQ[h][C:2 * C] * dec[h], 0.0) for h in H]
    Tinv = _tri_inv(Ls, masks)
    rhs = [jnp.concatenate([vv[:, sls[h]] * bh[h], kb[h] * eg[h]], axis=1).astype(BF16) for h in H]
    UW = [_dg(Tinv[h].astype(BF16), rhs[h]) for h in H]
    Ss = [ssm_ref[0, h] for h in H]
    Sb = [S.astype(BF16) for S in Ss]
    v_new = [UW[h][:, 0:LANES] - _dg(UW[h][:, LANES:2 * LANES].astype(BF16), Sb[h]) for h in H]
    vnb = [v.astype(BF16) for v in v_new]
    os_ = [_dg(jnp.concatenate([qh[h] * eg[h], attn[h]], axis=1).astype(BF16),
               jnp.concatenate([Sb[h], vnb[h]], axis=0)) for h in H]
    for h in H:
        k_dec = ks[h] * jnp.exp(glast[h] - gh[h])
        ssm_ref[0, h] = Ss[h] * jnp.exp(glast[h]) + _dg(k_dec.T.astype(BF16), vnb[h])
    for h in H:
        o = os_[h]
        on = o * lax.rsqrt(jnp.mean(o * o, axis=-1, keepdims=True) + RMS_EPS) * nw_ref[...]
        yb_ref[0, :, sls[h]] = on * _silu(z[:, sls[h]])

    @pl.when(t == pl.num_programs(1) - 1)
    def _():
        conv_ref[0] = pltpu.roll(x, CONV_W - 1, 0)[0:CONV_W - 1]


def _rwkv_sample_prep_kernel(pa_ref, prev_ref, mu_ref, w0_ref, w2_ref, a0_ref, a2_ref, g2_ref, kk_ref,
                             ka_ref, r_ref, w_ref, km_ref, v_ref, kkn_ref, b_ref, g_ref):
    r, km, v, kkr, a, lw, g = _rwkv_prep(pa_ref[...], prev_ref[...], mu_ref[...], w0_ref[...], w2_ref,
                                         a0_ref[...], a2_ref, g2_ref, kk_ref[...], ka_ref[...])
    n = r.shape[0]
    heads = lambda x: x.reshape(H_A, HD_A, n)
    kk3 = heads(kkr.T)
    kkn = kk3 * lax.rsqrt(jnp.sum(kk3 * kk3, axis=1, keepdims=True) + 1e-12)
    r_ref[...] = r.T
    w_ref[...] = jnp.exp(lw).T
    km_ref[...] = km.T
    v_ref[...] = v.T
    kkn_ref[...] = kkn.reshape(D_MODEL, n)
    b_ref[...] = (kkn * heads(a.T)).reshape(D_MODEL, n)
    g_ref[...] = g.T


def _rwkv_sample_state_kernel(r_ref, w_ref, km_ref, kkn_ref, b_ref, v_ref, s0_ref, *rest):
    s_ref, y_ref = rest[-2:]
    rh, wh, kh, kkh, bh = r_ref[...], w_ref[...], km_ref[...], kkn_ref[...], b_ref[...]
    G = 8
    row = _iota2((G, 1), 0)

    def body(j, carry):
        base = pl.multiple_of(j * G, G)
        vrows = v_ref[pl.ds(base, G), :]
        U = range(G)
        Ss = [s0_ref[base + u] for u in U]
        sas = [-jnp.sum(Ss[u] * kkh, axis=0, keepdims=True) for u in U]
        Sn = [Ss[u] * wh + sas[u] * bh + vrows[u:u + 1, :] * kh for u in U]
        for u in U:
            s_ref[base + u] = Sn[u]
        yrows = [jnp.sum(Sn[u] * rh, axis=0, keepdims=True) for u in U]
        y = jnp.zeros_like(vrows)
        for u in U:
            y = jnp.where(row == u, yrows[u], y)
        y_ref[pl.ds(base, G), :] = y
        return carry

    lax.fori_loop(0, HD_A // G, body, 0)


def _rwkv_sample_post_kernel(y_ref, r_ref, km_ref, v_ref, g_ref, rk_ref, lnw_ref, lnb_ref, ya_ref):
    n = y_ref.shape[1]
    heads = lambda x: x.reshape(H_A, HD_A, n)
    y3 = heads(y_ref[...])
    yc = y3 - jnp.mean(y3, axis=1, keepdims=True)
    var = jnp.mean(yc * yc, axis=1, keepdims=True)
    yn = (yc * lax.rsqrt(var + RWKV_GN_EPS)).reshape(D_MODEL, n) * lnw_ref[...] + lnb_ref[...]
    bonus = jnp.sum(heads(r_ref[...] * km_ref[...] * rk_ref[...]), axis=1, keepdims=True) * heads(v_ref[...])
    out = (yn + bonus.reshape(D_MODEL, n)) * g_ref[...]
    ya_ref[...] = out.T


def _gdn_sample_prep_kernel(qkv_ref, c0_ref, ab_ref, cw_ref, alog_ref, dtb_ref,
                            q_ref, k_ref, v_ref, eg_ref, beta_ref, cnew_ref):
    x = qkv_ref[...]
    conv = x * cw_ref[CONV_W - 1:CONV_W, :]
    for j in range(CONV_W - 1):
        conv = conv + c0_ref[j] * cw_ref[j:j + 1, :]
    for j in range(CONV_W - 2):
        cnew_ref[j] = c0_ref[j + 1]
    cnew_ref[CONV_W - 2] = x
    qkvc = _silu(conv)
    qs = _head_l2norm(qkvc[:, 0:D_MODEL], 1e-6)
    ks = _head_l2norm(qkvc[:, D_MODEL:2 * D_MODEL], 1e-6)
    for h in range(H_B):
        q_ref[:, LANES * h:LANES * (h + 1)] = qs[h] * (HD_B ** -0.5)
        k_ref[:, LANES * h:LANES * (h + 1)] = ks[h]
    v_ref[...] = qkvc[:, 2 * D_MODEL:3 * D_MODEL]
    g_all, beta_all = _gdn_gates(ab_ref[...], alog_ref[...], dtb_ref[...])
    eg_ref[...] = jnp.exp(_mm_sel_rhs(g_all, _sel_matrix(0), 3))
    beta_ref[...] = _mm_sel_rhs(beta_all, _sel_matrix(H_B), 3)


def _gdn_sample_state_kernel(qt_ref, kt_ref, v_ref, eg_ref, beta_ref, s0_ref, *rest, bb):
    s_ref, o_ref = rest[-2:]

    def body(i, carry):
        qt, kt = qt_ref[i], kt_ref[i]
        vb, egb, btb = v_ref[i], eg_ref[i], beta_ref[i]
        H = range(H_B)
        Sd = [s0_ref[i, h] * egb[h:h + 1, :] for h in H]
        ksr = [jnp.sum(Sd[h] * kt[:, h:h + 1], axis=0, keepdims=True) for h in H]
        v_new = [btb[h:h + 1, :] * (vb[h:h + 1, :] - ksr[h]) for h in H]
        Sn = [Sd[h] + kt[:, h:h + 1] * v_new[h] for h in H]
        for h in H:
            s_ref[i, h] = Sn[h]
        for h in H:
            o_ref[i, h:h + 1, :] = jnp.sum(Sn[h] * qt[:, h:h + 1], axis=0, keepdims=True)
        return carry

    lax.fori_loop(0, bb, body, 0)


def _gdn_sample_post_kernel(o_ref, z_ref, nw_ref, yb_ref):
    o = o_ref[...]
    z = z_ref[...]
    for h in range(H_B):
        sl = slice(LANES * h, LANES * (h + 1))
        oh = o[:, sl]
        on = oh * lax.rsqrt(jnp.mean(oh * oh, axis=-1, keepdims=True) + RMS_EPS) * nw_ref[...]
        yb_ref[:, sl] = on * _silu(z[:, sl])


def _params(sem):
    return pltpu.CompilerParams(dimension_semantics=sem, vmem_limit_bytes=VMEM_LIMIT)


def _full(shape):
    return pl.BlockSpec(shape, lambda *_: (0,) * len(shape))


_RESIDENT = pl.BlockSpec(memory_space=pltpu.VMEM)


def _row_tile(t, cap):
    tm = min(t, cap)
    assert t % tm == 0
    return tm


def _mod_spec(mod):
    rows = mod.shape[2]
    if rows == 1:
        return pl.BlockSpec((3 * N_SUB, 1, 1, D_MODEL), lambda b, i: (0, b, 0, 0))
    return pl.BlockSpec((3 * N_SUB, 1, rows, D_MODEL), lambda b, i: (0, 0, i, 0))


def _ada_call(c_all, w_ada, b_ada):
    depth = w_ada.shape[0]
    n = c_all.shape[0]
    return pl.pallas_call(
        _ada_kernel,
        grid=(depth, 3 * N_SUB),
        in_specs=[pl.BlockSpec((n, D_MODEL), lambda l, j: (0, 0)),
                  pl.BlockSpec((None, D_MODEL, D_MODEL), lambda l, j: (l, 0, j)),
                  pl.BlockSpec((None, 1, D_MODEL), lambda l, j: (l, 0, j))],
        out_specs=pl.BlockSpec((None, None, n, D_MODEL), lambda l, j: (l, j, 0, 0)),
        out_shape=jax.ShapeDtypeStruct((depth, 3 * N_SUB, n, D_MODEL), F32),
        compiler_params=_params(("arbitrary", "arbitrary")),
        name="ada_mod",
    )(c_all, w_ada, b_ada.reshape(depth, 1, 3 * N_SUB * D_MODEL))


def _ffn_call(x, mod, npre, npost, up, down, sub):
    b, t, _ = x.shape
    tm = _row_tile(t, 512)
    xspec = pl.BlockSpec((1, tm, D_MODEL), lambda b, i: (b, i, 0))
    return pl.pallas_call(
        functools.partial(_ffn_kernel, sub=sub),
        grid=(b, t // tm),
        in_specs=[xspec, _mod_spec(mod), _full((1, D_MODEL)), _full((1, D_MODEL)), _RESIDENT, _RESIDENT],
        out_specs=xspec,
        out_shape=jax.ShapeDtypeStruct(x.shape, F32),
        compiler_params=_params(("arbitrary", "arbitrary")),
        name="ffn",
    )(x, mod, npre, npost, up, down)


def _inproj_call(x, mod, npre, w):
    b, t, _ = x.shape
    tm = _row_tile(t, 256)
    widths = (A_COLS, 3 * D_MODEL, D_MODEL, 2 * D_MODEL, AB_COLS)
    spec = lambda n: pl.BlockSpec((1, tm, n), lambda b, i: (b, i, 0))
    return pl.pallas_call(
        _inproj_kernel,
        grid=(b, t // tm),
        in_specs=[spec(D_MODEL), _mod_spec(mod), _full((1, D_MODEL)), _RESIDENT],
        out_specs=[spec(n) for n in widths],
        out_shape=[jax.ShapeDtypeStruct((b, t, n), F32) for n in widths],
        compiler_params=_params(("arbitrary", "arbitrary")),
        name="in_proj",
    )(x, mod, npre, w)


def _outproj_call(x, ya, yb, pg, mod, npost, w):
    b, t, _ = x.shape
    tm = _row_tile(t, 512)
    spec = lambda n: pl.BlockSpec((1, tm, n), lambda b, i: (b, i, 0))
    return pl.pallas_call(
        _outproj_kernel,
        grid=(b, t // tm),
        in_specs=[spec(D_MODEL), spec(D_MODEL), spec(D_MODEL), spec(2 * D_MODEL), _mod_spec(mod),
                  _full((1, D_MODEL)), _RESIDENT],
        out_specs=spec(D_MODEL),
        out_shape=jax.ShapeDtypeStruct(x.shape, F32),
        compiler_params=_params(("arbitrary", "arbitrary")),
        name="out_proj",
    )(x, ya, yb, pg, mod, npost, w)


def _rwkv_prompt_call(pa, rp):
    b, t, _ = pa.shape
    C = RW_C
    R = RW_ROWS
    assert t % C == 0 and b % R == 0
    vec = _full((1, D_MODEL))
    lora = _full((2, LANES, D_MODEL))
    tile3 = _full((N_PAIR * R, 1, LANES))
    per_tile = lambda x: jnp.repeat(x, R, axis=0)
    return pl.pallas_call(
        _rwkv_prompt_kernel,
        grid=(b // R, t // C),
        in_specs=[pl.BlockSpec((R, C, A_COLS), lambda b, i: (b, i, 0)), _full((1, A_COLS)),
                  vec, lora, vec, lora, lora, vec, vec, tile3, tile3, tile3],
        out_specs=[pl.BlockSpec((R, C, D_MODEL), lambda b, i: (b, i, 0)),
                   pl.BlockSpec((R, H_A, HD_A, HD_A), lambda b, i: (b, 0, 0, 0)),
                   pl.BlockSpec((R, 1, A_COLS), lambda b, i: (b, 0, 0))],
        out_shape=[jax.ShapeDtypeStruct((b, t, D_MODEL), F32),
                   jax.ShapeDtypeStruct((b, H_A, HD_A, HD_A), F32),
                   jax.ShapeDtypeStruct((b, 1, A_COLS), F32)],
        scratch_shapes=[pltpu.VMEM((R, A_COLS), F32), pltpu.VMEM((N_PAIR * R, LANES, LANES), F32)],
        compiler_params=_params(("arbitrary", "arbitrary")),
        name="rwkv_prompt",
    )(pa, rp["mu"], rp["w0"], rp["w2p"], rp["a0"], rp["a2p"], rp["g2"], rp["k_k"], rp["k_a"],
      per_tile(rp["rk3"]), per_tile(rp["lnw3"]), per_tile(rp["lnb3"]))


def _gdn_prompt_call(qkv, ab, z, gp):
    b, t, _ = qkv.shape
    C = GD_C
    assert t % C == 0
    spec = lambda n: pl.BlockSpec((1, C, n), lambda b, i: (b, i, 0))
    return pl.pallas_call(
        _gdn_prompt_kernel,
        grid=(b, t // C),
        in_specs=[spec(3 * D_MODEL), spec(AB_COLS), spec(D_MODEL), _full((CONV_W, 3 * D_MODEL)),
                  _full((1, LANES)), _full((1, LANES)), _full((1, HD_B))],
        out_specs=[spec(D_MODEL),
                   pl.BlockSpec((1, CONV_W - 1, 3 * D_MODEL), lambda b, i: (b, 0, 0)),
                   pl.BlockSpec((1, H_B, HD_B, HD_B), lambda b, i: (b, 0, 0, 0))],
        out_shape=[jax.ShapeDtypeStruct((b, t, D_MODEL), F32),
                   jax.ShapeDtypeStruct((b, CONV_W - 1, 3 * D_MODEL), F32),
                   jax.ShapeDtypeStruct((b, H_B, HD_B, HD_B), F32)],
        scratch_shapes=[pltpu.VMEM((8, 3 * D_MODEL), F32)],
        compiler_params=_params(("arbitrary", "arbitrary")),
        name="gdn_prompt",
    )(qkv, ab, z, gp["conv_w"], gp["alog"], gp["dtb"], gp["norm_w"])


def _rwkv_sample(pa, shift0, wkv_all, wkv_new, l, rp):
    n = pa.shape[0]
    vec = _full((1, D_MODEL))
    lora = _full((2, LANES, D_MODEL))
    chan = _full((D_MODEL, n))
    r, w, km, v, kkn, bb_, g = pl.pallas_call(
        _rwkv_sample_prep_kernel,
        grid=(1,),
        in_specs=[_full((n, A_COLS)), _full((n, A_COLS)), _full((1, A_COLS)), vec, lora, vec, lora, lora,
                  vec, vec],
        out_specs=[chan] * 7,
        out_shape=[jax.ShapeDtypeStruct((D_MODEL, n), F32)] * 7,
        compiler_params=_params(("arbitrary",)),
        name="rwkv_sample_prep",
    )(pa, shift0, rp["mu"], rp["w0"], rp["w2p"], rp["a0"], rp["a2p"], rp["g2"], rp["k_k"], rp["k_a"])

    hspec = pl.BlockSpec((HD_A, n), lambda h: (h, 0))
    sspec = pl.BlockSpec((None, None, HD_A, HD_A, n), lambda h: (l, h, 0, 0, 0))
    carried = [] if wkv_new is None else [wkv_new]
    s_new, y = pl.pallas_call(
        _rwkv_sample_state_kernel,
        grid=(H_A,),
        in_specs=[hspec] * 6 + [sspec] + [pl.BlockSpec(memory_space=pl.ANY)] * len(carried),
        out_specs=[sspec, hspec],
        out_shape=[jax.ShapeDtypeStruct(wkv_all.shape, F32), jax.ShapeDtypeStruct((D_MODEL, n), F32)],
        input_output_aliases={7: 0} if carried else {},
        compiler_params=_params(("arbitrary",)),
        name="rwkv_sample_state",
    )(r, w, km, kkn, bb_, v, wkv_all, *carried)

    col = _full((D_MODEL, 1))
    ya = pl.pallas_call(
        _rwkv_sample_post_kernel,
        grid=(1,),
        in_specs=[chan] * 5 + [col] * 3,
        out_specs=_full((n, D_MODEL)),
        out_shape=jax.ShapeDtypeStruct((n, D_MODEL), F32),
        compiler_params=_params(("arbitrary",)),
        name="rwkv_sample_post",
    )(y, r, km, v, g, rp["rk_col"], rp["lnw_col"], rp["lnb_col"])
    return ya, s_new


def _gdn_sample(qkv, conv0, ssm_all, ssm_new, l, ab, z, gp):
    n = qkv.shape[0]
    flat = _full((n, D_MODEL))
    cspec = _full((CONV_W - 1, n, 3 * D_MODEL))
    q, k, v, eg, beta, cnew = pl.pallas_call(
        _gdn_sample_prep_kernel,
        grid=(1,),
        in_specs=[_full((n, 3 * D_MODEL)), cspec, _full((n, AB_COLS)), _full((CONV_W, 3 * D_MODEL)),
                  _full((1, LANES)), _full((1, LANES))],
        out_specs=[flat] * 5 + [cspec],
        out_shape=[jax.ShapeDtypeStruct((n, D_MODEL), F32)] * 5
        + [jax.ShapeDtypeStruct((CONV_W - 1, n, 3 * D_MODEL), F32)],
        compiler_params=_params(("arbitrary",)),
        name="gdn_sample_prep",
    )(qkv, jnp.swapaxes(conv0, 0, 1), ab, gp["conv_w"], gp["alog"], gp["dtb"])

    bb = 4
    assert n % bb == 0
    heads = lambda x: x.reshape(n, H_B, HD_B)
    hspec = pl.BlockSpec((bb, H_B, HD_B), lambda i: (i, 0, 0))
    tspec = pl.BlockSpec((bb, HD_B, H_B), lambda i: (i, 0, 0))
    sspec = pl.BlockSpec((None, bb, H_B, HD_B, HD_B), lambda i: (l, i, 0, 0, 0))
    carried = [] if ssm_new is None else [ssm_new]
    s_new, o = pl.pallas_call(
        functools.partial(_gdn_sample_state_kernel, bb=bb),
        grid=(n // bb,),
        in_specs=[tspec, tspec, hspec, hspec, hspec, sspec] + [pl.BlockSpec(memory_space=pl.ANY)] * len(carried),
        out_specs=[sspec, hspec],
        out_shape=[jax.ShapeDtypeStruct(ssm_all.shape, F32), jax.ShapeDtypeStruct((n, H_B, HD_B), F32)],
        input_output_aliases={6: 0} if carried else {},
        compiler_params=_params(("arbitrary",)),
        name="gdn_sample_state",
    )(jnp.swapaxes(heads(q), 1, 2), jnp.swapaxes(heads(k), 1, 2), heads(v), heads(eg), heads(beta), ssm_all,
      *carried)

    yb = pl.pallas_call(
        _gdn_sample_post_kernel,
        grid=(1,),
        in_specs=[flat, flat, _full((1, HD_B))],
        out_specs=flat,
        out_shape=jax.ShapeDtypeStruct((n, D_MODEL), F32),
        compiler_params=_params(("arbitrary",)),
        name="gdn_sample_post",
    )(o.reshape(n, D_MODEL), z, gp["norm_w"])
    return yb, jnp.swapaxes(cnew, 0, 1), s_new


def _hi_lo(w):
    hi = w.astype(BF16)
    return jnp.stack([hi, (w - hi.astype(F32)).astype(BF16)])


def _pad_rows(x, rows, at):
    out = jnp.zeros((rows, x.shape[1]), x.dtype)
    return lax.dynamic_update_slice(out, x, (at, 0))


def kernel(x_prompt, x_sample, c_prompt, c_sample, state_rwkv_shift, state_rwkv_wkv, state_gdn_conv, state_gdn_ssm, w_ada, b_ada, norm_pre, norm_post, ffn_up, ffn_down, w_in, w_out, rwkv_mu, rwkv_w0, rwkv_w2, rwkv_a0, rwkv_a2, rwkv_g2, rwkv_k_k, rwkv_k_a, rwkv_r_k, rwkv_ln_w, rwkv_ln_b, gdn_conv, gdn_a_log, gdn_dt_bias, gdn_norm_w):
    depth = w_ada.shape[0]
    bp = x_prompt.shape[0]
    ns = x_sample.shape[0]

    n_c = bp + ns
    n_cp = -(-n_c // 16) * 16
    c_all = jnp.concatenate([c_prompt, c_sample, jnp.zeros((n_cp - n_c, D_MODEL), F32)], axis=0)
    mod = _ada_call(c_all, w_ada, b_ada)

    o_b = A_COLS
    o_ab = o_b + 3 * D_MODEL
    o_z = o_ab + 2 * H_B
    o_g = o_z + D_MODEL
    def w_in_layer(l):
        w = w_in[l]
        return jnp.concatenate(
            [w[:, 0:o_b], w[:, o_b:o_ab], w[:, o_z:o_g], w[:, o_g:], w[:, o_ab:o_z],
             jnp.zeros((D_MODEL, AB_COLS - 2 * H_B), F32)], axis=-1).astype(BF16)

    wkv_t = jnp.transpose(state_rwkv_wkv, (0, 2, 3, 4, 1))

    yp = x_prompt
    ys = x_sample.reshape(1, ns, D_MODEL)
    outs = [[] for _ in range(6)]
    s_wkv = s_ssm = None
    for l in range(depth):
        w_in_r = w_in_layer(l)
        up_b = [ffn_up[l, j].astype(BF16) for j in range(2)]
        down_b = [ffn_down[l, j].astype(BF16) for j in range(2)]
        w_out_b = w_out[l].astype(BF16)
        mod_p = mod[l, :, 0:bp].reshape(3 * N_SUB, bp, 1, D_MODEL)
        mod_s = mod[l, :, bp:bp + ns].reshape(3 * N_SUB, 1, ns, D_MODEL)
        npre = [norm_pre[l, i][None] for i in range(N_SUB)]
        npost = [norm_post[l, i][None] for i in range(N_SUB)]
        row = lambda x: x[l][None]
        tile3 = lambda x: x[l].reshape(N_PAIR, 1, LANES)
        rp = dict(mu=row(rwkv_mu), w0=row(rwkv_w0), a0=row(rwkv_a0), k_k=row(rwkv_k_k), k_a=row(rwkv_k_a),
                  w2p=_hi_lo(_pad_rows(rwkv_w2[l], LANES, 0)), a2p=_hi_lo(_pad_rows(rwkv_a2[l], LANES, W_LORA)),
                  g2=_hi_lo(rwkv_g2[l]), rk3=tile3(rwkv_r_k), lnw3=tile3(rwkv_ln_w), lnb3=tile3(rwkv_ln_b),
                  rk_col=rwkv_r_k[l].reshape(D_MODEL, 1), lnw_col=rwkv_ln_w[l].reshape(D_MODEL, 1),
                  lnb_col=rwkv_ln_b[l].reshape(D_MODEL, 1))
        lane_row = lambda x: jnp.zeros((1, LANES), F32).at[0, 0:H_B].set(x[l])
        gp = dict(conv_w=gdn_conv[l], norm_w=row(gdn_norm_w), alog=lane_row(gdn_a_log),
                  dtb=lane_row(gdn_dt_bias))

        yp = _ffn_call(yp, mod_p, npre[0], npost[0], up_b[0], down_b[0], 0)
        pa, qkv, z, pg, ab = _inproj_call(yp, mod_p, npre[1], w_in_r)
        ya, p_wkv, p_shift = _rwkv_prompt_call(pa, rp)
        yb, p_conv, p_ssm = _gdn_prompt_call(qkv, ab, z, gp)
        yp = _outproj_call(yp, ya, yb, pg, mod_p, npost[1], w_out_b)
        yp = _ffn_call(yp, mod_p, npre[2], npost[2], up_b[1], down_b[1], 2)

        ys = _ffn_call(ys, mod_s, npre[0], npost[0], up_b[0], down_b[0], 0)
        pa, qkv, z, pg, ab = _inproj_call(ys, mod_s, npre[1], w_in_r)
        ya, s_wkv = _rwkv_sample(pa[0], state_rwkv_shift[l], wkv_t, s_wkv, l, rp)
        yb, s_conv, s_ssm = _gdn_sample(qkv[0], state_gdn_conv[l], state_gdn_ssm, s_ssm, l, ab[0], z[0], gp)
        ys = _outproj_call(ys, ya[None], yb[None], pg, mod_s, npost[1], w_out_b)
        ys = _ffn_call(ys, mod_s, npre[2], npost[2], up_b[1], down_b[1], 2)

        for lst, val in zip(outs, (p_shift[:, 0], p_wkv, p_conv, p_ssm, pa[0], s_conv)):
            lst.append(val)

    st = [jnp.stack(o) for o in outs]
    return (yp, ys.reshape(ns, 1, D_MODEL), st[0], st[1], st[2], st[3], st[4],
            jnp.transpose(s_wkv, (0, 4, 1, 2, 3)), st[5], s_ssm)
```

```python
import functools

import jax
import jax.numpy as jnp
from jax import lax
from jax.experimental import pallas as pl
from jax.experimental.pallas import tpu as pltpu

F32 = jnp.float32
BF16 = jnp.bfloat16

D_MODEL = 1024
HD_A = 64
H_A = D_MODEL // HD_A
W_LORA = 64
A_LORA = 64
G_LORA = 128
A_COLS = 3 * D_MODEL + W_LORA + A_LORA + G_LORA
RWKV_GN_EPS = 6.4e-4
HD_B = 128
H_B = D_MODEL // HD_B
CONV_W = 4
D_FF = 2816
N_SUB = 3
FFN_RES = 0.5
RMS_EPS = 1e-6

LANES = 128
N_PAIR = D_MODEL // LANES
RW_C = 64
RW_ROWS = 2
GD_C = 128
AB_COLS = LANES
IN_COLS_R = A_COLS + 3 * D_MODEL + D_MODEL + 2 * D_MODEL + AB_COLS
VMEM_LIMIT = 56 * 1024 * 1024
REC_PASSES = 1

_NN = (((1,), (0,)), ((), ()))
_NT = (((1,), (1,)), ((), ()))


def _dg(a, b, dn=_NN):
    return lax.dot_general(a, b, dn, preferred_element_type=F32)


def _split2(x):
    hi = x.astype(BF16)
    lo = (x - hi.astype(F32)).astype(BF16)
    return hi, lo


def _split3(x):
    h1 = x.astype(BF16)
    r1 = x - h1.astype(F32)
    h2 = r1.astype(BF16)
    h3 = (r1 - h2.astype(F32)).astype(BF16)
    return h1, h2, h3


def _mm(a, b, dn=_NN, passes=REC_PASSES):
    if passes == 1:
        return _dg(a.astype(BF16), b.astype(BF16), dn)
    ah, al = _split2(a)
    bh, bl = _split2(b)
    return _dg(ah, bh, dn) + (_dg(ah, bl, dn) + _dg(al, bh, dn))


def _mm_sel_rhs(x, e, pieces):
    parts = _split3(x) if pieces == 3 else _split2(x)
    out = _dg(parts[0], e)
    for p in parts[1:]:
        out = out + _dg(p, e)
    return out


def _mm_sel_lhs(e, x, pieces):
    parts = _split3(x) if pieces == 3 else _split2(x)
    out = _dg(e, parts[0])
    for p in parts[1:]:
        out = out + _dg(e, p)
    return out


def _iota2(shape, dim):
    return lax.broadcasted_iota(jnp.int32, shape, dim)


def _softplus(x):
    return jnp.maximum(x, 0.0) + jnp.log(1.0 + jnp.exp(-jnp.abs(x)))


def _silu(x):
    return x * jax.nn.sigmoid(x)


def _rms(x, w):
    return x * lax.rsqrt(jnp.mean(x * x, axis=-1, keepdims=True) + RMS_EPS) * w


def _mod_in(x, npre, mod_ref, sub):
    shift = mod_ref[3 * sub, 0]
    scale = mod_ref[3 * sub + 1, 0]
    return _rms(x, npre) * (1.0 + scale) + shift


def _tri_masks(n, top):
    i = _iota2((n, n), 0)
    j = _iota2((n, n), 1)
    masks = []
    s = 1
    while s <= top:
        sh = s.bit_length() - 1
        same = (i >> (sh + 1)) == (j >> (sh + 1))
        masks.append(same & (((i >> sh) & 1) == 1) & (((j >> sh) & 1) == 0))
        s *= 2
    return masks


def _tri_inv(Ls, masks):
    n = Ls[0].shape[0]
    eye = jnp.where(_iota2((n, n), 0) == _iota2((n, n), 1), 1.0, 0.0).astype(F32)
    Xs = [eye - jnp.where(masks[0], L, 0.0) for L in Ls]
    Lb = [L.astype(BF16) for L in Ls]
    zero = jnp.zeros((), BF16)
    for m in masks[1:]:
        Xb = [X.astype(BF16) for X in Xs]
        Ts = [_dg(xb, jnp.where(m, lb, zero)) for xb, lb in zip(Xb, Lb)]
        Xs = [X - _dg(T.astype(BF16), xb) for X, T, xb in zip(Xs, Ts, Xb)]
    return Xs


def _to_tiles(x):
    return jnp.stack([x[:, LANES * p:LANES * (p + 1)] for p in range(N_PAIR)], axis=0)


def _pair_headsum(x3, ebd):
    p, n, l = x3.shape
    return _mm_sel_rhs(x3.reshape(p * n, l), ebd, 2).reshape(p, n, l)


def _ebd():
    i = _iota2((LANES, LANES), 0)
    j = _iota2((LANES, LANES), 1)
    return jnp.where((i >> 6) == (j >> 6), 1.0, 0.0).astype(BF16)


def _mm_lora(a, w_ref):
    ah, al = _split2(a)
    return _dg(ah, w_ref[0]) + (_dg(ah, w_ref[1]) + _dg(al, w_ref[0]))


def _rwkv_prep(pa, prev, mu, w0, w2p, a0, a2p, g2, k_k, k_a):
    xs = pa + mu * (prev - pa)
    r = xs[:, 0:D_MODEL]
    k = xs[:, D_MODEL:2 * D_MODEL]
    v = xs[:, 2 * D_MODEL:3 * D_MODEL]
    wa = xs[:, 3 * D_MODEL:3 * D_MODEL + LANES]
    gd = xs[:, 3 * D_MODEL + LANES:A_COLS]
    w_lin = _mm_lora(jnp.tanh(wa), w2p)
    a_lin = _mm_lora(wa, a2p)
    g = _mm_lora(jax.nn.sigmoid(gd), g2)
    w_log = -_softplus(-(w0 + w_lin)) - 0.5
    lw = -jnp.exp(w_log)
    a = jax.nn.sigmoid(a0 + a_lin)
    kkr = k * k_k
    km = k * (1.0 + (a - 1.0) * k_a)
    return r, km, v, kkr, a, lw, g


def _rwkv_epilogue(y3, r3, km3, v3, rk3, lnw3, lnb3, ebd):
    mean = _pair_headsum(y3, ebd) * (1.0 / HD_A)
    yc = y3 - mean
    var = _pair_headsum(yc * yc, ebd) * (1.0 / HD_A)
    yn = yc * lax.rsqrt(var + RWKV_GN_EPS) * lnw3 + lnb3
    bonus = _pair_headsum(r3 * km3 * rk3, ebd) * v3
    return yn + bonus


def _gdn_gates(ab, alog, dtb):
    g_all = -jnp.exp(alog) * _softplus(ab + dtb)
    beta_all = jax.nn.sigmoid(ab)
    return g_all, beta_all


def _sel_matrix(offset):
    i = _iota2((LANES, D_MODEL), 0)
    j = _iota2((LANES, D_MODEL), 1)
    return jnp.where(i == (j >> 7) + offset, 1.0, 0.0).astype(BF16)


def _head_l2norm(x, eps):
    outs = []
    for h in range(H_B):
        xh = x[:, LANES * h:LANES * (h + 1)]
        outs.append(xh * lax.rsqrt(jnp.sum(xh * xh, axis=-1, keepdims=True) + eps))
    return outs


def _ada_kernel(c_ref, w_ref, b_ref, o_ref):
    s = _silu(c_ref[...]).astype(BF16)
    o_ref[...] = _dg(s, w_ref[...].astype(BF16)) + b_ref[...]


def _ffn_kernel(x_ref, mod_ref, npre_ref, npost_ref, up_ref, down_ref, o_ref, *, sub):
    x = x_ref[0]
    h = _mod_in(x, npre_ref[...], mod_ref, sub).astype(BF16)
    gate = _dg(h, up_ref[:, 0:D_FF])
    val = _dg(h, up_ref[:, D_FF:2 * D_FF])
    act = (_silu(gate) * val).astype(BF16)
    y = _dg(act, down_ref[...])
    o_ref[0] = x + FFN_RES * mod_ref[3 * sub + 2, 0] * _rms(y, npost_ref[...])


def _inproj_kernel(x_ref, mod_ref, npre_ref, w_ref, pa_ref, qkv_ref, z_ref, pg_ref, ab_ref):
    h = _mod_in(x_ref[0], npre_ref[...], mod_ref, 1).astype(BF16)
    o = 0
    for ref in (pa_ref, qkv_ref, z_ref, pg_ref, ab_ref):
        n = ref.shape[-1]
        ref[0] = _dg(h, w_ref[:, o:o + n]).astype(ref.dtype)
        o += n


def _causal_conv_silu(x, hist_ref, cw_ref):
    n = x.shape[0]
    hist_ref[8:8 + n, :] = x
    conv = x * cw_ref[CONV_W - 1:CONV_W, :]
    for s in range(1, CONV_W):
        conv = conv + hist_ref[8 - s:8 - s + n, :] * cw_ref[CONV_W - 1 - s:CONV_W - s, :]
    hist_ref[0:8, :] = x[n - 8:n]
    return _silu(conv)


def _outproj_kernel(x_ref, ya_ref, yb_ref, pg_ref, mod_ref, npost_ref, w_ref, o_ref):
    pg = pg_ref[0].astype(F32)
    ya = ya_ref[0].astype(F32)
    yb = yb_ref[0].astype(F32)
    m = jax.nn.sigmoid(pg[:, 0:D_MODEL]) * ya + jax.nn.sigmoid(pg[:, D_MODEL:]) * yb
    y = _dg(m.astype(BF16), w_ref[...])
    o_ref[0] = x_ref[0] + mod_ref[3 * 1 + 2, 0] * _rms(y, npost_ref[...])


def _rwkv_prompt_kernel(pa_ref, mu_ref, w0_ref, w2_ref, a0_ref, a2_ref, g2_ref, kk_ref, ka_ref,
                        rk3_ref, lnw3_ref, lnb3_ref, ya_ref, s_ref, shift_ref, carry_ref, sbd_ref):
    C = RW_C
    R = RW_ROWS
    N = R * C
    t = pl.program_id(1)

    @pl.when(t == 0)
    def _():
        carry_ref[...] = jnp.zeros_like(carry_ref)
        sbd_ref[...] = jnp.zeros_like(sbd_ref)

    pa = pa_ref[...].reshape(N, A_COLS)
    last = jnp.concatenate([jnp.broadcast_to(carry_ref[s:s + 1, :], (C, A_COLS)) for s in range(R)], axis=0)
    prev = jnp.where((_iota2((N, 1), 0) & (C - 1)) == 0, last, pltpu.roll(pa, 1, 0))
    for s in range(R):
        carry_ref[s:s + 1, :] = pa[s * C + C - 1:s * C + C]
    r, km, v, kkr, a, lw, g = _rwkv_prep(pa, prev, mu_ref[...], w0_ref[...], w2_ref, a0_ref[...],
                                         a2_ref, g2_ref, kk_ref[...], ka_ref[...])

    ti = _iota2((N, N), 0)
    tj = _iota2((N, N), 1)
    csh = C.bit_length() - 1
    tri = jnp.where((ti >= tj) & ((ti >> csh) == (tj >> csh)), 1.0, 0.0).astype(BF16)
    cum = _mm_sel_lhs(tri, lw, 3)
    ebd = _ebd()
    NQ = N_PAIR * R
    tiles = lambda x: _to_tiles(x).reshape(NQ, C, LANES)
    r3, km3, v3, kkr3, a3, lw3, cum3 = (tiles(x) for x in (r, km, v, kkr, a, lw, cum))
    kk3 = kkr3 * lax.rsqrt(_pair_headsum(kkr3 * kkr3, ebd) + 1e-12)
    b3 = kk3 * a3
    cl = cum3[:, C - 1:C, :]
    e_neg = jnp.exp(-cum3)
    e_end = jnp.exp(cl - cum3)
    rh3 = r3 * jnp.exp(cum3)
    ah3 = kk3 * jnp.exp(cum3 - lw3)
    bt3 = b3 * e_neg
    kt3 = km3 * e_neg
    be3 = b3 * e_end
    ke3 = km3 * e_end
    e_last3 = jnp.exp(cl)

    lane0 = _iota2((1, LANES), 1) < HD_A

    def sm(x):
        return jnp.concatenate([jnp.where(lane0, x, 0.0), jnp.where(lane0, 0.0, x)], axis=0)

    ii = _iota2((2 * C, 2 * C), 0)
    jj = _iota2((2 * C, 2 * C), 1)
    strict = jj < ii
    incl = jj <= ii
    masks = _tri_masks(2 * C, C // 2)

    P = range(NQ)
    T2 = 2 * C
    bsm = lambda x: sm(x).astype(BF16)
    ARs = [jnp.concatenate([bsm(ah3[p]), bsm(rh3[p])], axis=0) for p in P]
    BKs = [jnp.concatenate([bsm(bt3[p]), bsm(kt3[p])], axis=0) for p in P]
    BKe = [jnp.concatenate([bsm(be3[p]), bsm(ke3[p])], axis=0) for p in P]
    Vs = [sm(v3[p]) for p in P]
    Vb = [V.astype(BF16) for V in Vs]
    Ss = [sbd_ref[p] for p in P]
    QQ = [_dg(ARs[p], BKs[p], _NT) for p in P]
    QS = [_dg(ARs[p], Ss[p].astype(BF16), _NT) for p in P]
    Lab = [jnp.where(strict, Q[0:T2, 0:T2], 0.0) for Q in QQ]
    Lak = [jnp.where(strict, Q[0:T2, T2:2 * T2], 0.0).astype(BF16) for Q in QQ]
    Ms = [QS[p][0:T2] + _dg(Lak[p], Vb[p]) for p in P]
    Tinv = _tri_inv(Lab, masks)
    Us = [-_dg(Tinv[p].astype(BF16), Ms[p].astype(BF16)) for p in P]
    UVb = [jnp.concatenate([Us[p].astype(BF16), Vb[p]], axis=0) for p in P]
    incl2 = jnp.concatenate([incl, incl], axis=1)
    Arbk = [jnp.where(incl2, Q[T2:2 * T2, :], 0.0).astype(BF16) for Q in QQ]
    Ys = [QS[p][T2:2 * T2] + _dg(Arbk[p], UVb[p]) for p in P]
    for p in P:
        UVt = jnp.concatenate([Us[p].T, Vs[p].T], axis=1).astype(BF16)
        sbd_ref[p] = Ss[p] * e_last3[p] + _dg(UVt, BKe[p])
    y3 = jnp.stack([Y[0:C] + Y[C:2 * C] for Y in Ys], axis=0)

    out3 = _rwkv_epilogue(y3, r3, km3, v3, rk3_ref[...], lnw3_ref[...], lnb3_ref[...], ebd)
    for q in P:
        p, s = divmod(q, R)
        ya_ref[s, :, LANES * p:LANES * (p + 1)] = (
            out3[q] * g[s * C:(s + 1) * C, LANES * p:LANES * (p + 1)]).astype(ya_ref.dtype)

    @pl.when(t == pl.num_programs(1) - 1)
    def _():
        for s in range(R):
            shift_ref[s] = pa[s * C + C - 1:s * C + C]
        for q in P:
            p, s = divmod(q, R)
            S = sbd_ref[q]
            s_ref[s, 2 * p] = S[0:HD_A, 0:HD_A]
            s_ref[s, 2 * p + 1] = pltpu.roll(S, HD_A, 1)[HD_A:LANES, 0:HD_A]


def _gdn_prompt_kernel(qkv_ref, ab_ref, z_ref, cw_ref, alog_ref, dtb_ref, nw_ref,
                       yb_ref, conv_ref, ssm_ref, hist_ref):
    C = GD_C
    t = pl.program_id(1)

    @pl.when(t == 0)
    def _():
        hist_ref[0:8, :] = jnp.zeros((8, 3 * D_MODEL), F32)
        ssm_ref[...] = jnp.zeros_like(ssm_ref)

    x = qkv_ref[0]
    qkvc = _causal_conv_silu(x, hist_ref, cw_ref)
    qs = _head_l2norm(qkvc[:, 0:D_MODEL], 1e-6)
    ks = _head_l2norm(qkvc[:, D_MODEL:2 * D_MODEL], 1e-6)
    vv = qkvc[:, 2 * D_MODEL:3 * D_MODEL]

    g_all, beta_all = _gdn_gates(ab_ref[0], alog_ref[...], dtb_ref[...])
    tri = jnp.where(_iota2((C, C), 0) >= _iota2((C, C), 1), 1.0, 0.0).astype(BF16)
    gc_all = _mm_sel_lhs(tri, g_all, 3)

    ii = _iota2((C, C), 0)
    jj = _iota2((C, C), 1)
    strict = jj < ii
    incl = jj <= ii
    masks = _tri_masks(C, C // 2)
    z = z_ref[0].astype(F32)

    H = range(H_B)
    sls = [slice(LANES * h, LANES * (h + 1)) for h in H]
    qh = [qs[h] * (HD_B ** -0.5) for h in H]
    bcast = lambda col: jnp.broadcast_to(col, (C, LANES))
    gh = [bcast(gc_all[:, h:h + 1]) for h in H]
    bh = [bcast(beta_all[:, H_B + h:H_B + h + 1]) for h in H]
    gct = gc_all.T
    dec = [jnp.exp(gh[h] - gct[h:h + 1, :]) for h in H]
    eg = [jnp.exp(g) for g in gh]
    glast = [g[C - 1:C, :] for g in gh]
    kb = [ks[h] * bh[h] for h in H]
    KQ = [_dg(jnp.concatenate([kb[h], qh[h]], axis=0).astype(BF16), ks[h].astype(BF16), _NT) for h in H]
    Ls = [jnp.where(strict, KQ[h][0:C] * dec[h], 0.0) for h in H]
    attn = [jnp.where(incl, KQ[h][C:2 * C] * dec[h], 0.0) for h in H]
    Tinv = _tri_inv(Ls, masks)
    rhs = [jnp.concatenate([vv[:, sls[h]] * bh[h], kb[h] * eg[h]], axis=1).astype(BF16) for h in H]
    UW = [_dg(Tinv[h].astype(BF16), rhs[h]) for h in H]
    Ss = [ssm_ref[0, h] for h in H]
    Sb = [S.astype(BF16) for S in Ss]
    v_new = [UW[h][:, 0:LANES] - _dg(UW[h][:, LANES:2 * LANES].astype(BF16), Sb[h]) for h in H]
    vnb = [v.astype(BF16) for v in v_new]
    os_ = [_dg(jnp.concatenate([qh[h] * eg[h], attn[h]], axis=1).astype(BF16),
               jnp.concatenate([Sb[h], vnb[h]], axis=0)) for h in H]
    for h in H:
        k_dec = ks[h] * jnp.exp(glast[h] - gh[h])
        ssm_ref[0, h] = Ss[h] * jnp.exp(glast[h]) + _dg(k_dec.T.astype(BF16), vnb[h])
    for h in H:
        o = os_[h]
        on = o * lax.rsqrt(jnp.mean(o * o, axis=-1, keepdims=True) + RMS_EPS) * nw_ref[...]
        yb_ref[0, :, sls[h]] = (on * _silu(z[:, sls[h]])).astype(yb_ref.dtype)

    @pl.when(t == pl.num_programs(1) - 1)
    def _():
        conv_ref[0] = x[C - (CONV_W - 1):C]


def _rwkv_sample_prep_kernel(pa_ref, prev_ref, mu_ref, w0_ref, w2_ref, a0_ref, a2_ref, g2_ref, kk_ref,
                             ka_ref, r_ref, w_ref, km_ref, v_ref, kkn_ref, b_ref, g_ref):
    r, km, v, kkr, a, lw, g = _rwkv_prep(pa_ref[...], prev_ref[...], mu_ref[...], w0_ref[...], w2_ref,
                                         a0_ref[...], a2_ref, g2_ref, kk_ref[...], ka_ref[...])
    n = r.shape[0]
    heads = lambda x: x.reshape(H_A, HD_A, n)
    kk3 = heads(kkr.T)
    kkn = kk3 * lax.rsqrt(jnp.sum(kk3 * kk3, axis=1, keepdims=True) + 1e-12)
    r_ref[...] = r.T
    w_ref[...] = jnp.exp(lw).T
    km_ref[...] = km.T
    v_ref[...] = v.T
    kkn_ref[...] = kkn.reshape(D_MODEL, n)
    b_ref[...] = (kkn * heads(a.T)).reshape(D_MODEL, n)
    g_ref[...] = g.T


def _rwkv_sample_state_kernel(r_ref, w_ref, km_ref, kkn_ref, b_ref, v_ref, s0_ref, stacked_ref, s_ref, y_ref):
    del stacked_ref
    rh, wh, kh, kkh, bh = r_ref[...], w_ref[...], km_ref[...], kkn_ref[...], b_ref[...]
    G = 8
    row = _iota2((G, 1), 0)

    def body(j, carry):
        base = pl.multiple_of(j * G, G)
        vrows = v_ref[pl.ds(base, G), :]
        U = range(G)
        Ss = [s0_ref[base + u] for u in U]
        sas = [-jnp.sum(Ss[u] * kkh, axis=0, keepdims=True) for u in U]
        Sn = [Ss[u] * wh + sas[u] * bh + vrows[u:u + 1, :] * kh for u in U]
        for u in U:
            s_ref[base + u] = Sn[u]
        yrows = [jnp.sum(Sn[u] * rh, axis=0, keepdims=True) for u in U]
        y = jnp.zeros_like(vrows)
        for u in U:
            y = jnp.where(row == u, yrows[u], y)
        y_ref[pl.ds(base, G), :] = y
        return carry

    lax.fori_loop(0, HD_A // G, body, 0)


def _rwkv_sample_post_kernel(y_ref, r_ref, km_ref, v_ref, g_ref, rk_ref, lnw_ref, lnb_ref, ya_ref):
    n = y_ref.shape[1]
    heads = lambda x: x.reshape(H_A, HD_A, n)
    y3 = heads(y_ref[...])
    yc = y3 - jnp.mean(y3, axis=1, keepdims=True)
    var = jnp.mean(yc * yc, axis=1, keepdims=True)
    yn = (yc * lax.rsqrt(var + RWKV_GN_EPS)).reshape(D_MODEL, n) * lnw_ref[...] + lnb_ref[...]
    bonus = jnp.sum(heads(r_ref[...] * km_ref[...] * rk_ref[...]), axis=1, keepdims=True) * heads(v_ref[...])
    out = (yn + bonus.reshape(D_MODEL, n)) * g_ref[...]
    ya_ref[...] = out.T


def _gdn_sample_prep_kernel(qkv_ref, c0_ref, ab_ref, cw_ref, alog_ref, dtb_ref,
                            q_ref, k_ref, v_ref, eg_ref, beta_ref, cnew_ref):
    x = qkv_ref[...]
    conv = x * cw_ref[CONV_W - 1:CONV_W, :]
    for j in range(CONV_W - 1):
        conv = conv + c0_ref[j] * cw_ref[j:j + 1, :]
    for j in range(CONV_W - 2):
        cnew_ref[j] = c0_ref[j + 1]
    cnew_ref[CONV_W - 2] = x
    qkvc = _silu(conv)
    qs = _head_l2norm(qkvc[:, 0:D_MODEL], 1e-6)
    ks = _head_l2norm(qkvc[:, D_MODEL:2 * D_MODEL], 1e-6)
    for h in range(H_B):
        q_ref[:, LANES * h:LANES * (h + 1)] = qs[h] * (HD_B ** -0.5)
        k_ref[:, LANES * h:LANES * (h + 1)] = ks[h]
    v_ref[...] = qkvc[:, 2 * D_MODEL:3 * D_MODEL]
    g_all, beta_all = _gdn_gates(ab_ref[...], alog_ref[...], dtb_ref[...])
    eg_ref[...] = jnp.exp(_mm_sel_rhs(g_all, _sel_matrix(0), 3))
    beta_ref[...] = _mm_sel_rhs(beta_all, _sel_matrix(H_B), 3)


def _gdn_sample_state_kernel(qt_ref, kt_ref, v_ref, eg_ref, beta_ref, s0_ref, stacked_ref, s_ref, o_ref, *, bb):
    del stacked_ref

    def body(i, carry):
        qt, kt = qt_ref[i], kt_ref[i]
        vb, egb, btb = v_ref[i], eg_ref[i], beta_ref[i]
        H = range(H_B)
        Sd = [s0_ref[i, h] * egb[h:h + 1, :] for h in H]
        ksr = [jnp.sum(Sd[h] * kt[:, h:h + 1], axis=0, keepdims=True) for h in H]
        v_new = [btb[h:h + 1, :] * (vb[h:h + 1, :] - ksr[h]) for h in H]
        Sn = [Sd[h] + kt[:, h:h + 1] * v_new[h] for h in H]
        for h in H:
            s_ref[i, h] = Sn[h]
        for h in H:
            o_ref[i, h:h + 1, :] = jnp.sum(Sn[h] * qt[:, h:h + 1], axis=0, keepdims=True)
        return carry

    lax.fori_loop(0, bb, body, 0)


def _gdn_sample_post_kernel(o_ref, z_ref, nw_ref, yb_ref):
    o = o_ref[...]
    z = z_ref[...].astype(F32)
    for h in range(H_B):
        sl = slice(LANES * h, LANES * (h + 1))
        oh = o[:, sl]
        on = oh * lax.rsqrt(jnp.mean(oh * oh, axis=-1, keepdims=True) + RMS_EPS) * nw_ref[...]
        yb_ref[:, sl] = on * _silu(z[:, sl])


def _params(sem):
    return pltpu.CompilerParams(dimension_semantics=sem, vmem_limit_bytes=VMEM_LIMIT)


def _full(shape):
    return pl.BlockSpec(shape, lambda *_: (0,) * len(shape))


def _weight_spec(w, idx):
    lead = len(idx)
    return pl.BlockSpec((None,) * lead + tuple(w.shape[lead:]), lambda *_: tuple(idx) + (0, 0),
                        pipeline_mode=pl.Buffered(1))


def _row_tile(t, cap):
    tm = min(t, cap)
    assert t % tm == 0
    return tm


def _mod_spec(mod):
    rows = mod.shape[2]
    if rows == 1:
        return pl.BlockSpec((3 * N_SUB, 1, 1, D_MODEL), lambda b, i: (0, b, 0, 0))
    return pl.BlockSpec((3 * N_SUB, 1, rows, D_MODEL), lambda b, i: (0, 0, i, 0))


def _ada_call(c_all, w_ada, b_ada):
    depth = w_ada.shape[0]
    n = c_all.shape[0]
    return pl.pallas_call(
        _ada_kernel,
        grid=(depth, 3 * N_SUB),
        in_specs=[pl.BlockSpec((n, D_MODEL), lambda l, j: (0, 0)),
                  pl.BlockSpec((None, D_MODEL, D_MODEL), lambda l, j: (l, 0, j)),
                  pl.BlockSpec((None, 1, D_MODEL), lambda l, j: (l, 0, j))],
        out_specs=pl.BlockSpec((None, None, n, D_MODEL), lambda l, j: (l, j, 0, 0)),
        out_shape=jax.ShapeDtypeStruct((depth, 3 * N_SUB, n, D_MODEL), F32),
        compiler_params=_params(("arbitrary", "arbitrary")),
        name="ada_mod",
    )(c_all, w_ada, b_ada.reshape(depth, 1, 3 * N_SUB * D_MODEL))


def _ffn_call(x, mod, npre, npost, up, down, widx, sub):
    b, t, _ = x.shape
    tm = _row_tile(t, 512)
    xspec = pl.BlockSpec((1, tm, D_MODEL), lambda b, i: (b, i, 0))
    return pl.pallas_call(
        functools.partial(_ffn_kernel, sub=sub),
        grid=(b, t // tm),
        in_specs=[xspec, _mod_spec(mod), _full((1, D_MODEL)), _full((1, D_MODEL)),
                  _weight_spec(up, widx), _weight_spec(down, widx)],
        out_specs=xspec,
        out_shape=jax.ShapeDtypeStruct(x.shape, F32),
        compiler_params=_params(("arbitrary", "arbitrary")),
        name="ffn",
    )(x, mod, npre, npost, up, down)


def _inproj_call(x, mod, npre, w, widx):
    b, t, _ = x.shape
    tm = _row_tile(t, 256)
    widths = (A_COLS, 3 * D_MODEL, D_MODEL, 2 * D_MODEL, AB_COLS)
    dtypes = (F32, F32, BF16, BF16, F32)
    spec = lambda n: pl.BlockSpec((1, tm, n), lambda b, i: (b, i, 0))
    return pl.pallas_call(
        _inproj_kernel,
        grid=(b, t // tm),
        in_specs=[spec(D_MODEL), _mod_spec(mod), _full((1, D_MODEL)), _weight_spec(w, widx)],
        out_specs=[spec(n) for n in widths],
        out_shape=[jax.ShapeDtypeStruct((b, t, n), d) for n, d in zip(widths, dtypes)],
        compiler_params=_params(("arbitrary", "arbitrary")),
        name="in_proj",
    )(x, mod, npre, w)


def _outproj_call(x, ya, yb, pg, mod, npost, w, widx):
    b, t, _ = x.shape
    tm = _row_tile(t, 512)
    spec = lambda n: pl.BlockSpec((1, tm, n), lambda b, i: (b, i, 0))
    return pl.pallas_call(
        _outproj_kernel,
        grid=(b, t // tm),
        in_specs=[spec(D_MODEL), spec(D_MODEL), spec(D_MODEL), spec(2 * D_MODEL), _mod_spec(mod),
                  _full((1, D_MODEL)), _weight_spec(w, widx)],
        out_specs=spec(D_MODEL),
        out_shape=jax.ShapeDtypeStruct(x.shape, F32),
        compiler_params=_params(("arbitrary", "arbitrary")),
        name="out_proj",
    )(x, ya, yb, pg, mod, npost, w)


def _rwkv_prompt_call(pa, rp):
    b, t, _ = pa.shape
    C = RW_C
    R = RW_ROWS
    assert t % C == 0 and b % R == 0
    vec = _full((1, D_MODEL))
    lora = _full((2, LANES, D_MODEL))
    tile3 = _full((N_PAIR * R, 1, LANES))
    per_tile = lambda x: jnp.repeat(x, R, axis=0)
    return pl.pallas_call(
        _rwkv_prompt_kernel,
        grid=(b // R, t // C),
        in_specs=[pl.BlockSpec((R, C, A_COLS), lambda b, i: (b, i, 0)), _full((1, A_COLS)),
                  vec, lora, vec, lora, lora, vec, vec, tile3, tile3, tile3],
        out_specs=[pl.BlockSpec((R, C, D_MODEL), lambda b, i: (b, i, 0)),
                   pl.BlockSpec((R, H_A, HD_A, HD_A), lambda b, i: (b, 0, 0, 0)),
                   pl.BlockSpec((R, 1, A_COLS), lambda b, i: (b, 0, 0))],
        out_shape=[jax.ShapeDtypeStruct((b, t, D_MODEL), BF16),
                   jax.ShapeDtypeStruct((b, H_A, HD_A, HD_A), F32),
                   jax.ShapeDtypeStruct((b, 1, A_COLS), F32)],
        scratch_shapes=[pltpu.VMEM((R, A_COLS), F32), pltpu.VMEM((N_PAIR * R, LANES, LANES), F32)],
        compiler_params=_params(("arbitrary", "arbitrary")),
        name="rwkv_prompt",
    )(pa, rp["mu"], rp["w0"], rp["w2p"], rp["a0"], rp["a2p"], rp["g2"], rp["k_k"], rp["k_a"],
      per_tile(rp["rk3"]), per_tile(rp["lnw3"]), per_tile(rp["lnb3"]))


def _gdn_prompt_call(qkv, ab, z, gp):
    b, t, _ = qkv.shape
    C = GD_C
    assert t % C == 0
    spec = lambda n: pl.BlockSpec((1, C, n), lambda b, i: (b, i, 0))
    return pl.pallas_call(
        _gdn_prompt_kernel,
        grid=(b, t // C),
        in_specs=[spec(3 * D_MODEL), spec(AB_COLS), spec(D_MODEL), _full((CONV_W, 3 * D_MODEL)),
                  _full((1, LANES)), _full((1, LANES)), _full((1, HD_B))],
        out_specs=[spec(D_MODEL),
                   pl.BlockSpec((1, CONV_W - 1, 3 * D_MODEL), lambda b, i: (b, 0, 0)),
                   pl.BlockSpec((1, H_B, HD_B, HD_B), lambda b, i: (b, 0, 0, 0))],
        out_shape=[jax.ShapeDtypeStruct((b, t, D_MODEL), BF16),
                   jax.ShapeDtypeStruct((b, CONV_W - 1, 3 * D_MODEL), F32),
                   jax.ShapeDtypeStruct((b, H_B, HD_B, HD_B), F32)],
        scratch_shapes=[pltpu.VMEM((8 + C, 3 * D_MODEL), F32)],
        compiler_params=_params(("arbitrary", "arbitrary")),
        name="gdn_prompt",
    )(qkv, ab, z, gp["conv_w"], gp["alog"], gp["dtb"], gp["norm_w"])


def _rwkv_sample(pa, shift0, wkv_all, wkv_new, l, rp):
    n = pa.shape[0]
    vec = _full((1, D_MODEL))
    lora = _full((2, LANES, D_MODEL))
    chan = _full((D_MODEL, n))
    r, w, km, v, kkn, bb_, g = pl.pallas_call(
        _rwkv_sample_prep_kernel,
        grid=(1,),
        in_specs=[_full((n, A_COLS)), _full((n, A_COLS)), _full((1, A_COLS)), vec, lora, vec, lora, lora,
                  vec, vec],
        out_specs=[chan] * 7,
        out_shape=[jax.ShapeDtypeStruct((D_MODEL, n), F32)] * 7,
        compiler_params=_params(("arbitrary",)),
        name="rwkv_sample_prep",
    )(pa, shift0, rp["mu"], rp["w0"], rp["w2p"], rp["a0"], rp["a2p"], rp["g2"], rp["k_k"], rp["k_a"])

    hspec = pl.BlockSpec((HD_A, n), lambda h: (h, 0))
    sspec = pl.BlockSpec((None, None, HD_A, HD_A, n), lambda h: (l, h, 0, 0, 0))
    s_new, y = pl.pallas_call(
        _rwkv_sample_state_kernel,
        grid=(H_A,),
        in_specs=[hspec] * 6 + [sspec, pl.BlockSpec(memory_space=pl.ANY)],
        out_specs=[sspec, hspec],
        out_shape=[jax.ShapeDtypeStruct(wkv_all.shape, F32), jax.ShapeDtypeStruct((D_MODEL, n), F32)],
        input_output_aliases={7: 0},
        compiler_params=_params(("arbitrary",)),
        name="rwkv_sample_state",
    )(r, w, km, kkn, bb_, v, wkv_all, wkv_new)

    col = _full((D_MODEL, 1))
    ya = pl.pallas_call(
        _rwkv_sample_post_kernel,
        grid=(1,),
        in_specs=[chan] * 5 + [col] * 3,
        out_specs=_full((n, D_MODEL)),
        out_shape=jax.ShapeDtypeStruct((n, D_MODEL), F32),
        compiler_params=_params(("arbitrary",)),
        name="rwkv_sample_post",
    )(y, r, km, v, g, rp["rk_col"], rp["lnw_col"], rp["lnb_col"])
    return ya, s_new


def _gdn_sample(qkv, conv0, ssm_all, ssm_new, l, ab, z, gp):
    n = qkv.shape[0]
    flat = _full((n, D_MODEL))
    cspec = _full((CONV_W - 1, n, 3 * D_MODEL))
    q, k, v, eg, beta, cnew = pl.pallas_call(
        _gdn_sample_prep_kernel,
        grid=(1,),
        in_specs=[_full((n, 3 * D_MODEL)), cspec, _full((n, AB_COLS)), _full((CONV_W, 3 * D_MODEL)),
                  _full((1, LANES)), _full((1, LANES))],
        out_specs=[flat] * 5 + [cspec],
        out_shape=[jax.ShapeDtypeStruct((n, D_MODEL), F32)] * 5
        + [jax.ShapeDtypeStruct((CONV_W - 1, n, 3 * D_MODEL), F32)],
        compiler_params=_params(("arbitrary",)),
        name="gdn_sample_prep",
    )(qkv, jnp.swapaxes(conv0, 0, 1), ab, gp["conv_w"], gp["alog"], gp["dtb"])

    bb = 4
    assert n % bb == 0
    heads = lambda x: x.reshape(n, H_B, HD_B)
    hspec = pl.BlockSpec((bb, H_B, HD_B), lambda i: (i, 0, 0))
    tspec = pl.BlockSpec((bb, HD_B, H_B), lambda i: (i, 0, 0))
    sspec = pl.BlockSpec((None, bb, H_B, HD_B, HD_B), lambda i: (l, i, 0, 0, 0))
    s_new, o = pl.pallas_call(
        functools.partial(_gdn_sample_state_kernel, bb=bb),
        grid=(n // bb,),
        in_specs=[tspec, tspec, hspec, hspec, hspec, sspec, pl.BlockSpec(memory_space=pl.ANY)],
        out_specs=[sspec, hspec],
        out_shape=[jax.ShapeDtypeStruct(ssm_all.shape, F32), jax.ShapeDtypeStruct((n, H_B, HD_B), F32)],
        input_output_aliases={6: 0},
        compiler_params=_params(("arbitrary",)),
        name="gdn_sample_state",
    )(jnp.swapaxes(heads(q), 1, 2), jnp.swapaxes(heads(k), 1, 2), heads(v), heads(eg), heads(beta), ssm_all,
      ssm_new)

    yb = pl.pallas_call(
        _gdn_sample_post_kernel,
        grid=(1,),
        in_specs=[flat, flat, _full((1, HD_B))],
        out_specs=flat,
        out_shape=jax.ShapeDtypeStruct((n, D_MODEL), F32),
        compiler_params=_params(("arbitrary",)),
        name="gdn_sample_post",
    )(o.reshape(n, D_MODEL), z, gp["norm_w"])
    return yb, jnp.swapaxes(cnew, 0, 1), s_new


def _hi_lo(w):
    hi = w.astype(BF16)
    return jnp.stack([hi, (w - hi.astype(F32)).astype(BF16)])


def _pad_rows(x, rows, at):
    out = jnp.zeros((rows, x.shape[1]), x.dtype)
    return lax.dynamic_update_slice(out, x, (at, 0))


def kernel(x_prompt, x_sample, c_prompt, c_sample, state_rwkv_shift, state_rwkv_wkv, state_gdn_conv, state_gdn_ssm, w_ada, b_ada, norm_pre, norm_post, ffn_up, ffn_down, w_in, w_out, rwkv_mu, rwkv_w0, rwkv_w2, rwkv_a0, rwkv_a2, rwkv_g2, rwkv_k_k, rwkv_k_a, rwkv_r_k, rwkv_ln_w, rwkv_ln_b, gdn_conv, gdn_a_log, gdn_dt_bias, gdn_norm_w):
    depth = w_ada.shape[0]
    bp = x_prompt.shape[0]
    ns = x_sample.shape[0]

    n_c = bp + ns
    n_cp = -(-n_c // 16) * 16
    c_all = jnp.concatenate([c_prompt, c_sample, jnp.zeros((n_cp - n_c, D_MODEL), F32)], axis=0)
    mod = _ada_call(c_all, w_ada, b_ada)

    o_b = A_COLS
    o_ab = o_b + 3 * D_MODEL
    o_z = o_ab + 2 * H_B
    o_g = o_z + D_MODEL
    w_in_r = jnp.concatenate(
        [w_in[:, :, 0:o_b], w_in[:, :, o_b:o_ab], w_in[:, :, o_z:o_g], w_in[:, :, o_g:],
         w_in[:, :, o_ab:o_z], jnp.zeros((depth, D_MODEL, AB_COLS - 2 * H_B), F32)], axis=-1).astype(BF16)
    up_b = ffn_up.astype(BF16)
    down_b = ffn_down.astype(BF16)
    w_out_b = w_out.astype(BF16)

    wkv_t = jnp.transpose(state_rwkv_wkv, (0, 2, 3, 4, 1))

    yp = x_prompt
    ys = x_sample.reshape(1, ns, D_MODEL)
    outs = [[] for _ in range(6)]
    s_wkv = jnp.zeros(wkv_t.shape, F32)
    s_ssm = jnp.zeros(state_gdn_ssm.shape, F32)
    for l in range(depth):
        mod_p = mod[l, :, 0:bp].reshape(3 * N_SUB, bp, 1, D_MODEL)
        mod_s = mod[l, :, bp:bp + ns].reshape(3 * N_SUB, 1, ns, D_MODEL)
        npre = [norm_pre[l, i][None] for i in range(N_SUB)]
        npost = [norm_post[l, i][None] for i in range(N_SUB)]
        row = lambda x: x[l][None]
        tile3 = lambda x: x[l].reshape(N_PAIR, 1, LANES)
        rp = dict(mu=row(rwkv_mu), w0=row(rwkv_w0), a0=row(rwkv_a0), k_k=row(rwkv_k_k), k_a=row(rwkv_k_a),
                  w2p=_hi_lo(_pad_rows(rwkv_w2[l], LANES, 0)), a2p=_hi_lo(_pad_rows(rwkv_a2[l], LANES, W_LORA)),
                  g2=_hi_lo(rwkv_g2[l]), rk3=tile3(rwkv_r_k), lnw3=tile3(rwkv_ln_w), lnb3=tile3(rwkv_ln_b),
                  rk_col=rwkv_r_k[l].reshape(D_MODEL, 1), lnw_col=rwkv_ln_w[l].reshape(D_MODEL, 1),
                  lnb_col=rwkv_ln_b[l].reshape(D_MODEL, 1))
        lane_row = lambda x: jnp.zeros((1, LANES), F32).at[0, 0:H_B].set(x[l])
        gp = dict(conv_w=gdn_conv[l], norm_w=row(gdn_norm_w), alog=lane_row(gdn_a_log),
                  dtb=lane_row(gdn_dt_bias))

        yp = _ffn_call(yp, mod_p, npre[0], npost[0], up_b, down_b, (l, 0), 0)
        pa, qkv, z, pg, ab = _inproj_call(yp, mod_p, npre[1], w_in_r, (l,))
        ya, p_wkv, p_shift = _rwkv_prompt_call(pa, rp)
        yb, p_conv, p_ssm = _gdn_prompt_call(qkv, ab, z, gp)
        yp = _outproj_call(yp, ya, yb, pg, mod_p, npost[1], w_out_b, (l,))
        yp = _ffn_call(yp, mod_p, npre[2], npost[2], up_b, down_b, (l, 1), 2)

        ys = _ffn_call(ys, mod_s, npre[0], npost[0], up_b, down_b, (l, 0), 0)
        pa, qkv, z, pg, ab = _inproj_call(ys, mod_s, npre[1], w_in_r, (l,))
        ya, s_wkv = _rwkv_sample(pa[0], state_rwkv_shift[l], wkv_t, s_wkv, l, rp)
        yb, s_conv, s_ssm = _gdn_sample(qkv[0], state_gdn_conv[l], state_gdn_ssm, s_ssm, l, ab[0], z[0], gp)
        ys = _outproj_call(ys, ya[None], yb[None], pg, mod_s, npost[1], w_out_b, (l,))
        ys = _ffn_call(ys, mod_s, npre[2], npost[2], up_b, down_b, (l, 1), 2)

        for lst, val in zip(outs, (p_shift[:, 0], p_wkv, p_conv, p_ssm, pa[0], s_conv)):
            lst.append(val)

    st = [jnp.stack(o) for o in outs]
    return (yp, ys.reshape(ns, 1, D_MODEL), st[0], st[1], st[2], st[3], st[4],
            jnp.transpose(s_wkv, (0, 4, 1, 2, 3)), st[5], s_ssm)
```

```python
import functools

import jax
import jax.numpy as jnp
from jax import lax
from jax.experimental import pallas as pl
from jax.experimental.pallas import tpu as pltpu

F32 = jnp.float32
BF16 = jnp.bfloat16

D_MODEL = 1024
HD_A = 64
H_A = D_MODEL // HD_A
W_LORA = 64
A_LORA = 64
G_LORA = 128
A_COLS = 3 * D_MODEL + W_LORA + A_LORA + G_LORA
RWKV_GN_EPS = 6.4e-4
HD_B = 128
H_B = D_MODEL // HD_B
CONV_W = 4
D_FF = 2816
N_SUB = 3
FFN_RES = 0.5
RMS_EPS = 1e-6

LANES = 128
N_PAIR = D_MODEL // LANES
RW_C = 64
RW_ROWS = 2
GD_C = 128
AB_COLS = LANES
IN_COLS_R = A_COLS + 3 * D_MODEL + D_MODEL + 2 * D_MODEL + AB_COLS
VMEM_LIMIT = 56 * 1024 * 1024
REC_PASSES = 1

_NN = (((1,), (0,)), ((), ()))
_NT = (((1,), (1,)), ((), ()))


def _dg(a, b, dn=_NN):
    return lax.dot_general(a, b, dn, preferred_element_type=F32)


def _split2(x):
    hi = x.astype(BF16)
    lo = (x - hi.astype(F32)).astype(BF16)
    return hi, lo


def _split3(x):
    h1 = x.astype(BF16)
    r1 = x - h1.astype(F32)
    h2 = r1.astype(BF16)
    h3 = (r1 - h2.astype(F32)).astype(BF16)
    return h1, h2, h3


def _mm(a, b, dn=_NN, passes=REC_PASSES):
    if passes == 1:
        return _dg(a.astype(BF16), b.astype(BF16), dn)
    ah, al = _split2(a)
    bh, bl = _split2(b)
    return _dg(ah, bh, dn) + (_dg(ah, bl, dn) + _dg(al, bh, dn))


def _mm_sel_rhs(x, e, pieces):
    parts = _split3(x) if pieces == 3 else _split2(x)
    out = _dg(parts[0], e)
    for p in parts[1:]:
        out = out + _dg(p, e)
    return out


def _mm_sel_lhs(e, x, pieces):
    parts = _split3(x) if pieces == 3 else _split2(x)
    out = _dg(e, parts[0])
    for p in parts[1:]:
        out = out + _dg(e, p)
    return out


def _iota2(shape, dim):
    return lax.broadcasted_iota(jnp.int32, shape, dim)


def _softplus(x):
    return jnp.maximum(x, 0.0) + jnp.log(1.0 + jnp.exp(-jnp.abs(x)))


def _silu(x):
    return x * jax.nn.sigmoid(x)


def _rms(x, w):
    return x * lax.rsqrt(jnp.mean(x * x, axis=-1, keepdims=True) + RMS_EPS) * w


def _mod_in(x, npre, mod_ref, sub):
    shift = mod_ref[3 * sub, 0]
    scale = mod_ref[3 * sub + 1, 0]
    return _rms(x, npre) * (1.0 + scale) + shift


def _tri_masks(n, top):
    i = _iota2((n, n), 0)
    j = _iota2((n, n), 1)
    masks = []
    s = 1
    while s <= top:
        sh = s.bit_length() - 1
        same = (i >> (sh + 1)) == (j >> (sh + 1))
        masks.append(same & (((i >> sh) & 1) == 1) & (((j >> sh) & 1) == 0))
        s *= 2
    return masks


def _tri_inv(Ls, masks):
    n = Ls[0].shape[0]
    eye = jnp.where(_iota2((n, n), 0) == _iota2((n, n), 1), 1.0, 0.0).astype(F32)
    Xs = [eye - jnp.where(masks[0], L, 0.0) for L in Ls]
    Lb = [L.astype(BF16) for L in Ls]
    zero = jnp.zeros((), BF16)
    for m in masks[1:]:
        Xb = [X.astype(BF16) for X in Xs]
        Ts = [_dg(xb, jnp.where(m, lb, zero)) for xb, lb in zip(Xb, Lb)]
        Xs = [X - _dg(T.astype(BF16), xb) for X, T, xb in zip(Xs, Ts, Xb)]
    return Xs


def _to_tiles(x):
    return jnp.stack([x[:, LANES * p:LANES * (p + 1)] for p in range(N_PAIR)], axis=0)


def _pair_headsum(x3, ebd):
    p, n, l = x3.shape
    return _mm_sel_rhs(x3.reshape(p * n, l), ebd, 2).reshape(p, n, l)


def _ebd():
    i = _iota2((LANES, LANES), 0)
    j = _iota2((LANES, LANES), 1)
    return jnp.where((i >> 6) == (j >> 6), 1.0, 0.0).astype(BF16)


def _mm_lora(a, w_ref):
    ah, al = _split2(a)
    return _dg(ah, w_ref[0]) + (_dg(ah, w_ref[1]) + _dg(al, w_ref[0]))


def _rwkv_prep(pa, prev, mu, w0, w2p, a0, a2p, g2, k_k, k_a):
    xs = pa + mu * (prev - pa)
    r = xs[:, 0:D_MODEL]
    k = xs[:, D_MODEL:2 * D_MODEL]
    v = xs[:, 2 * D_MODEL:3 * D_MODEL]
    wa = xs[:, 3 * D_MODEL:3 * D_MODEL + LANES]
    gd = xs[:, 3 * D_MODEL + LANES:A_COLS]
    w_lin = _mm_lora(jnp.tanh(wa), w2p)
    a_lin = _mm_lora(wa, a2p)
    g = _mm_lora(jax.nn.sigmoid(gd), g2)
    w_log = -_softplus(-(w0 + w_lin)) - 0.5
    lw = -jnp.exp(w_log)
    a = jax.nn.sigmoid(a0 + a_lin)
    kkr = k * k_k
    km = k * (1.0 + (a - 1.0) * k_a)
    return r, km, v, kkr, a, lw, g


def _rwkv_epilogue(y3, r3, km3, v3, rk3, lnw3, lnb3, ebd):
    mean = _pair_headsum(y3, ebd) * (1.0 / HD_A)
    yc = y3 - mean
    var = _pair_headsum(yc * yc, ebd) * (1.0 / HD_A)
    yn = yc * lax.rsqrt(var + RWKV_GN_EPS) * lnw3 + lnb3
    bonus = _pair_headsum(r3 * km3 * rk3, ebd) * v3
    return yn + bonus


def _gdn_gates(ab, alog, dtb):
    g_all = -jnp.exp(alog) * _softplus(ab + dtb)
    beta_all = jax.nn.sigmoid(ab)
    return g_all, beta_all


def _sel_matrix(offset):
    i = _iota2((LANES, D_MODEL), 0)
    j = _iota2((LANES, D_MODEL), 1)
    return jnp.where(i == (j >> 7) + offset, 1.0, 0.0).astype(BF16)


def _head_l2norm(x, eps):
    outs = []
    for h in range(H_B):
        xh = x[:, LANES * h:LANES * (h + 1)]
        outs.append(xh * lax.rsqrt(jnp.sum(xh * xh, axis=-1, keepdims=True) + eps))
    return outs


def _ada_kernel(c_ref, w_ref, b_ref, o_ref):
    s = _silu(c_ref[...]).astype(BF16)
    o_ref[...] = _dg(s, w_ref[...].astype(BF16)) + b_ref[...]


def _ffn_kernel(x_ref, mod_ref, npre_ref, npost_ref, up_ref, down_ref, o_ref, *, sub):
    x = x_ref[0]
    h = _mod_in(x, npre_ref[...], mod_ref, sub).astype(BF16)
    gate = _dg(h, up_ref[:, 0:D_FF])
    val = _dg(h, up_ref[:, D_FF:2 * D_FF])
    act = (_silu(gate) * val).astype(BF16)
    y = _dg(act, down_ref[...])
    o_ref[0] = x + FFN_RES * mod_ref[3 * sub + 2, 0] * _rms(y, npost_ref[...])


def _inproj_kernel(x_ref, mod_ref, npre_ref, w_ref, pa_ref, qkv_ref, z_ref, pg_ref, ab_ref):
    h = _mod_in(x_ref[0], npre_ref[...], mod_ref, 1).astype(BF16)
    o = 0
    for ref in (pa_ref, qkv_ref, z_ref, pg_ref, ab_ref):
        n = ref.shape[-1]
        ref[0] = _dg(h, w_ref[:, o:o + n]).astype(ref.dtype)
        o += n


def _causal_conv_silu(x, hist_ref, cw_ref):
    n = x.shape[0]
    hist_ref[8:8 + n, :] = x
    conv = x * cw_ref[CONV_W - 1:CONV_W, :]
    for s in range(1, CONV_W):
        conv = conv + hist_ref[8 - s:8 - s + n, :] * cw_ref[CONV_W - 1 - s:CONV_W - s, :]
    hist_ref[0:8, :] = x[n - 8:n]
    return _silu(conv)


def _outproj_kernel(x_ref, ya_ref, yb_ref, pg_ref, mod_ref, npost_ref, w_ref, o_ref):
    pg = pg_ref[0].astype(F32)
    ya = ya_ref[0].astype(F32)
    yb = yb_ref[0].astype(F32)
    m = jax.nn.sigmoid(pg[:, 0:D_MODEL]) * ya + jax.nn.sigmoid(pg[:, D_MODEL:]) * yb
    y = _dg(m.astype(BF16), w_ref[...])
    o_ref[0] = x_ref[0] + mod_ref[3 * 1 + 2, 0] * _rms(y, npost_ref[...])


def _rwkv_prompt_kernel(pa_ref, mu_ref, w0_ref, w2_ref, a0_ref, a2_ref, g2_ref, kk_ref, ka_ref,
                        rk3_ref, lnw3_ref, lnb3_ref, ya_ref, s_ref, shift_ref, carry_ref, sbd_ref):
    C = RW_C
    R = RW_ROWS
    N = R * C
    t = pl.program_id(1)

    @pl.when(t == 0)
    def _():
        carry_ref[...] = jnp.zeros_like(carry_ref)
        sbd_ref[...] = jnp.zeros_like(sbd_ref)

    pa = pa_ref[...].reshape(N, A_COLS)
    last = jnp.concatenate([jnp.broadcast_to(carry_ref[s:s + 1, :], (C, A_COLS)) for s in range(R)], axis=0)
    prev = jnp.where((_iota2((N, 1), 0) & (C - 1)) == 0, last, pltpu.roll(pa, 1, 0))
    for s in range(R):
        carry_ref[s:s + 1, :] = pa[s * C + C - 1:s * C + C]
    r, km, v, kkr, a, lw, g = _rwkv_prep(pa, prev, mu_ref[...], w0_ref[...], w2_ref, a0_ref[...],
                                         a2_ref, g2_ref, kk_ref[...], ka_ref[...])

    ti = _iota2((N, N), 0)
    tj = _iota2((N, N), 1)
    csh = C.bit_length() - 1
    tri = jnp.where((ti >= tj) & ((ti >> csh) == (tj >> csh)), 1.0, 0.0).astype(BF16)
    cum = _mm_sel_lhs(tri, lw, 3)
    ebd = _ebd()
    NQ = N_PAIR * R
    tiles = lambda x: _to_tiles(x).reshape(NQ, C, LANES)
    r3, km3, v3, kkr3, a3, lw3, cum3 = (tiles(x) for x in (r, km, v, kkr, a, lw, cum))
    kk3 = kkr3 * lax.rsqrt(_pair_headsum(kkr3 * kkr3, ebd) + 1e-12)
    b3 = kk3 * a3
    cl = cum3[:, C - 1:C, :]
    e_neg = jnp.exp(-cum3)
    e_end = jnp.exp(cl - cum3)
    rh3 = r3 * jnp.exp(cum3)
    ah3 = kk3 * jnp.exp(cum3 - lw3)
    bt3 = b3 * e_neg
    kt3 = km3 * e_neg
    be3 = b3 * e_end
    ke3 = km3 * e_end
    e_last3 = jnp.exp(cl)

    lane0 = _iota2((1, LANES), 1) < HD_A

    def sm(x):
        return jnp.concatenate([jnp.where(lane0, x, 0.0), jnp.where(lane0, 0.0, x)], axis=0)

    ii = _iota2((2 * C, 2 * C), 0)
    jj = _iota2((2 * C, 2 * C), 1)
    strict = jj < ii
    incl = jj <= ii
    masks = _tri_masks(2 * C, C // 2)

    P = range(NQ)
    T2 = 2 * C
    bsm = lambda x: sm(x).astype(BF16)
    ARs = [jnp.concatenate([bsm(ah3[p]), bsm(rh3[p])], axis=0) for p in P]
    BKs = [jnp.concatenate([bsm(bt3[p]), bsm(kt3[p])], axis=0) for p in P]
    BKe = [jnp.concatenate([bsm(be3[p]), bsm(ke3[p])], axis=0) for p in P]
    Vs = [sm(v3[p]) for p in P]
    Vb = [V.astype(BF16) for V in Vs]
    Ss = [sbd_ref[p] for p in P]
    QQ = [_dg(ARs[p], BKs[p], _NT) for p in P]
    QS = [_dg(ARs[p], Ss[p].astype(BF16), _NT) for p in P]
    Lab = [jnp.where(strict, Q[0:T2, 0:T2], 0.0) for Q in QQ]
    Lak = [jnp.where(strict, Q[0:T2, T2:2 * T2], 0.0).astype(BF16) for Q in QQ]
    Ms = [QS[p][0:T2] + _dg(Lak[p], Vb[p]) for p in P]
    Tinv = _tri_inv(Lab, masks)
    Us = [-_dg(Tinv[p].astype(BF16), Ms[p].astype(BF16)) for p in P]
    UVb = [jnp.concatenate([Us[p].astype(BF16), Vb[p]], axis=0) for p in P]
    incl2 = jnp.concatenate([incl, incl], axis=1)
    Arbk = [jnp.where(incl2, Q[T2:2 * T2, :], 0.0).astype(BF16) for Q in QQ]
    Ys = [QS[p][T2:2 * T2] + _dg(Arbk[p], UVb[p]) for p in P]
    for p in P:
        UVt = jnp.concatenate([Us[p].T, Vs[p].T], axis=1).astype(BF16)
        sbd_ref[p] = Ss[p] * e_last3[p] + _dg(UVt, BKe[p])
    y3 = jnp.stack([Y[0:C] + Y[C:2 * C] for Y in Ys], axis=0)

    out3 = _rwkv_epilogue(y3, r3, km3, v3, rk3_ref[...], lnw3_ref[...], lnb3_ref[...], ebd)
    for q in P:
        p, s = divmod(q, R)
        ya_ref[s, :, LANES * p:LANES * (p + 1)] = (
            out3[q] * g[s * C:(s + 1) * C, LANES * p:LANES * (p + 1)]).astype(ya_ref.dtype)

    @pl.when(t == pl.num_programs(1) - 1)
    def _():
        for s in range(R):
            shift_ref[s] = pa[s * C + C - 1:s * C + C]
        for q in P:
            p, s = divmod(q, R)
            S = sbd_ref[q]
            s_ref[s, 2 * p] = S[0:HD_A, 0:HD_A]
            s_ref[s, 2 * p + 1] = pltpu.roll(S, HD_A, 1)[HD_A:LANES, 0:HD_A]


def _gdn_prompt_kernel(qkv_ref, ab_ref, z_ref, cw_ref, alog_ref, dtb_ref, nw_ref,
                       yb_ref, conv_ref, ssm_ref, hist_ref):
    C = GD_C
    t = pl.program_id(1)

    @pl.when(t == 0)
    def _():
        hist_ref[0:8, :] = jnp.zeros((8, 3 * D_MODEL), F32)
        ssm_ref[...] = jnp.zeros_like(ssm_ref)

    x = qkv_ref[0]
    qkvc = _causal_conv_silu(x, hist_ref, cw_ref)
    qs = _head_l2norm(qkvc[:, 0:D_MODEL], 1e-6)
    ks = _head_l2norm(qkvc[:, D_MODEL:2 * D_MODEL], 1e-6)
    vv = qkvc[:, 2 * D_MODEL:3 * D_MODEL]

    g_all, beta_all = _gdn_gates(ab_ref[0], alog_ref[...], dtb_ref[...])
    tri = jnp.where(_iota2((C, C), 0) >= _iota2((C, C), 1), 1.0, 0.0).astype(BF16)
    gc_all = _mm_sel_lhs(tri, g_all, 3)

    ii = _iota2((C, C), 0)
    jj = _iota2((C, C), 1)
    strict = jj < ii
    incl = jj <= ii
    masks = _tri_masks(C, C // 2)
    z = z_ref[0].astype(F32)

    H = range(H_B)
    sls = [slice(LANES * h, LANES * (h + 1)) for h in H]
    qh = [qs[h] * (HD_B ** -0.5) for h in H]
    bcast = lambda col: jnp.broadcast_to(col, (C, LANES))
    gh = [bcast(gc_all[:, h:h + 1]) for h in H]
    bh = [bcast(beta_all[:, H_B + h:H_B + h + 1]) for h in H]
    gct = gc_all.T
    dec = [jnp.exp(gh[h] - gct[h:h + 1, :]) for h in H]
    eg = [jnp.exp(g) for g in gh]
    glast = [g[C - 1:C, :] for g in gh]
    kb = [ks[h] * bh[h] for h in H]
    KQ = [_dg(jnp.concatenate([kb[h], qh[h]], axis=0).astype(BF16), ks[h].astype(BF16), _NT) for h in H]
    Ls = [jnp.where(strict, KQ[h][0:C] * dec[h], 0.0) for h in H]
    attn = [jnp.where(incl, KQ[h][C:2 * C] * dec[h], 0.0) for h in H]
    Tinv = _tri_inv(Ls, masks)
    rhs = [jnp.concatenate([vv[:, sls[h]] * bh[h], kb[h] * eg[h]], axis=1).astype(BF16) for h in H]
    UW = [_dg(Tinv[h].astype(BF16), rhs[h]) for h in H]
    Ss = [ssm_ref[0, h] for h in H]
    Sb = [S.astype(BF16) for S in Ss]
    v_new = [UW[h][:, 0:LANES] - _dg(UW[h][:, LANES:2 * LANES].astype(BF16), Sb[h]) for h in H]
    vnb = [v.astype(BF16) for v in v_new]
    os_ = [_dg(jnp.concatenate([qh[h] * eg[h], attn[h]], axis=1).astype(BF16),
               jnp.concatenate([Sb[h], vnb[h]], axis=0)) for h in H]
    for h in H:
        k_dec = ks[h] * jnp.exp(glast[h] - gh[h])
        ssm_ref[0, h] = Ss[h] * jnp.exp(glast[h]) + _dg(k_dec.T.astype(BF16), vnb[h])
    for h in H:
        o = os_[h]
        on = o * lax.rsqrt(jnp.mean(o * o, axis=-1, keepdims=True) + RMS_EPS) * nw_ref[...]
        yb_ref[0, :, sls[h]] = (on * _silu(z[:, sls[h]])).astype(yb_ref.dtype)

    @pl.when(t == pl.num_programs(1) - 1)
    def _():
        conv_ref[0] = x[C - (CONV_W - 1):C]


def _rwkv_sample_prep_kernel(pa_ref, prev_ref, mu_ref, w0_ref, w2_ref, a0_ref, a2_ref, g2_ref, kk_ref,
                             ka_ref, r_ref, w_ref, km_ref, v_ref, kkn_ref, b_ref, g_ref):
    r, km, v, kkr, a, lw, g = _rwkv_prep(pa_ref[...], prev_ref[...], mu_ref[...], w0_ref[...], w2_ref,
                                         a0_ref[...], a2_ref, g2_ref, kk_ref[...], ka_ref[...])
    n = r.shape[0]
    heads = lambda x: x.reshape(H_A, HD_A, n)
    kk3 = heads(kkr.T)
    kkn = kk3 * lax.rsqrt(jnp.sum(kk3 * kk3, axis=1, keepdims=True) + 1e-12)
    r_ref[...] = r.T
    w_ref[...] = jnp.exp(lw).T
    km_ref[...] = km.T
    v_ref[...] = v.T
    kkn_ref[...] = kkn.reshape(D_MODEL, n)
    b_ref[...] = (kkn * heads(a.T)).reshape(D_MODEL, n)
    g_ref[...] = g.T


def _rwkv_sample_state_kernel(r_ref, w_ref, km_ref, kkn_ref, b_ref, v_ref, s0_ref, stacked_ref, s_ref, y_ref):
    del stacked_ref
    rh, wh, kh, kkh, bh = r_ref[...], w_ref[...], km_ref[...], kkn_ref[...], b_ref[...]
    G = 8
    row = _iota2((G, 1), 0)

    def body(j, carry):
        base = pl.multiple_of(j * G, G)
        vrows = v_ref[pl.ds(base, G), :]
        U = range(G)
        Ss = [s0_ref[base + u] for u in U]
        sas = [-jnp.sum(Ss[u] * kkh, axis=0, keepdims=True) for u in U]
        Sn = [Ss[u] * wh + sas[u] * bh + vrows[u:u + 1, :] * kh for u in U]
        for u in U:
            s_ref[base + u] = Sn[u]
        yrows = [jnp.sum(Sn[u] * rh, axis=0, keepdims=True) for u in U]
        y = jnp.zeros_like(vrows)
        for u in U:
            y = jnp.where(row == u, yrows[u], y)
        y_ref[pl.ds(base, G), :] = y
        return carry

    lax.fori_loop(0, HD_A // G, body, 0)


def _rwkv_sample_post_kernel(y_ref, r_ref, km_ref, v_ref, g_ref, rk_ref, lnw_ref, lnb_ref, ya_ref):
    n = y_ref.shape[1]
    heads = lambda x: x.reshape(H_A, HD_A, n)
    y3 = heads(y_ref[...])
    yc = y3 - jnp.mean(y3, axis=1, keepdims=True)
    var = jnp.mean(yc * yc, axis=1, keepdims=True)
    yn = (yc * lax.rsqrt(var + RWKV_GN_EPS)).reshape(D_MODEL, n) * lnw_ref[...] + lnb_ref[...]
    bonus = jnp.sum(heads(r_ref[...] * km_ref[...] * rk_ref[...]), axis=1, keepdims=True) * heads(v_ref[...])
    out = (yn + bonus.reshape(D_MODEL, n)) * g_ref[...]
    ya_ref[...] = out.T


def _gdn_sample_prep_kernel(qkv_ref, c0_ref, ab_ref, cw_ref, alog_ref, dtb_ref,
                            q_ref, k_ref, v_ref, eg_ref, beta_ref, cnew_ref):
    x = qkv_ref[...]
    conv = x * cw_ref[CONV_W - 1:CONV_W, :]
    for j in range(CONV_W - 1):
        conv = conv + c0_ref[j] * cw_ref[j:j + 1, :]
    for j in range(CONV_W - 2):
        cnew_ref[j] = c0_ref[j + 1]
    cnew_ref[CONV_W - 2] = x
    qkvc = _silu(conv)
    qs = _head_l2norm(qkvc[:, 0:D_MODEL], 1e-6)
    ks = _head_l2norm(qkvc[:, D_MODEL:2 * D_MODEL], 1e-6)
    for h in range(H_B):
        q_ref[:, LANES * h:LANES * (h + 1)] = qs[h] * (HD_B ** -0.5)
        k_ref[:, LANES * h:LANES * (h + 1)] = ks[h]
    v_ref[...] = qkvc[:, 2 * D_MODEL:3 * D_MODEL]
    g_all, beta_all = _gdn_gates(ab_ref[...], alog_ref[...], dtb_ref[...])
    eg_ref[...] = jnp.exp(_mm_sel_rhs(g_all, _sel_matrix(0), 3))
    beta_ref[...] = _mm_sel_rhs(beta_all, _sel_matrix(H_B), 3)


def _gdn_sample_state_kernel(qt_ref, kt_ref, v_ref, eg_ref, beta_ref, s0_ref, stacked_ref, s_ref, o_ref, *, bb):
    del stacked_ref

    def body(i, carry):
        qt, kt = qt_ref[i], kt_ref[i]
        vb, egb, btb = v_ref[i], eg_ref[i], beta_ref[i]
        H = range(H_B)
        kc = [jnp.broadcast_to(kt[:, h:h + 1], (HD_B, HD_B)) for h in H]
        Sd = [s0_ref[i, h] * egb[h:h + 1, :] for h in H]
        ksr = [jnp.sum(Sd[h] * kc[h], axis=0, keepdims=True) for h in H]
        v_new = [btb[h:h + 1, :] * (vb[h:h + 1, :] - ksr[h]) for h in H]
        Sn = [Sd[h] + kc[h] * v_new[h] for h in H]
        for h in H:
            s_ref[i, h] = Sn[h]
        for h in H:
            o_ref[i, h:h + 1, :] = jnp.sum(Sn[h] * qt[:, h:h + 1], axis=0, keepdims=True)
        return carry

    lax.fori_loop(0, bb, body, 0, unroll=2)


def _gdn_sample_post_kernel(o_ref, z_ref, nw_ref, yb_ref):
    o = o_ref[...]
    z = z_ref[...].astype(F32)
    for h in range(H_B):
        sl = slice(LANES * h, LANES * (h + 1))
        oh = o[:, sl]
        on = oh * lax.rsqrt(jnp.mean(oh * oh, axis=-1, keepdims=True) + RMS_EPS) * nw_ref[...]
        yb_ref[:, sl] = on * _silu(z[:, sl])


def _params(sem):
    return pltpu.CompilerParams(dimension_semantics=sem, vmem_limit_bytes=VMEM_LIMIT)


def _full(shape):
    return pl.BlockSpec(shape, lambda *_: (0,) * len(shape))


_RESIDENT = pl.BlockSpec(memory_space=pltpu.VMEM)


def _row_tile(t, cap):
    tm = min(t, cap)
    assert t % tm == 0
    return tm


def _mod_spec(mod):
    rows = mod.shape[2]
    if rows == 1:
        return pl.BlockSpec((3 * N_SUB, 1, 1, D_MODEL), lambda b, i: (0, b, 0, 0))
    return pl.BlockSpec((3 * N_SUB, 1, rows, D_MODEL), lambda b, i: (0, 0, i, 0))


def _ada_call(c_all, w_ada, b_ada):
    depth = w_ada.shape[0]
    n = c_all.shape[0]
    return pl.pallas_call(
        _ada_kernel,
        grid=(depth, 3 * N_SUB),
        in_specs=[pl.BlockSpec((n, D_MODEL), lambda l, j: (0, 0)),
                  pl.BlockSpec((None, D_MODEL, D_MODEL), lambda l, j: (l, 0, j)),
                  pl.BlockSpec((None, 1, D_MODEL), lambda l, j: (l, 0, j))],
        out_specs=pl.BlockSpec((None, None, n, D_MODEL), lambda l, j: (l, j, 0, 0)),
        out_shape=jax.ShapeDtypeStruct((depth, 3 * N_SUB, n, D_MODEL), F32),
        compiler_params=_params(("arbitrary", "arbitrary")),
        name="ada_mod",
    )(c_all, w_ada, b_ada.reshape(depth, 1, 3 * N_SUB * D_MODEL))


def _ffn_call(x, mod, npre, npost, up, down, sub):
    b, t, _ = x.shape
    tm = _row_tile(t, 512)
    xspec = pl.BlockSpec((1, tm, D_MODEL), lambda b, i: (b, i, 0))
    return pl.pallas_call(
        functools.partial(_ffn_kernel, sub=sub),
        grid=(b, t // tm),
        in_specs=[xspec, _mod_spec(mod), _full((1, D_MODEL)), _full((1, D_MODEL)), _RESIDENT, _RESIDENT],
        out_specs=xspec,
        out_shape=jax.ShapeDtypeStruct(x.shape, F32),
        compiler_params=_params(("arbitrary", "arbitrary")),
        name="ffn",
    )(x, mod, npre, npost, up, down)


def _inproj_call(x, mod, npre, w):
    b, t, _ = x.shape
    tm = _row_tile(t, 256)
    widths = (A_COLS, 3 * D_MODEL, D_MODEL, 2 * D_MODEL, AB_COLS)
    dtypes = (F32, F32, BF16, BF16, F32)
    spec = lambda n: pl.BlockSpec((1, tm, n), lambda b, i: (b, i, 0))
    return pl.pallas_call(
        _inproj_kernel,
        grid=(b, t // tm),
        in_specs=[spec(D_MODEL), _mod_spec(mod), _full((1, D_MODEL)), _RESIDENT],
        out_specs=[spec(n) for n in widths],
        out_shape=[jax.ShapeDtypeStruct((b, t, n), d) for n, d in zip(widths, dtypes)],
        compiler_params=_params(("arbitrary", "arbitrary")),
        name="in_proj",
    )(x, mod, npre, w)


def _outproj_call(x, ya, yb, pg, mod, npost, w):
    b, t, _ = x.shape
    tm = _row_tile(t, 512)
    spec = lambda n: pl.BlockSpec((1, tm, n), lambda b, i: (b, i, 0))
    return pl.pallas_call(
        _outproj_kernel,
        grid=(b, t // tm),
        in_specs=[spec(D_MODEL), spec(D_MODEL), spec(D_MODEL), spec(2 * D_MODEL), _mod_spec(mod),
                  _full((1, D_MODEL)), _RESIDENT],
        out_specs=spec(D_MODEL),
        out_shape=jax.ShapeDtypeStruct(x.shape, F32),
        compiler_params=_params(("arbitrary", "arbitrary")),
        name="out_proj",
    )(x, ya, yb, pg, mod, npost, w)


def _rwkv_prompt_call(pa, rp):
    b, t, _ = pa.shape
    C = RW_C
    R = RW_ROWS
    assert t % C == 0 and b % R == 0
    vec = _full((1, D_MODEL))
    lora = _full((2, LANES, D_MODEL))
    tile3 = _full((N_PAIR * R, 1, LANES))
    per_tile = lambda x: jnp.repeat(x, R, axis=0)
    return pl.pallas_call(
        _rwkv_prompt_kernel,
        grid=(b // R, t // C),
        in_specs=[pl.BlockSpec((R, C, A_COLS), lambda b, i: (b, i, 0)), _full((1, A_COLS)),
                  vec, lora, vec, lora, lora, vec, vec, tile3, tile3, tile3],
        out_specs=[pl.BlockSpec((R, C, D_MODEL), lambda b, i: (b, i, 0)),
                   pl.BlockSpec((R, H_A, HD_A, HD_A), lambda b, i: (b, 0, 0, 0)),
                   pl.BlockSpec((R, 1, A_COLS), lambda b, i: (b, 0, 0))],
        out_shape=[jax.ShapeDtypeStruct((b, t, D_MODEL), BF16),
                   jax.ShapeDtypeStruct((b, H_A, HD_A, HD_A), F32),
                   jax.ShapeDtypeStruct((b, 1, A_COLS), F32)],
        scratch_shapes=[pltpu.VMEM((R, A_COLS), F32), pltpu.VMEM((N_PAIR * R, LANES, LANES), F32)],
        compiler_params=_params(("arbitrary", "arbitrary")),
        name="rwkv_prompt",
    )(pa, rp["mu"], rp["w0"], rp["w2p"], rp["a0"], rp["a2p"], rp["g2"], rp["k_k"], rp["k_a"],
      per_tile(rp["rk3"]), per_tile(rp["lnw3"]), per_tile(rp["lnb3"]))


def _gdn_prompt_call(qkv, ab, z, gp):
    b, t, _ = qkv.shape
    C = GD_C
    assert t % C == 0
    spec = lambda n: pl.BlockSpec((1, C, n), lambda b, i: (b, i, 0))
    return pl.pallas_call(
        _gdn_prompt_kernel,
        grid=(b, t // C),
        in_specs=[spec(3 * D_MODEL), spec(AB_COLS), spec(D_MODEL), _full((CONV_W, 3 * D_MODEL)),
                  _full((1, LANES)), _full((1, LANES)), _full((1, HD_B))],
        out_specs=[spec(D_MODEL),
                   pl.BlockSpec((1, CONV_W - 1, 3 * D_MODEL), lambda b, i: (b, 0, 0)),
                   pl.BlockSpec((1, H_B, HD_B, HD_B), lambda b, i: (b, 0, 0, 0))],
        out_shape=[jax.ShapeDtypeStruct((b, t, D_MODEL), BF16),
                   jax.ShapeDtypeStruct((b, CONV_W - 1, 3 * D_MODEL), F32),
                   jax.ShapeDtypeStruct((b, H_B, HD_B, HD_B), F32)],
        scratch_shapes=[pltpu.VMEM((8 + C, 3 * D_MODEL), F32)],
        compiler_params=_params(("arbitrary", "arbitrary")),
        name="gdn_prompt",
    )(qkv, ab, z, gp["conv_w"], gp["alog"], gp["dtb"], gp["norm_w"])


def _rwkv_sample(pa, shift0, wkv_all, wkv_new, l, rp):
    n = pa.shape[0]
    vec = _full((1, D_MODEL))
    lora = _full((2, LANES, D_MODEL))
    chan = _full((D_MODEL, n))
    r, w, km, v, kkn, bb_, g = pl.pallas_call(
        _rwkv_sample_prep_kernel,
        grid=(1,),
        in_specs=[_full((n, A_COLS)), _full((n, A_COLS)), _full((1, A_COLS)), vec, lora, vec, lora, lora,
                  vec, vec],
        out_specs=[chan] * 7,
        out_shape=[jax.ShapeDtypeStruct((D_MODEL, n), F32)] * 7,
        compiler_params=_params(("arbitrary",)),
        name="rwkv_sample_prep",
    )(pa, shift0, rp["mu"], rp["w0"], rp["w2p"], rp["a0"], rp["a2p"], rp["g2"], rp["k_k"], rp["k_a"])

    hspec = pl.BlockSpec((HD_A, n), lambda h: (h, 0))
    sspec = pl.BlockSpec((None, None, HD_A, HD_A, n), lambda h: (l, h, 0, 0, 0))
    s_new, y = pl.pallas_call(
        _rwkv_sample_state_kernel,
        grid=(H_A,),
        in_specs=[hspec] * 6 + [sspec, pl.BlockSpec(memory_space=pl.ANY)],
        out_specs=[sspec, hspec],
        out_shape=[jax.ShapeDtypeStruct(wkv_all.shape, F32), jax.ShapeDtypeStruct((D_MODEL, n), F32)],
        input_output_aliases={7: 0},
        compiler_params=_params(("arbitrary",)),
        name="rwkv_sample_state",
    )(r, w, km, kkn, bb_, v, wkv_all, wkv_new)

    col = _full((D_MODEL, 1))
    ya = pl.pallas_call(
        _rwkv_sample_post_kernel,
        grid=(1,),
        in_specs=[chan] * 5 + [col] * 3,
        out_specs=_full((n, D_MODEL)),
        out_shape=jax.ShapeDtypeStruct((n, D_MODEL), F32),
        compiler_params=_params(("arbitrary",)),
        name="rwkv_sample_post",
    )(y, r, km, v, g, rp["rk_col"], rp["lnw_col"], rp["lnb_col"])
    return ya, s_new


def _gdn_sample(qkv, conv0, ssm_all, ssm_new, l, ab, z, gp):
    n = qkv.shape[0]
    flat = _full((n, D_MODEL))
    cspec = _full((CONV_W - 1, n, 3 * D_MODEL))
    q, k, v, eg, beta, cnew = pl.pallas_call(
        _gdn_sample_prep_kernel,
        grid=(1,),
        in_specs=[_full((n, 3 * D_MODEL)), cspec, _full((n, AB_COLS)), _full((CONV_W, 3 * D_MODEL)),
                  _full((1, LANES)), _full((1, LANES))],
        out_specs=[flat] * 5 + [cspec],
        out_shape=[jax.ShapeDtypeStruct((n, D_MODEL), F32)] * 5
        + [jax.ShapeDtypeStruct((CONV_W - 1, n, 3 * D_MODEL), F32)],
        compiler_params=_params(("arbitrary",)),
        name="gdn_sample_prep",
    )(qkv, jnp.swapaxes(conv0, 0, 1), ab, gp["conv_w"], gp["alog"], gp["dtb"])

    bb = 4
    assert n % bb == 0
    heads = lambda x: x.reshape(n, H_B, HD_B)
    hspec = pl.BlockSpec((bb, H_B, HD_B), lambda i: (i, 0, 0))
    tspec = pl.BlockSpec((bb, HD_B, H_B), lambda i: (i, 0, 0))
    sspec = pl.BlockSpec((None, bb, H_B, HD_B, HD_B), lambda i: (l, i, 0, 0, 0))
    s_new, o = pl.pallas_call(
        functools.partial(_gdn_sample_state_kernel, bb=bb),
        grid=(n // bb,),
        in_specs=[tspec, tspec, hspec, hspec, hspec, sspec, pl.BlockSpec(memory_space=pl.ANY)],
        out_specs=[sspec, hspec],
        out_shape=[jax.ShapeDtypeStruct(ssm_all.shape, F32), jax.ShapeDtypeStruct((n, H_B, HD_B), F32)],
        input_output_aliases={6: 0},
        compiler_params=_params(("arbitrary",)),
        name="gdn_sample_state",
    )(jnp.swapaxes(heads(q), 1, 2), jnp.swapaxes(heads(k), 1, 2), heads(v), heads(eg), heads(beta), ssm_all,
      ssm_new)

    yb = pl.pallas_call(
        _gdn_sample_post_kernel,
        grid=(1,),
        in_specs=[flat, flat, _full((1, HD_B))],
        out_specs=flat,
        out_shape=jax.ShapeDtypeStruct((n, D_MODEL), F32),
        compiler_params=_params(("arbitrary",)),
        name="gdn_sample_post",
    )(o.reshape(n, D_MODEL), z, gp["norm_w"])
    return yb, jnp.swapaxes(cnew, 0, 1), s_new


def _hi_lo(w):
    hi = w.astype(BF16)
    return jnp.stack([hi, (w - hi.astype(F32)).astype(BF16)])


def _pad_rows(x, rows, at):
    out = jnp.zeros((rows, x.shape[1]), x.dtype)
    return lax.dynamic_update_slice(out, x, (at, 0))


def kernel(x_prompt, x_sample, c_prompt, c_sample, state_rwkv_shift, state_rwkv_wkv, state_gdn_conv, state_gdn_ssm, w_ada, b_ada, norm_pre, norm_post, ffn_up, ffn_down, w_in, w_out, rwkv_mu, rwkv_w0, rwkv_w2, rwkv_a0, rwkv_a2, rwkv_g2, rwkv_k_k, rwkv_k_a, rwkv_r_k, rwkv_ln_w, rwkv_ln_b, gdn_conv, gdn_a_log, gdn_dt_bias, gdn_norm_w):
    depth = w_ada.shape[0]
    bp = x_prompt.shape[0]
    ns = x_sample.shape[0]

    n_c = bp + ns
    n_cp = -(-n_c // 16) * 16
    c_all = jnp.concatenate([c_prompt, c_sample, jnp.zeros((n_cp - n_c, D_MODEL), F32)], axis=0)
    mod = _ada_call(c_all, w_ada, b_ada)

    o_b = A_COLS
    o_ab = o_b + 3 * D_MODEL
    o_z = o_ab + 2 * H_B
    o_g = o_z + D_MODEL
    def w_in_layer(l):
        w = w_in[l]
        return jnp.concatenate(
            [w[:, 0:o_b], w[:, o_b:o_ab], w[:, o_z:o_g], w[:, o_g:], w[:, o_ab:o_z],
             jnp.zeros((D_MODEL, AB_COLS - 2 * H_B), F32)], axis=-1).astype(BF16)

    wkv_t = jnp.transpose(state_rwkv_wkv, (0, 2, 3, 4, 1))

    yp = x_prompt
    ys = x_sample.reshape(1, ns, D_MODEL)
    outs = [[] for _ in range(6)]
    s_wkv = jnp.zeros(wkv_t.shape, F32)
    s_ssm = jnp.zeros(state_gdn_ssm.shape, F32)
    for l in range(depth):
        w_in_r = w_in_layer(l)
        up_b = [ffn_up[l, j].astype(BF16) for j in range(2)]
        down_b = [ffn_down[l, j].astype(BF16) for j in range(2)]
        w_out_b = w_out[l].astype(BF16)
        mod_p = mod[l, :, 0:bp].reshape(3 * N_SUB, bp, 1, D_MODEL)
        mod_s = mod[l, :, bp:bp + ns].reshape(3 * N_SUB, 1, ns, D_MODEL)
        npre = [norm_pre[l, i][None] for i in range(N_SUB)]
        npost = [norm_post[l, i][None] for i in range(N_SUB)]
        row = lambda x: x[l][None]
        tile3 = lambda x: x[l].reshape(N_PAIR, 1, LANES)
        rp = dict(mu=row(rwkv_mu), w0=row(rwkv_w0), a0=row(rwkv_a0), k_k=row(rwkv_k_k), k_a=row(rwkv_k_a),
                  w2p=_hi_lo(_pad_rows(rwkv_w2[l], LANES, 0)), a2p=_hi_lo(_pad_rows(rwkv_a2[l], LANES, W_LORA)),
                  g2=_hi_lo(rwkv_g2[l]), rk3=tile3(rwkv_r_k), lnw3=tile3(rwkv_ln_w), lnb3=tile3(rwkv_ln_b),
                  rk_col=rwkv_r_k[l].reshape(D_MODEL, 1), lnw_col=rwkv_ln_w[l].reshape(D_MODEL, 1),
                  lnb_col=rwkv_ln_b[l].reshape(D_MODEL, 1))
        lane_row = lambda x: jnp.zeros((1, LANES), F32).at[0, 0:H_B].set(x[l])
        gp = dict(conv_w=gdn_conv[l], norm_w=row(gdn_norm_w), alog=lane_row(gdn_a_log),
                  dtb=lane_row(gdn_dt_bias))

        yp = _ffn_call(yp, mod_p, npre[0], npost[0], up_b[0], down_b[0], 0)
        pa, qkv, z, pg, ab = _inproj_call(yp, mod_p, npre[1], w_in_r)
        ya, p_wkv, p_shift = _rwkv_prompt_call(pa, rp)
        yb, p_conv, p_ssm = _gdn_prompt_call(qkv, ab, z, gp)
        yp = _outproj_call(yp, ya, yb, pg, mod_p, npost[1], w_out_b)
        yp = _ffn_call(yp, mod_p, npre[2], npost[2], up_b[1], down_b[1], 2)

        ys = _ffn_call(ys, mod_s, npre[0], npost[0], up_b[0], down_b[0], 0)
        pa, qkv, z, pg, ab = _inproj_call(ys, mod_s, npre[1], w_in_r)
        ya, s_wkv = _rwkv_sample(pa[0], state_rwkv_shift[l], wkv_t, s_wkv, l, rp)
        yb, s_conv, s_ssm = _gdn_sample(qkv[0], state_gdn_conv[l], state_gdn_ssm, s_ssm, l, ab[0], z[0], gp)
        ys = _outproj_call(ys, ya[None], yb[None], pg, mod_s, npost[1], w_out_b)
        ys = _ffn_call(ys, mod_s, npre[2], npost[2], up_b[1], down_b[1], 2)

        for lst, val in zip(outs, (p_shift[:, 0], p_wkv, p_conv, p_ssm, pa[0], s_conv)):
            lst.append(val)

    st = [jnp.stack(o) for o in outs]
    return (yp, ys.reshape(ns, 1, D_MODEL), st[0], st[1], st[2], st[3], st[4],
            jnp.transpose(s_wkv, (0, 4, 1, 2, 3)), st[5], s_ssm)
```

```python
import functools

import jax
import jax.numpy as jnp
from jax import lax
from jax.experimental import pallas as pl
from jax.experimental.pallas import tpu as pltpu

F32 = jnp.float32
BF16 = jnp.bfloat16

D_MODEL = 1024
HD_A = 64
H_A = D_MODEL // HD_A
W_LORA = 64
A_LORA = 64
G_LORA = 128
A_COLS = 3 * D_MODEL + W_LORA + A_LORA + G_LORA
RWKV_GN_EPS = 6.4e-4
HD_B = 128
H_B = D_MODEL // HD_B
CONV_W = 4
D_FF = 2816
N_SUB = 3
FFN_RES = 0.5
RMS_EPS = 1e-6

LANES = 128
N_PAIR = D_MODEL // LANES
RW_C = 64
RW_ROWS = 2
GD_C = 128
AB_COLS = LANES
IN_COLS_R = A_COLS + 3 * D_MODEL + D_MODEL + 2 * D_MODEL + AB_COLS
VMEM_LIMIT = 56 * 1024 * 1024
REC_PASSES = 1

_NN = (((1,), (0,)), ((), ()))
_NT = (((1,), (1,)), ((), ()))


def _dg(a, b, dn=_NN):
    return lax.dot_general(a, b, dn, preferred_element_type=F32)


def _split2(x):
    hi = x.astype(BF16)
    lo = (x - hi.astype(F32)).astype(BF16)
    return hi, lo


def _split3(x):
    h1 = x.astype(BF16)
    r1 = x - h1.astype(F32)
    h2 = r1.astype(BF16)
    h3 = (r1 - h2.astype(F32)).astype(BF16)
    return h1, h2, h3


def _mm(a, b, dn=_NN, passes=REC_PASSES):
    if passes == 1:
        return _dg(a.astype(BF16), b.astype(BF16), dn)
    ah, al = _split2(a)
    bh, bl = _split2(b)
    return _dg(ah, bh, dn) + (_dg(ah, bl, dn) + _dg(al, bh, dn))


def _mm_sel_rhs(x, e, pieces):
    parts = _split3(x) if pieces == 3 else _split2(x)
    out = _dg(parts[0], e)
    for p in parts[1:]:
        out = out + _dg(p, e)
    return out


def _mm_sel_lhs(e, x, pieces):
    parts = _split3(x) if pieces == 3 else _split2(x)
    out = _dg(e, parts[0])
    for p in parts[1:]:
        out = out + _dg(e, p)
    return out


def _iota2(shape, dim):
    return lax.broadcasted_iota(jnp.int32, shape, dim)


def _softplus(x):
    return jnp.maximum(x, 0.0) + jnp.log(1.0 + jnp.exp(-jnp.abs(x)))


def _silu(x):
    return x * jax.nn.sigmoid(x)


def _rms(x, w):
    return x * lax.rsqrt(jnp.mean(x * x, axis=-1, keepdims=True) + RMS_EPS) * w


def _mod_in(x, npre, mod_ref, sub):
    shift = mod_ref[3 * sub, 0]
    scale = mod_ref[3 * sub + 1, 0]
    return _rms(x, npre) * (1.0 + scale) + shift


def _tri_masks(n, top):
    i = _iota2((n, n), 0)
    j = _iota2((n, n), 1)
    masks = []
    s = 1
    while s <= top:
        sh = s.bit_length() - 1
        same = (i >> (sh + 1)) == (j >> (sh + 1))
        masks.append(same & (((i >> sh) & 1) == 1) & (((j >> sh) & 1) == 0))
        s *= 2
    return masks


def _tri_inv(Ls, masks):
    n = Ls[0].shape[0]
    eye = jnp.where(_iota2((n, n), 0) == _iota2((n, n), 1), 1.0, 0.0).astype(F32)
    Xs = [eye - jnp.where(masks[0], L, 0.0) for L in Ls]
    Lb = [L.astype(BF16) for L in Ls]
    zero = jnp.zeros((), BF16)
    for m in masks[1:]:
        Xb = [X.astype(BF16) for X in Xs]
        Ts = [_dg(xb, jnp.where(m, lb, zero)) for xb, lb in zip(Xb, Lb)]
        Xs = [X - _dg(T.astype(BF16), xb) for X, T, xb in zip(Xs, Ts, Xb)]
    return Xs


def _to_tiles(x):
    return jnp.stack([x[:, LANES * p:LANES * (p + 1)] for p in range(N_PAIR)], axis=0)


def _pair_headsum(x3, ebd):
    p, n, l = x3.shape
    return _mm_sel_rhs(x3.reshape(p * n, l), ebd, 2).reshape(p, n, l)


def _ebd():
    i = _iota2((LANES, LANES), 0)
    j = _iota2((LANES, LANES), 1)
    return jnp.where((i >> 6) == (j >> 6), 1.0, 0.0).astype(BF16)


def _mm_lora(a, w_ref):
    ah, al = _split2(a)
    return _dg(ah, w_ref[0]) + (_dg(ah, w_ref[1]) + _dg(al, w_ref[0]))


def _rwkv_prep(pa, prev, mu, w0, w2p, a0, a2p, g2, k_k, k_a):
    xs = pa + mu * (prev - pa)
    r = xs[:, 0:D_MODEL]
    k = xs[:, D_MODEL:2 * D_MODEL]
    v = xs[:, 2 * D_MODEL:3 * D_MODEL]
    wa = xs[:, 3 * D_MODEL:3 * D_MODEL + LANES]
    gd = xs[:, 3 * D_MODEL + LANES:A_COLS]
    w_lin = _mm_lora(jnp.tanh(wa), w2p)
    a_lin = _mm_lora(wa, a2p)
    g = _mm_lora(jax.nn.sigmoid(gd), g2)
    w_log = -_softplus(-(w0 + w_lin)) - 0.5
    lw = -jnp.exp(w_log)
    a = jax.nn.sigmoid(a0 + a_lin)
    kkr = k * k_k
    km = k * (1.0 + (a - 1.0) * k_a)
    return r, km, v, kkr, a, lw, g


def _rwkv_epilogue(y3, r3, km3, v3, rk3, lnw3, lnb3, ebd):
    mean = _pair_headsum(y3, ebd) * (1.0 / HD_A)
    yc = y3 - mean
    var = _pair_headsum(yc * yc, ebd) * (1.0 / HD_A)
    yn = yc * lax.rsqrt(var + RWKV_GN_EPS) * lnw3 + lnb3
    bonus = _pair_headsum(r3 * km3 * rk3, ebd) * v3
    return yn + bonus


def _gdn_gates(ab, alog, dtb):
    g_all = -jnp.exp(alog) * _softplus(ab + dtb)
    beta_all = jax.nn.sigmoid(ab)
    return g_all, beta_all


def _sel_matrix(offset):
    i = _iota2((LANES, D_MODEL), 0)
    j = _iota2((LANES, D_MODEL), 1)
    return jnp.where(i == (j >> 7) + offset, 1.0, 0.0).astype(BF16)


def _head_l2norm(x, eps):
    outs = []
    for h in range(H_B):
        xh = x[:, LANES * h:LANES * (h + 1)]
        outs.append(xh * lax.rsqrt(jnp.sum(xh * xh, axis=-1, keepdims=True) + eps))
    return outs


def _ada_kernel(c_ref, w_ref, b_ref, o_ref):
    s = _silu(c_ref[...]).astype(BF16)
    o_ref[...] = _dg(s, w_ref[...].astype(BF16)) + b_ref[...]


def _ffn_kernel(x_ref, mod_ref, npre_ref, npost_ref, up_ref, down_ref, o_ref, *, sub):
    x = x_ref[0]
    h = _mod_in(x, npre_ref[...], mod_ref, sub).astype(BF16)
    gate = _dg(h, up_ref[:, 0:D_FF])
    val = _dg(h, up_ref[:, D_FF:2 * D_FF])
    act = (_silu(gate) * val).astype(BF16)
    y = _dg(act, down_ref[...])
    o_ref[0] = x + FFN_RES * mod_ref[3 * sub + 2, 0] * _rms(y, npost_ref[...])


def _inproj_kernel(x_ref, mod_ref, npre_ref, w_ref, pa_ref, qkv_ref, z_ref, pg_ref, ab_ref):
    h = _mod_in(x_ref[0], npre_ref[...], mod_ref, 1).astype(BF16)
    o = 0
    for ref in (pa_ref, qkv_ref, z_ref, pg_ref, ab_ref):
        n = ref.shape[-1]
        ref[0] = _dg(h, w_ref[:, o:o + n]).astype(ref.dtype)
        o += n


def _causal_conv_silu(x, hist_ref, cw_ref):
    n = x.shape[0]
    hist_ref[8:8 + n, :] = x
    conv = x * cw_ref[CONV_W - 1:CONV_W, :]
    for s in range(1, CONV_W):
        conv = conv + hist_ref[8 - s:8 - s + n, :] * cw_ref[CONV_W - 1 - s:CONV_W - s, :]
    hist_ref[0:8, :] = x[n - 8:n]
    return _silu(conv)


def _outproj_kernel(x_ref, ya_ref, yb_ref, pg_ref, mod_ref, npost_ref, w_ref, o_ref):
    pg = pg_ref[0].astype(F32)
    ya = ya_ref[0].astype(F32)
    yb = yb_ref[0].astype(F32)
    m = jax.nn.sigmoid(pg[:, 0:D_MODEL]) * ya + jax.nn.sigmoid(pg[:, D_MODEL:]) * yb
    y = _dg(m.astype(BF16), w_ref[...])
    o_ref[0] = x_ref[0] + mod_ref[3 * 1 + 2, 0] * _rms(y, npost_ref[...])


def _rwkv_prompt_kernel(pa_ref, mu_ref, w0_ref, w2_ref, a0_ref, a2_ref, g2_ref, kk_ref, ka_ref,
                        rk3_ref, lnw3_ref, lnb3_ref, ya_ref, s_ref, shift_ref, carry_ref, sbd_ref):
    C = RW_C
    R = RW_ROWS
    N = R * C
    t = pl.program_id(1)

    @pl.when(t == 0)
    def _():
        carry_ref[...] = jnp.zeros_like(carry_ref)
        sbd_ref[...] = jnp.zeros_like(sbd_ref)

    pa = pa_ref[...].reshape(N, A_COLS)
    last = jnp.concatenate([jnp.broadcast_to(carry_ref[s:s + 1, :], (C, A_COLS)) for s in range(R)], axis=0)
    prev = jnp.where((_iota2((N, 1), 0) & (C - 1)) == 0, last, pltpu.roll(pa, 1, 0))
    for s in range(R):
        carry_ref[s:s + 1, :] = pa[s * C + C - 1:s * C + C]
    r, km, v, kkr, a, lw, g = _rwkv_prep(pa, prev, mu_ref[...], w0_ref[...], w2_ref, a0_ref[...],
                                         a2_ref, g2_ref, kk_ref[...], ka_ref[...])

    ti = _iota2((N, N), 0)
    tj = _iota2((N, N), 1)
    csh = C.bit_length() - 1
    tri = jnp.where((ti >= tj) & ((ti >> csh) == (tj >> csh)), 1.0, 0.0).astype(BF16)
    cum = _mm_sel_lhs(tri, lw, 3)
    ebd = _ebd()
    NQ = N_PAIR * R
    tiles = lambda x: _to_tiles(x).reshape(NQ, C, LANES)
    r3, km3, v3, kkr3, a3, lw3, cum3 = (tiles(x) for x in (r, km, v, kkr, a, lw, cum))
    kk3 = kkr3 * lax.rsqrt(_pair_headsum(kkr3 * kkr3, ebd) + 1e-12)
    b3 = kk3 * a3
    cl = cum3[:, C - 1:C, :]
    e_neg = jnp.exp(-cum3)
    e_end = jnp.exp(cl - cum3)
    rh3 = r3 * jnp.exp(cum3)
    ah3 = kk3 * jnp.exp(cum3 - lw3)
    bt3 = b3 * e_neg
    kt3 = km3 * e_neg
    be3 = b3 * e_end
    ke3 = km3 * e_end
    e_last3 = jnp.exp(cl)

    lane0 = _iota2((1, LANES), 1) < HD_A

    def sm(x):
        return jnp.concatenate([jnp.where(lane0, x, 0.0), jnp.where(lane0, 0.0, x)], axis=0)

    ii = _iota2((2 * C, 2 * C), 0)
    jj = _iota2((2 * C, 2 * C), 1)
    strict = jj < ii
    incl = jj <= ii
    masks = _tri_masks(2 * C, C // 2)

    P = range(NQ)
    T2 = 2 * C
    bsm = lambda x: sm(x).astype(BF16)
    ARs = [jnp.concatenate([bsm(ah3[p]), bsm(rh3[p])], axis=0) for p in P]
    BKs = [jnp.concatenate([bsm(bt3[p]), bsm(kt3[p])], axis=0) for p in P]
    BKe = [jnp.concatenate([bsm(be3[p]), bsm(ke3[p])], axis=0) for p in P]
    Vs = [sm(v3[p]) for p in P]
    Vb = [V.astype(BF16) for V in Vs]
    Ss = [sbd_ref[p] for p in P]
    QQ = [_dg(ARs[p], BKs[p], _NT) for p in P]
    QS = [_dg(ARs[p], Ss[p].astype(BF16), _NT) for p in P]
    Lab = [jnp.where(strict, Q[0:T2, 0:T2], 0.0) for Q in QQ]
    Lak = [jnp.where(strict, Q[0:T2, T2:2 * T2], 0.0).astype(BF16) for Q in QQ]
    Ms = [QS[p][0:T2] + _dg(Lak[p], Vb[p]) for p in P]
    Tinv = _tri_inv(Lab, masks)
    Us = [-_dg(Tinv[p].astype(BF16), Ms[p].astype(BF16)) for p in P]
    UVb = [jnp.concatenate([Us[p].astype(BF16), Vb[p]], axis=0) for p in P]
    incl2 = jnp.concatenate([incl, incl], axis=1)
    Arbk = [jnp.where(incl2, Q[T2:2 * T2, :], 0.0).astype(BF16) for Q in QQ]
    Ys = [QS[p][T2:2 * T2] + _dg(Arbk[p], UVb[p]) for p in P]
    for p in P:
        UVt = jnp.concatenate([Us[p].T, Vs[p].T], axis=1).astype(BF16)
        sbd_ref[p] = Ss[p] * e_last3[p] + _dg(UVt, BKe[p])
    y3 = jnp.stack([Y[0:C] + Y[C:2 * C] for Y in Ys], axis=0)

    out3 = _rwkv_epilogue(y3, r3, km3, v3, rk3_ref[...], lnw3_ref[...], lnb3_ref[...], ebd)
    for q in P:
        p, s = divmod(q, R)
        ya_ref[s, :, LANES * p:LANES * (p + 1)] = (
            out3[q] * g[s * C:(s + 1) * C, LANES * p:LANES * (p + 1)]).astype(ya_ref.dtype)

    @pl.when(t == pl.num_programs(1) - 1)
    def _():
        for s in range(R):
            shift_ref[s] = pa[s * C + C - 1:s * C + C]
        for q in P:
            p, s = divmod(q, R)
            S = sbd_ref[q]
            s_ref[s, 2 * p] = S[0:HD_A, 0:HD_A]
            s_ref[s, 2 * p + 1] = pltpu.roll(S, HD_A, 1)[HD_A:LANES, 0:HD_A]


def _gdn_prompt_kernel(qkv_ref, ab_ref, z_ref, cw_ref, alog_ref, dtb_ref, nw_ref,
                       yb_ref, conv_ref, ssm_ref, hist_ref):
    C = GD_C
    t = pl.program_id(1)

    @pl.when(t == 0)
    def _():
        hist_ref[0:8, :] = jnp.zeros((8, 3 * D_MODEL), F32)
        ssm_ref[...] = jnp.zeros_like(ssm_ref)

    x = qkv_ref[0]
    qkvc = _causal_conv_silu(x, hist_ref, cw_ref)
    qs = _head_l2norm(qkvc[:, 0:D_MODEL], 1e-6)
    ks = _head_l2norm(qkvc[:, D_MODEL:2 * D_MODEL], 1e-6)
    vv = qkvc[:, 2 * D_MODEL:3 * D_MODEL]

    g_all, beta_all = _gdn_gates(ab_ref[0], alog_ref[...], dtb_ref[...])
    tri = jnp.where(_iota2((C, C), 0) >= _iota2((C, C), 1), 1.0, 0.0).astype(BF16)
    gc_all = _mm_sel_lhs(tri, g_all, 3)

    ii = _iota2((C, C), 0)
    jj = _iota2((C, C), 1)
    strict = jj < ii
    incl = jj <= ii
    masks = _tri_masks(C, C // 2)
    z = z_ref[0].astype(F32)

    H = range(H_B)
    sls = [slice(LANES * h, LANES * (h + 1)) for h in H]
    qh = [qs[h] * (HD_B ** -0.5) for h in H]
    bcast = lambda col: jnp.broadcast_to(col, (C, LANES))
    gh = [bcast(gc_all[:, h:h + 1]) for h in H]
    bh = [bcast(beta_all[:, H_B + h:H_B + h + 1]) for h in H]
    gct = gc_all.T
    dec = [jnp.exp(gh[h] - gct[h:h + 1, :]) for h in H]
    eg = [jnp.exp(g) for g in gh]
    glast = [g[C - 1:C, :] for g in gh]
    kb = [ks[h] * bh[h] for h in H]
    KQ = [_dg(jnp.concatenate([kb[h], qh[h]], axis=0).astype(BF16), ks[h].astype(BF16), _NT) for h in H]
    Ls = [jnp.where(strict, KQ[h][0:C] * dec[h], 0.0) for h in H]
    attn = [jnp.where(incl, KQ[h][C:2 * C] * dec[h], 0.0) for h in H]
    Tinv = _tri_inv(Ls, masks)
    rhs = [jnp.concatenate([vv[:, sls[h]] * bh[h], kb[h] * eg[h]], axis=1).astype(BF16) for h in H]
    UW = [_dg(Tinv[h].astype(BF16), rhs[h]) for h in H]
    Ss = [ssm_ref[0, h] for h in H]
    Sb = [S.astype(BF16) for S in Ss]
    v_new = [UW[h][:, 0:LANES] - _dg(UW[h][:, LANES:2 * LANES].astype(BF16), Sb[h]) for h in H]
    vnb = [v.astype(BF16) for v in v_new]
    os_ = [_dg(jnp.concatenate([qh[h] * eg[h], attn[h]], axis=1).astype(BF16),
               jnp.concatenate([Sb[h], vnb[h]], axis=0)) for h in H]
    for h in H:
        k_dec = ks[h] * jnp.exp(glast[h] - gh[h])
        ssm_ref[0, h] = Ss[h] * jnp.exp(glast[h]) + _dg(k_dec.T.astype(BF16), vnb[h])
    for h in H:
        o = os_[h]
        on = o * lax.rsqrt(jnp.mean(o * o, axis=-1, keepdims=True) + RMS_EPS) * nw_ref[...]
        yb_ref[0, :, sls[h]] = (on * _silu(z[:, sls[h]])).astype(yb_ref.dtype)

    @pl.when(t == pl.num_programs(1) - 1)
    def _():
        conv_ref[0] = x[C - (CONV_W - 1):C]


def _rwkv_sample_prep_kernel(pa_ref, prev_ref, mu_ref, w0_ref, w2_ref, a0_ref, a2_ref, g2_ref, kk_ref,
                             ka_ref, r_ref, w_ref, km_ref, v_ref, kkn_ref, b_ref, g_ref):
    r, km, v, kkr, a, lw, g = _rwkv_prep(pa_ref[...], prev_ref[...], mu_ref[...], w0_ref[...], w2_ref,
                                         a0_ref[...], a2_ref, g2_ref, kk_ref[...], ka_ref[...])
    n = r.shape[0]
    heads = lambda x: x.reshape(H_A, HD_A, n)
    kk3 = heads(kkr.T)
    kkn = kk3 * lax.rsqrt(jnp.sum(kk3 * kk3, axis=1, keepdims=True) + 1e-12)
    r_ref[...] = r.T
    w_ref[...] = jnp.exp(lw).T
    km_ref[...] = km.T
    v_ref[...] = v.T
    kkn_ref[...] = kkn.reshape(D_MODEL, n)
    b_ref[...] = (kkn * heads(a.T)).reshape(D_MODEL, n)
    g_ref[...] = g.T


def _rwkv_sample_state_kernel(r_ref, w_ref, km_ref, kkn_ref, b_ref, v_ref, s0_ref, stacked_ref, s_ref, y_ref):
    del stacked_ref
    rh, wh, kh, kkh, bh = r_ref[...], w_ref[...], km_ref[...], kkn_ref[...], b_ref[...]
    G = 8
    row = _iota2((G, 1), 0)

    def body(j, carry):
        base = pl.multiple_of(j * G, G)
        vrows = v_ref[pl.ds(base, G), :]
        U = range(G)
        Ss = [s0_ref[base + u] for u in U]
        sas = [-jnp.sum(Ss[u] * kkh, axis=0, keepdims=True) for u in U]
        Sn = [Ss[u] * wh + sas[u] * bh + vrows[u:u + 1, :] * kh for u in U]
        for u in U:
            s_ref[base + u] = Sn[u]
        yrows = [jnp.sum(Sn[u] * rh, axis=0, keepdims=True) for u in U]
        y = jnp.zeros_like(vrows)
        for u in U:
            y = jnp.where(row == u, yrows[u], y)
        y_ref[pl.ds(base, G), :] = y
        return carry

    lax.fori_loop(0, HD_A // G, body, 0)


def _rwkv_sample_post_kernel(y_ref, r_ref, km_ref, v_ref, g_ref, rk_ref, lnw_ref, lnb_ref, ya_ref):
    n = y_ref.shape[1]
    heads = lambda x: x.reshape(H_A, HD_A, n)
    y3 = heads(y_ref[...])
    yc = y3 - jnp.mean(y3, axis=1, keepdims=True)
    var = jnp.mean(yc * yc, axis=1, keepdims=True)
    yn = (yc * lax.rsqrt(var + RWKV_GN_EPS)).reshape(D_MODEL, n) * lnw_ref[...] + lnb_ref[...]
    bonus = jnp.sum(heads(r_ref[...] * km_ref[...] * rk_ref[...]), axis=1, keepdims=True) * heads(v_ref[...])
    out = (yn + bonus.reshape(D_MODEL, n)) * g_ref[...]
    ya_ref[...] = out.T


def _gdn_sample_prep_kernel(qkv_ref, c0_ref, ab_ref, cw_ref, alog_ref, dtb_ref,
                            q_ref, k_ref, v_ref, eg_ref, beta_ref, cnew_ref):
    x = qkv_ref[...]
    conv = x * cw_ref[CONV_W - 1:CONV_W, :]
    for j in range(CONV_W - 1):
        conv = conv + c0_ref[j] * cw_ref[j:j + 1, :]
    for j in range(CONV_W - 2):
        cnew_ref[j] = c0_ref[j + 1]
    cnew_ref[CONV_W - 2] = x
    qkvc = _silu(conv)
    qs = _head_l2norm(qkvc[:, 0:D_MODEL], 1e-6)
    ks = _head_l2norm(qkvc[:, D_MODEL:2 * D_MODEL], 1e-6)
    for h in range(H_B):
        q_ref[:, LANES * h:LANES * (h + 1)] = qs[h] * (HD_B ** -0.5)
        k_ref[:, LANES * h:LANES * (h + 1)] = ks[h]
    v_ref[...] = qkvc[:, 2 * D_MODEL:3 * D_MODEL]
    g_all, beta_all = _gdn_gates(ab_ref[...], alog_ref[...], dtb_ref[...])
    eg_ref[...] = jnp.exp(_mm_sel_rhs(g_all, _sel_matrix(0), 3))
    beta_ref[...] = _mm_sel_rhs(beta_all, _sel_matrix(H_B), 3)


def _gdn_sample_state_kernel(qt_ref, kt_ref, v_ref, eg_ref, beta_ref, s0_ref, stacked_ref, s_ref, o_ref, *, bb):
    del stacked_ref

    def body(i, carry):
        qt, kt = qt_ref[i], kt_ref[i]
        vb, egb, btb = v_ref[i], eg_ref[i], beta_ref[i]
        H = range(H_B)
        kc = [jnp.broadcast_to(kt[:, h:h + 1], (HD_B, HD_B)) for h in H]
        Sd = [s0_ref[i, h] * egb[h:h + 1, :] for h in H]
        ksr = [jnp.sum(Sd[h] * kc[h], axis=0, keepdims=True) for h in H]
        v_new = [btb[h:h + 1, :] * (vb[h:h + 1, :] - ksr[h]) for h in H]
        Sn = [Sd[h] + kc[h] * v_new[h] for h in H]
        for h in H:
            s_ref[i, h] = Sn[h]
        for h in H:
            o_ref[i, h:h + 1, :] = jnp.sum(Sn[h] * qt[:, h:h + 1], axis=0, keepdims=True)
        return carry

    lax.fori_loop(0, bb, body, 0, unroll=2)


def _gdn_sample_post_kernel(o_ref, z_ref, nw_ref, yb_ref):
    o = o_ref[...]
    z = z_ref[...].astype(F32)
    for h in range(H_B):
        sl = slice(LANES * h, LANES * (h + 1))
        oh = o[:, sl]
        on = oh * lax.rsqrt(jnp.mean(oh * oh, axis=-1, keepdims=True) + RMS_EPS) * nw_ref[...]
        yb_ref[:, sl] = on * _silu(z[:, sl])


def _params(sem):
    return pltpu.CompilerParams(dimension_semantics=sem, vmem_limit_bytes=VMEM_LIMIT)


def _full(shape):
    return pl.BlockSpec(shape, lambda *_: (0,) * len(shape))


_RESIDENT = pl.BlockSpec(memory_space=pltpu.VMEM)


def _row_tile(t, cap):
    tm = min(t, cap)
    assert t % tm == 0
    return tm


def _mod_spec(mod):
    rows = mod.shape[2]
    if rows == 1:
        return pl.BlockSpec((3 * N_SUB, 1, 1, D_MODEL), lambda b, i: (0, b, 0, 0))
    return pl.BlockSpec((3 * N_SUB, 1, rows, D_MODEL), lambda b, i: (0, 0, i, 0))


def _ada_call(c_all, w_ada, b_ada):
    depth = w_ada.shape[0]
    n = c_all.shape[0]
    return pl.pallas_call(
        _ada_kernel,
        grid=(depth, 3 * N_SUB),
        in_specs=[pl.BlockSpec((n, D_MODEL), lambda l, j: (0, 0)),
                  pl.BlockSpec((None, D_MODEL, D_MODEL), lambda l, j: (l, 0, j)),
                  pl.BlockSpec((None, 1, D_MODEL), lambda l, j: (l, 0, j))],
        out_specs=pl.BlockSpec((None, None, n, D_MODEL), lambda l, j: (l, j, 0, 0)),
        out_shape=jax.ShapeDtypeStruct((depth, 3 * N_SUB, n, D_MODEL), F32),
        compiler_params=_params(("arbitrary", "arbitrary")),
        name="ada_mod",
    )(c_all, w_ada, b_ada.reshape(depth, 1, 3 * N_SUB * D_MODEL))


def _cast_kernel(w_ref, o_ref):
    o_ref[...] = w_ref[...].astype(o_ref.dtype)


def _cast_call(w, idx):
    lead = len(idx)
    k, n = w.shape[lead:]
    rows = k // 4
    assert k % 4 == 0 and rows % 16 == 0
    return pl.pallas_call(
        _cast_kernel,
        grid=(4,),
        in_specs=[pl.BlockSpec((None,) * lead + (rows, n), lambda i: tuple(idx) + (i, 0))],
        out_specs=pl.BlockSpec((rows, n), lambda i: (i, 0)),
        out_shape=jax.ShapeDtypeStruct((k, n), BF16),
        compiler_params=_params(("arbitrary",)),
        name="cast_weight",
    )(w)


def _ffn_call(x, mod, npre, npost, up, down, sub):
    b, t, _ = x.shape
    tm = _row_tile(t, 512)
    xspec = pl.BlockSpec((1, tm, D_MODEL), lambda b, i: (b, i, 0))
    return pl.pallas_call(
        functools.partial(_ffn_kernel, sub=sub),
        grid=(b, t // tm),
        in_specs=[xspec, _mod_spec(mod), _full((1, D_MODEL)), _full((1, D_MODEL)), _RESIDENT, _RESIDENT],
        out_specs=xspec,
        out_shape=jax.ShapeDtypeStruct(x.shape, F32),
        compiler_params=_params(("arbitrary", "arbitrary")),
        name="ffn",
    )(x, mod, npre, npost, up, down)


def _inproj_call(x, mod, npre, w):
    b, t, _ = x.shape
    tm = _row_tile(t, 256)
    widths = (A_COLS, 3 * D_MODEL, D_MODEL, 2 * D_MODEL, AB_COLS)
    dtypes = (F32, F32, BF16, BF16, F32)
    spec = lambda n: pl.BlockSpec((1, tm, n), lambda b, i: (b, i, 0))
    return pl.pallas_call(
        _inproj_kernel,
        grid=(b, t // tm),
        in_specs=[spec(D_MODEL), _mod_spec(mod), _full((1, D_MODEL)), _RESIDENT],
        out_specs=[spec(n) for n in widths],
        out_shape=[jax.ShapeDtypeStruct((b, t, n), d) for n, d in zip(widths, dtypes)],
        compiler_params=_params(("arbitrary", "arbitrary")),
        name="in_proj",
    )(x, mod, npre, w)


def _outproj_call(x, ya, yb, pg, mod, npost, w):
    b, t, _ = x.shape
    tm = _row_tile(t, 512)
    spec = lambda n: pl.BlockSpec((1, tm, n), lambda b, i: (b, i, 0))
    return pl.pallas_call(
        _outproj_kernel,
        grid=(b, t // tm),
        in_specs=[spec(D_MODEL), spec(D_MODEL), spec(D_MODEL), spec(2 * D_MODEL), _mod_spec(mod),
                  _full((1, D_MODEL)), _RESIDENT],
        out_specs=spec(D_MODEL),
        out_shape=jax.ShapeDtypeStruct(x.shape, F32),
        compiler_params=_params(("arbitrary", "arbitrary")),
        name="out_proj",
    )(x, ya, yb, pg, mod, npost, w)


def _rwkv_prompt_call(pa, rp):
    b, t, _ = pa.shape
    C = RW_C
    R = RW_ROWS
    assert t % C == 0 and b % R == 0
    vec = _full((1, D_MODEL))
    lora = _full((2, LANES, D_MODEL))
    tile3 = _full((N_PAIR * R, 1, LANES))
    per_tile = lambda x: jnp.repeat(x, R, axis=0)
    return pl.pallas_call(
        _rwkv_prompt_kernel,
        grid=(b // R, t // C),
        in_specs=[pl.BlockSpec((R, C, A_COLS), lambda b, i: (b, i, 0)), _full((1, A_COLS)),
                  vec, lora, vec, lora, lora, vec, vec, tile3, tile3, tile3],
        out_specs=[pl.BlockSpec((R, C, D_MODEL), lambda b, i: (b, i, 0)),
                   pl.BlockSpec((R, H_A, HD_A, HD_A), lambda b, i: (b, 0, 0, 0)),
                   pl.BlockSpec((R, 1, A_COLS), lambda b, i: (b, 0, 0))],
        out_shape=[jax.ShapeDtypeStruct((b, t, D_MODEL), BF16),
                   jax.ShapeDtypeStruct((b, H_A, HD_A, HD_A), F32),
                   jax.ShapeDtypeStruct((b, 1, A_COLS), F32)],
        scratch_shapes=[pltpu.VMEM((R, A_COLS), F32), pltpu.VMEM((N_PAIR * R, LANES, LANES), F32)],
        compiler_params=_params(("arbitrary", "arbitrary")),
        name="rwkv_prompt",
    )(pa, rp["mu"], rp["w0"], rp["w2p"], rp["a0"], rp["a2p"], rp["g2"], rp["k_k"], rp["k_a"],
      per_tile(rp["rk3"]), per_tile(rp["lnw3"]), per_tile(rp["lnb3"]))


def _gdn_prompt_call(qkv, ab, z, gp):
    b, t, _ = qkv.shape
    C = GD_C
    assert t % C == 0
    spec = lambda n: pl.BlockSpec((1, C, n), lambda b, i: (b, i, 0))
    return pl.pallas_call(
        _gdn_prompt_kernel,
        grid=(b, t // C),
        in_specs=[spec(3 * D_MODEL), spec(AB_COLS), spec(D_MODEL), _full((CONV_W, 3 * D_MODEL)),
                  _full((1, LANES)), _full((1, LANES)), _full((1, HD_B))],
        out_specs=[spec(D_MODEL),
                   pl.BlockSpec((1, CONV_W - 1, 3 * D_MODEL), lambda b, i: (b, 0, 0)),
                   pl.BlockSpec((1, H_B, HD_B, HD_B), lambda b, i: (b, 0, 0, 0))],
        out_shape=[jax.ShapeDtypeStruct((b, t, D_MODEL), BF16),
                   jax.ShapeDtypeStruct((b, CONV_W - 1, 3 * D_MODEL), F32),
                   jax.ShapeDtypeStruct((b, H_B, HD_B, HD_B), F32)],
        scratch_shapes=[pltpu.VMEM((8 + C, 3 * D_MODEL), F32)],
        compiler_params=_params(("arbitrary", "arbitrary")),
        name="gdn_prompt",
    )(qkv, ab, z, gp["conv_w"], gp["alog"], gp["dtb"], gp["norm_w"])


def _rwkv_sample(pa, shift0, wkv_all, wkv_new, l, rp):
    n = pa.shape[0]
    vec = _full((1, D_MODEL))
    lora = _full((2, LANES, D_MODEL))
    chan = _full((D_MODEL, n))
    r, w, km, v, kkn, bb_, g = pl.pallas_call(
        _rwkv_sample_prep_kernel,
        grid=(1,),
        in_specs=[_full((n, A_COLS)), _full((n, A_COLS)), _full((1, A_COLS)), vec, lora, vec, lora, lora,
                  vec, vec],
        out_specs=[chan] * 7,
        out_shape=[jax.ShapeDtypeStruct((D_MODEL, n), F32)] * 7,
        compiler_params=_params(("arbitrary",)),
        name="rwkv_sample_prep",
    )(pa, shift0, rp["mu"], rp["w0"], rp["w2p"], rp["a0"], rp["a2p"], rp["g2"], rp["k_k"], rp["k_a"])

    hspec = pl.BlockSpec((HD_A, n), lambda h: (h, 0))
    sspec = pl.BlockSpec((None, None, HD_A, HD_A, n), lambda h: (l, h, 0, 0, 0))
    s_new, y = pl.pallas_call(
        _rwkv_sample_state_kernel,
        grid=(H_A,),
        in_specs=[hspec] * 6 + [sspec, pl.BlockSpec(memory_space=pl.ANY)],
        out_specs=[sspec, hspec],
        out_shape=[jax.ShapeDtypeStruct(wkv_all.shape, F32), jax.ShapeDtypeStruct((D_MODEL, n), F32)],
        input_output_aliases={7: 0},
        compiler_params=_params(("arbitrary",)),
        name="rwkv_sample_state",
    )(r, w, km, kkn, bb_, v, wkv_all, wkv_new)

    col = _full((D_MODEL, 1))
    ya = pl.pallas_call(
        _rwkv_sample_post_kernel,
        grid=(1,),
        in_specs=[chan] * 5 + [col] * 3,
        out_specs=_full((n, D_MODEL)),
        out_shape=jax.ShapeDtypeStruct((n, D_MODEL), F32),
        compiler_params=_params(("arbitrary",)),
        name="rwkv_sample_post",
    )(y, r, km, v, g, rp["rk_col"], rp["lnw_col"], rp["lnb_col"])
    return ya, s_new


def _gdn_sample(qkv, conv0, ssm_all, ssm_new, l, ab, z, gp):
    n = qkv.shape[0]
    flat = _full((n, D_MODEL))
    cspec = _full((CONV_W - 1, n, 3 * D_MODEL))
    q, k, v, eg, beta, cnew = pl.pallas_call(
        _gdn_sample_prep_kernel,
        grid=(1,),
        in_specs=[_full((n, 3 * D_MODEL)), cspec, _full((n, AB_COLS)), _full((CONV_W, 3 * D_MODEL)),
                  _full((1, LANES)), _full((1, LANES))],
        out_specs=[flat] * 5 + [cspec],
        out_shape=[jax.ShapeDtypeStruct((n, D_MODEL), F32)] * 5
        + [jax.ShapeDtypeStruct((CONV_W - 1, n, 3 * D_MODEL), F32)],
        compiler_params=_params(("arbitrary",)),
        name="gdn_sample_prep",
    )(qkv, jnp.swapaxes(conv0, 0, 1), ab, gp["conv_w"], gp["alog"], gp["dtb"])

    bb = 4
    assert n % bb == 0
    heads = lambda x: x.reshape(n, H_B, HD_B)
    hspec = pl.BlockSpec((bb, H_B, HD_B), lambda i: (i, 0, 0))
    tspec = pl.BlockSpec((bb, HD_B, H_B), lambda i: (i, 0, 0))
    sspec = pl.BlockSpec((None, bb, H_B, HD_B, HD_B), lambda i: (l, i, 0, 0, 0))
    s_new, o = pl.pallas_call(
        functools.partial(_gdn_sample_state_kernel, bb=bb),
        grid=(n // bb,),
        in_specs=[tspec, tspec, hspec, hspec, hspec, sspec, pl.BlockSpec(memory_space=pl.ANY)],
        out_specs=[sspec, hspec],
        out_shape=[jax.ShapeDtypeStruct(ssm_all.shape, F32), jax.ShapeDtypeStruct((n, H_B, HD_B), F32)],
        input_output_aliases={6: 0},
        compiler_params=_params(("arbitrary",)),
        name="gdn_sample_state",
    )(jnp.swapaxes(heads(q), 1, 2), jnp.swapaxes(heads(k), 1, 2), heads(v), heads(eg), heads(beta), ssm_all,
      ssm_new)

    yb = pl.pallas_call(
        _gdn_sample_post_kernel,
        grid=(1,),
        in_specs=[flat, flat, _full((1, HD_B))],
        out_specs=flat,
        out_shape=jax.ShapeDtypeStruct((n, D_MODEL), F32),
        compiler_params=_params(("arbitrary",)),
        name="gdn_sample_post",
    )(o.reshape(n, D_MODEL), z, gp["norm_w"])
    return yb, jnp.swapaxes(cnew, 0, 1), s_new


def _hi_lo(w):
    hi = w.astype(BF16)
    return jnp.stack([hi, (w - hi.astype(F32)).astype(BF16)])


def _pad_rows(x, rows, at):
    out = jnp.zeros((rows, x.shape[1]), x.dtype)
    return lax.dynamic_update_slice(out, x, (at, 0))


def kernel(x_prompt, x_sample, c_prompt, c_sample, state_rwkv_shift, state_rwkv_wkv, state_gdn_conv, state_gdn_ssm, w_ada, b_ada, norm_pre, norm_post, ffn_up, ffn_down, w_in, w_out, rwkv_mu, rwkv_w0, rwkv_w2, rwkv_a0, rwkv_a2, rwkv_g2, rwkv_k_k, rwkv_k_a, rwkv_r_k, rwkv_ln_w, rwkv_ln_b, gdn_conv, gdn_a_log, gdn_dt_bias, gdn_norm_w):
    depth = w_ada.shape[0]
    bp = x_prompt.shape[0]
    ns = x_sample.shape[0]

    n_c = bp + ns
    n_cp = -(-n_c // 16) * 16
    c_all = jnp.concatenate([c_prompt, c_sample, jnp.zeros((n_cp - n_c, D_MODEL), F32)], axis=0)
    mod = _ada_call(c_all, w_ada, b_ada)

    o_b = A_COLS
    o_ab = o_b + 3 * D_MODEL
    o_z = o_ab + 2 * H_B
    o_g = o_z + D_MODEL
    def w_in_layer(l):
        w = w_in[l]
        return jnp.concatenate(
            [w[:, 0:o_b], w[:, o_b:o_ab], w[:, o_z:o_g], w[:, o_g:], w[:, o_ab:o_z],
             jnp.zeros((D_MODEL, AB_COLS - 2 * H_B), F32)], axis=-1).astype(BF16)

    wkv_t = jnp.transpose(state_rwkv_wkv, (0, 2, 3, 4, 1))

    yp = x_prompt
    ys = x_sample.reshape(1, ns, D_MODEL)
    outs = [[] for _ in range(6)]
    s_wkv = jnp.zeros(wkv_t.shape, F32)
    s_ssm = jnp.zeros(state_gdn_ssm.shape, F32)
    for l in range(depth):
        w_in_r = w_in_layer(l)
        up_b = [_cast_call(ffn_up, (l, j)) for j in range(2)]
        down_b = [_cast_call(ffn_down, (l, j)) for j in range(2)]
        w_out_b = _cast_call(w_out, (l,))
        mod_p = mod[l, :, 0:bp].reshape(3 * N_SUB, bp, 1, D_MODEL)
        mod_s = mod[l, :, bp:bp + ns].reshape(3 * N_SUB, 1, ns, D_MODEL)
        npre = [norm_pre[l, i][None] for i in range(N_SUB)]
        npost = [norm_post[l, i][None] for i in range(N_SUB)]
        row = lambda x: x[l][None]
        tile3 = lambda x: x[l].reshape(N_PAIR, 1, LANES)
        rp = dict(mu=row(rwkv_mu), w0=row(rwkv_w0), a0=row(rwkv_a0), k_k=row(rwkv_k_k), k_a=row(rwkv_k_a),
                  w2p=_hi_lo(_pad_rows(rwkv_w2[l], LANES, 0)), a2p=_hi_lo(_pad_rows(rwkv_a2[l], LANES, W_LORA)),
                  g2=_hi_lo(rwkv_g2[l]), rk3=tile3(rwkv_r_k), lnw3=tile3(rwkv_ln_w), lnb3=tile3(rwkv_ln_b),
                  rk_col=rwkv_r_k[l].reshape(D_MODEL, 1), lnw_col=rwkv_ln_w[l].reshape(D_MODEL, 1),
                  lnb_col=rwkv_ln_b[l].reshape(D_MODEL, 1))
        lane_row = lambda x: jnp.zeros((1, LANES), F32).at[0, 0:H_B].set(x[l])
        gp = dict(conv_w=gdn_conv[l], norm_w=row(gdn_norm_w), alog=lane_row(gdn_a_log),
                  dtb=lane_row(gdn_dt_bias))

        yp = _ffn_call(yp, mod_p, npre[0], npost[0], up_b[0], down_b[0], 0)
        pa, qkv, z, pg, ab = _inproj_call(yp, mod_p, npre[1], w_in_r)
        ya, p_wkv, p_shift = _rwkv_prompt_call(pa, rp)
        yb, p_conv, p_ssm = _gdn_prompt_call(qkv, ab, z, gp)
        yp = _outproj_call(yp, ya, yb, pg, mod_p, npost[1], w_out_b)
        yp = _ffn_call(yp, mod_p, npre[2], npost[2], up_b[1], down_b[1], 2)

        ys = _ffn_call(ys, mod_s, npre[0], npost[0], up_b[0], down_b[0], 0)
        pa, qkv, z, pg, ab = _inproj_call(ys, mod_s, npre[1], w_in_r)
        ya, s_wkv = _rwkv_sample(pa[0], state_rwkv_shift[l], wkv_t, s_wkv, l, rp)
        yb, s_conv, s_ssm = _gdn_sample(qkv[0], state_gdn_conv[l], state_gdn_ssm, s_ssm, l, ab[0], z[0], gp)
        ys = _outproj_call(ys, ya[None], yb[None], pg, mod_s, npost[1], w_out_b)
        ys = _ffn_call(ys, mod_s, npre[2], npost[2], up_b[1], down_b[1], 2)

        for lst, val in zip(outs, (p_shift[:, 0], p_wkv, p_conv, p_ssm, pa[0], s_conv)):
            lst.append(val)

    st = [jnp.stack(o) for o in outs]
    return (yp, ys.reshape(ns, 1, D_MODEL), st[0], st[1], st[2], st[3], st[4],
            jnp.transpose(s_wkv, (0, 4, 1, 2, 3)), st[5], s_ssm)
```

```python
import functools

import jax
import jax.numpy as jnp
from jax import lax
from jax.experimental import pallas as pl
from jax.experimental.pallas import tpu as pltpu

F32 = jnp.float32
BF16 = jnp.bfloat16

D_MODEL = 1024
HD_A = 64
H_A = D_MODEL // HD_A
W_LORA = 64
A_LORA = 64
G_LORA = 128
A_COLS = 3 * D_MODEL + W_LORA + A_LORA + G_LORA
RWKV_GN_EPS = 6.4e-4
HD_B = 128
H_B = D_MODEL // HD_B
CONV_W = 4
D_FF = 2816
N_SUB = 3
FFN_RES = 0.5
RMS_EPS = 1e-6

LANES = 128
N_PAIR = D_MODEL // LANES
RW_C = 64
RW_ROWS = 2
GD_C = 128
AB_COLS = LANES
IN_COLS_R = A_COLS + 3 * D_MODEL + D_MODEL + 2 * D_MODEL + AB_COLS
VMEM_LIMIT = 56 * 1024 * 1024
REC_PASSES = 1

_NN = (((1,), (0,)), ((), ()))
_NT = (((1,), (1,)), ((), ()))


def _dg(a, b, dn=_NN):
    return lax.dot_general(a, b, dn, preferred_element_type=F32)


def _split2(x):
    hi = x.astype(BF16)
    lo = (x - hi.astype(F32)).astype(BF16)
    return hi, lo


def _split3(x):
    h1 = x.astype(BF16)
    r1 = x - h1.astype(F32)
    h2 = r1.astype(BF16)
    h3 = (r1 - h2.astype(F32)).astype(BF16)
    return h1, h2, h3


def _mm(a, b, dn=_NN, passes=REC_PASSES):
    if passes == 1:
        return _dg(a.astype(BF16), b.astype(BF16), dn)
    ah, al = _split2(a)
    bh, bl = _split2(b)
    return _dg(ah, bh, dn) + (_dg(ah, bl, dn) + _dg(al, bh, dn))


def _mm_sel_rhs(x, e, pieces):
    parts = _split3(x) if pieces == 3 else _split2(x)
    out = _dg(parts[0], e)
    for p in parts[1:]:
        out = out + _dg(p, e)
    return out


def _mm_sel_lhs(e, x, pieces):
    parts = _split3(x) if pieces == 3 else _split2(x)
    out = _dg(e, parts[0])
    for p in parts[1:]:
        out = out + _dg(e, p)
    return out


def _iota2(shape, dim):
    return lax.broadcasted_iota(jnp.int32, shape, dim)


def _softplus(x):
    return jnp.maximum(x, 0.0) + jnp.log(1.0 + jnp.exp(-jnp.abs(x)))


def _silu(x):
    return x * jax.nn.sigmoid(x)


def _rms(x, w):
    return x * lax.rsqrt(jnp.mean(x * x, axis=-1, keepdims=True) + RMS_EPS) * w


def _mod_in(x, npre, mod_ref, sub):
    shift = mod_ref[3 * sub, 0]
    scale = mod_ref[3 * sub + 1, 0]
    return _rms(x, npre) * (1.0 + scale) + shift


def _tri_masks(n, top):
    i = _iota2((n, n), 0)
    j = _iota2((n, n), 1)
    masks = []
    s = 1
    while s <= top:
        sh = s.bit_length() - 1
        same = (i >> (sh + 1)) == (j >> (sh + 1))
        masks.append(same & (((i >> sh) & 1) == 1) & (((j >> sh) & 1) == 0))
        s *= 2
    return masks


def _tri_inv(Ls, masks):
    n = Ls[0].shape[0]
    eye = jnp.where(_iota2((n, n), 0) == _iota2((n, n), 1), 1.0, 0.0).astype(F32)
    Xs = [eye - jnp.where(masks[0], L, 0.0) for L in Ls]
    Lb = [L.astype(BF16) for L in Ls]
    zero = jnp.zeros((), BF16)
    for m in masks[1:]:
        Xb = [X.astype(BF16) for X in Xs]
        Ts = [_dg(xb, jnp.where(m, lb, zero)) for xb, lb in zip(Xb, Lb)]
        Xs = [X - _dg(T.astype(BF16), xb) for X, T, xb in zip(Xs, Ts, Xb)]
    return Xs


def _to_tiles(x):
    return jnp.stack([x[:, LANES * p:LANES * (p + 1)] for p in range(N_PAIR)], axis=0)


def _pair_headsum(x3, ebd):
    p, n, l = x3.shape
    return _mm_sel_rhs(x3.reshape(p * n, l), ebd, 2).reshape(p, n, l)


def _ebd():
    i = _iota2((LANES, LANES), 0)
    j = _iota2((LANES, LANES), 1)
    return jnp.where((i >> 6) == (j >> 6), 1.0, 0.0).astype(BF16)


def _mm_lora(a, w_ref):
    ah, al = _split2(a)
    return _dg(ah, w_ref[0]) + (_dg(ah, w_ref[1]) + _dg(al, w_ref[0]))


def _rwkv_prep(pa, prev, mu, w0, w2p, a0, a2p, g2, k_k, k_a):
    xs = pa + mu * (prev - pa)
    r = xs[:, 0:D_MODEL]
    k = xs[:, D_MODEL:2 * D_MODEL]
    v = xs[:, 2 * D_MODEL:3 * D_MODEL]
    wa = xs[:, 3 * D_MODEL:3 * D_MODEL + LANES]
    gd = xs[:, 3 * D_MODEL + LANES:A_COLS]
    w_lin = _mm_lora(jnp.tanh(wa), w2p)
    a_lin = _mm_lora(wa, a2p)
    g = _mm_lora(jax.nn.sigmoid(gd), g2)
    w_log = -_softplus(-(w0 + w_lin)) - 0.5
    lw = -jnp.exp(w_log)
    a = jax.nn.sigmoid(a0 + a_lin)
    kkr = k * k_k
    km = k * (1.0 + (a - 1.0) * k_a)
    return r, km, v, kkr, a, lw, g


def _rwkv_epilogue(y3, r3, km3, v3, rk3, lnw3, lnb3, ebd):
    mean = _pair_headsum(y3, ebd) * (1.0 / HD_A)
    yc = y3 - mean
    var = _pair_headsum(yc * yc, ebd) * (1.0 / HD_A)
    yn = yc * lax.rsqrt(var + RWKV_GN_EPS) * lnw3 + lnb3
    bonus = _pair_headsum(r3 * km3 * rk3, ebd) * v3
    return yn + bonus


def _gdn_gates(ab, alog, dtb):
    g_all = -jnp.exp(alog) * _softplus(ab + dtb)
    beta_all = jax.nn.sigmoid(ab)
    return g_all, beta_all


def _sel_matrix(offset):
    i = _iota2((LANES, D_MODEL), 0)
    j = _iota2((LANES, D_MODEL), 1)
    return jnp.where(i == (j >> 7) + offset, 1.0, 0.0).astype(BF16)


def _head_l2norm(x, eps):
    outs = []
    for h in range(H_B):
        xh = x[:, LANES * h:LANES * (h + 1)]
        outs.append(xh * lax.rsqrt(jnp.sum(xh * xh, axis=-1, keepdims=True) + eps))
    return outs


def _ada_kernel(c_ref, w_ref, b_ref, o_ref):
    s = _silu(c_ref[...]).astype(BF16)
    o_ref[...] = _dg(s, w_ref[...].astype(BF16)) + b_ref[...]


def _ffn_kernel(x_ref, mod_ref, npre_ref, npost_ref, up_ref, down_ref, o_ref, *, sub):
    x = x_ref[0]
    h = _mod_in(x, npre_ref[...], mod_ref, sub).astype(BF16)
    gate = _dg(h, up_ref[:, 0:D_FF])
    val = _dg(h, up_ref[:, D_FF:2 * D_FF])
    act = (_silu(gate) * val).astype(BF16)
    y = _dg(act, down_ref[...])
    o_ref[0] = x + FFN_RES * mod_ref[3 * sub + 2, 0] * _rms(y, npost_ref[...])


def _inproj_kernel(x_ref, mod_ref, npre_ref, wt_ref, pa_ref, qkv_ref, z_ref, pg_ref, ab_ref):
    h = _mod_in(x_ref[0], npre_ref[...], mod_ref, 1).astype(BF16)
    o_qkv = A_COLS
    o_ab = o_qkv + 3 * D_MODEL
    o_z = o_ab + 2 * H_B
    o_pg = o_z + D_MODEL
    for ref, o in ((pa_ref, 0), (qkv_ref, o_qkv), (z_ref, o_z), (pg_ref, o_pg), (ab_ref, o_ab)):
        n = ref.shape[-1]
        ref[0] = _dg(h, wt_ref[o:o + n, :], _NT).astype(ref.dtype)


def _causal_conv_silu(x, hist_ref, cw_ref):
    n = x.shape[0]
    hist_ref[8:8 + n, :] = x
    conv = x * cw_ref[CONV_W - 1:CONV_W, :]
    for s in range(1, CONV_W):
        conv = conv + hist_ref[8 - s:8 - s + n, :] * cw_ref[CONV_W - 1 - s:CONV_W - s, :]
    hist_ref[0:8, :] = x[n - 8:n]
    return _silu(conv)


def _outproj_kernel(x_ref, ya_ref, yb_ref, pg_ref, mod_ref, npost_ref, w_ref, o_ref):
    pg = pg_ref[0].astype(F32)
    ya = ya_ref[0].astype(F32)
    yb = yb_ref[0].astype(F32)
    m = jax.nn.sigmoid(pg[:, 0:D_MODEL]) * ya + jax.nn.sigmoid(pg[:, D_MODEL:]) * yb
    y = _dg(m.astype(BF16), w_ref[...])
    o_ref[0] = x_ref[0] + mod_ref[3 * 1 + 2, 0] * _rms(y, npost_ref[...])


def _rwkv_prompt_kernel(pa_ref, mu_ref, w0_ref, w2_ref, a0_ref, a2_ref, g2_ref, kk_ref, ka_ref,
                        rk3_ref, lnw3_ref, lnb3_ref, ya_ref, s_ref, shift_ref, carry_ref, sbd_ref):
    C = RW_C
    R = RW_ROWS
    N = R * C
    t = pl.program_id(1)

    @pl.when(t == 0)
    def _():
        carry_ref[...] = jnp.zeros_like(carry_ref)
        sbd_ref[...] = jnp.zeros_like(sbd_ref)

    pa = pa_ref[...].reshape(N, A_COLS)
    last = jnp.concatenate([jnp.broadcast_to(carry_ref[s:s + 1, :], (C, A_COLS)) for s in range(R)], axis=0)
    prev = jnp.where((_iota2((N, 1), 0) & (C - 1)) == 0, last, pltpu.roll(pa, 1, 0))
    for s in range(R):
        carry_ref[s:s + 1, :] = pa[s * C + C - 1:s * C + C]
    r, km, v, kkr, a, lw, g = _rwkv_prep(pa, prev, mu_ref[...], w0_ref[...], w2_ref, a0_ref[...],
                                         a2_ref, g2_ref, kk_ref[...], ka_ref[...])

    ti = _iota2((N, N), 0)
    tj = _iota2((N, N), 1)
    csh = C.bit_length() - 1
    tri = jnp.where((ti >= tj) & ((ti >> csh) == (tj >> csh)), 1.0, 0.0).astype(BF16)
    cum = _mm_sel_lhs(tri, lw, 3)
    ebd = _ebd()
    NQ = N_PAIR * R
    tiles = lambda x: _to_tiles(x).reshape(NQ, C, LANES)
    r3, km3, v3, kkr3, a3, lw3, cum3 = (tiles(x) for x in (r, km, v, kkr, a, lw, cum))
    kk3 = kkr3 * lax.rsqrt(_pair_headsum(kkr3 * kkr3, ebd) + 1e-12)
    b3 = kk3 * a3
    cl = cum3[:, C - 1:C, :]
    e_neg = jnp.exp(-cum3)
    e_end = jnp.exp(cl - cum3)
    rh3 = r3 * jnp.exp(cum3)
    ah3 = kk3 * jnp.exp(cum3 - lw3)
    bt3 = b3 * e_neg
    kt3 = km3 * e_neg
    be3 = b3 * e_end
    ke3 = km3 * e_end
    e_last3 = jnp.exp(cl)

    lane0 = _iota2((1, LANES), 1) < HD_A

    def sm(x):
        return jnp.concatenate([jnp.where(lane0, x, 0.0), jnp.where(lane0, 0.0, x)], axis=0)

    ii = _iota2((2 * C, 2 * C), 0)
    jj = _iota2((2 * C, 2 * C), 1)
    strict = jj < ii
    incl = jj <= ii
    masks = _tri_masks(2 * C, C // 2)

    P = range(NQ)
    T2 = 2 * C
    bsm = lambda x: sm(x).astype(BF16)
    ARs = [jnp.concatenate([bsm(ah3[p]), bsm(rh3[p])], axis=0) for p in P]
    BKs = [jnp.concatenate([bsm(bt3[p]), bsm(kt3[p])], axis=0) for p in P]
    BKe = [jnp.concatenate([bsm(be3[p]), bsm(ke3[p])], axis=0) for p in P]
    Vs = [sm(v3[p]) for p in P]
    Vb = [V.astype(BF16) for V in Vs]
    Ss = [sbd_ref[p] for p in P]
    QQ = [_dg(ARs[p], BKs[p], _NT) for p in P]
    QS = [_dg(ARs[p], Ss[p].astype(BF16), _NT) for p in P]
    Lab = [jnp.where(strict, Q[0:T2, 0:T2], 0.0) for Q in QQ]
    Lak = [jnp.where(strict, Q[0:T2, T2:2 * T2], 0.0).astype(BF16) for Q in QQ]
    Ms = [QS[p][0:T2] + _dg(Lak[p], Vb[p]) for p in P]
    Tinv = _tri_inv(Lab, masks)
    Us = [-_dg(Tinv[p].astype(BF16), Ms[p].astype(BF16)) for p in P]
    UVb = [jnp.concatenate([Us[p].astype(BF16), Vb[p]], axis=0) for p in P]
    incl2 = jnp.concatenate([incl, incl], axis=1)
    Arbk = [jnp.where(incl2, Q[T2:2 * T2, :], 0.0).astype(BF16) for Q in QQ]
    Ys = [QS[p][T2:2 * T2] + _dg(Arbk[p], UVb[p]) for p in P]
    for p in P:
        UVt = jnp.concatenate([Us[p].T, Vs[p].T], axis=1).astype(BF16)
        sbd_ref[p] = Ss[p] * e_last3[p] + _dg(UVt, BKe[p])
    y3 = jnp.stack([Y[0:C] + Y[C:2 * C] for Y in Ys], axis=0)

    out3 = _rwkv_epilogue(y3, r3, km3, v3, rk3_ref[...], lnw3_ref[...], lnb3_ref[...], ebd)
    for q in P:
        p, s = divmod(q, R)
        ya_ref[s, :, LANES * p:LANES * (p + 1)] = (
            out3[q] * g[s * C:(s + 1) * C, LANES * p:LANES * (p + 1)]).astype(ya_ref.dtype)

    @pl.when(t == pl.num_programs(1) - 1)
    def _():
        for s in range(R):
            shift_ref[s] = pa[s * C + C - 1:s * C + C]
        for q in P:
            p, s = divmod(q, R)
            S = sbd_ref[q]
            s_ref[s, 2 * p] = S[0:HD_A, 0:HD_A]
            s_ref[s, 2 * p + 1] = pltpu.roll(S, HD_A, 1)[HD_A:LANES, 0:HD_A]


def _gdn_prompt_kernel(qkv_ref, ab_ref, z_ref, cw_ref, alog_ref, dtb_ref, nw_ref,
                       yb_ref, conv_ref, ssm_ref, hist_ref):
    C = GD_C
    t = pl.program_id(1)

    @pl.when(t == 0)
    def _():
        hist_ref[0:8, :] = jnp.zeros((8, 3 * D_MODEL), F32)
        ssm_ref[...] = jnp.zeros_like(ssm_ref)

    x = qkv_ref[0]
    qkvc = _causal_conv_silu(x, hist_ref, cw_ref)
    qs = _head_l2norm(qkvc[:, 0:D_MODEL], 1e-6)
    ks = _head_l2norm(qkvc[:, D_MODEL:2 * D_MODEL], 1e-6)
    vv = qkvc[:, 2 * D_MODEL:3 * D_MODEL]

    g_all, beta_all = _gdn_gates(ab_ref[0], alog_ref[...], dtb_ref[...])
    tri = jnp.where(_iota2((C, C), 0) >= _iota2((C, C), 1), 1.0, 0.0).astype(BF16)
    gc_all = _mm_sel_lhs(tri, g_all, 3)

    ii = _iota2((C, C), 0)
    jj = _iota2((C, C), 1)
    strict = jj < ii
    incl = jj <= ii
    masks = _tri_masks(C, C // 2)
    z = z_ref[0].astype(F32)

    H = range(H_B)
    sls = [slice(LANES * h, LANES * (h + 1)) for h in H]
    qh = [qs[h] * (HD_B ** -0.5) for h in H]
    bcast = lambda col: jnp.broadcast_to(col, (C, LANES))
    gh = [bcast(gc_all[:, h:h + 1]) for h in H]
    bh = [bcast(beta_all[:, H_B + h:H_B + h + 1]) for h in H]
    gct = gc_all.T
    dec = [jnp.exp(gh[h] - gct[h:h + 1, :]) for h in H]
    eg = [jnp.exp(g) for g in gh]
    glast = [g[C - 1:C, :] for g in gh]
    kb = [ks[h] * bh[h] for h in H]
    KQ = [_dg(jnp.concatenate([kb[h], qh[h]], axis=0).astype(BF16), ks[h].astype(BF16), _NT) for h in H]
    Ls = [jnp.where(strict, KQ[h][0:C] * dec[h], 0.0) for h in H]
    attn = [jnp.where(incl, KQ[h][C:2 * C] * dec[h], 0.0) for h in H]
    Tinv = _tri_inv(Ls, masks)
    rhs = [jnp.concatenate([vv[:, sls[h]] * bh[h], kb[h] * eg[h]], axis=1).astype(BF16) for h in H]
    UW = [_dg(Tinv[h].astype(BF16), rhs[h]) for h in H]
    Ss = [ssm_ref[0, h] for h in H]
    Sb = [S.astype(BF16) for S in Ss]
    v_new = [UW[h][:, 0:LANES] - _dg(UW[h][:, LANES:2 * LANES].astype(BF16), Sb[h]) for h in H]
    vnb = [v.astype(BF16) for v in v_new]
    os_ = [_dg(jnp.concatenate([qh[h] * eg[h], attn[h]], axis=1).astype(BF16),
               jnp.concatenate([Sb[h], vnb[h]], axis=0)) for h in H]
    for h in H:
        k_dec = ks[h] * jnp.exp(glast[h] - gh[h])
        ssm_ref[0, h] = Ss[h] * jnp.exp(glast[h]) + _dg(k_dec.T.astype(BF16), vnb[h])
    for h in H:
        o = os_[h]
        on = o * lax.rsqrt(jnp.mean(o * o, axis=-1, keepdims=True) + RMS_EPS) * nw_ref[...]
        yb_ref[0, :, sls[h]] = (on * _silu(z[:, sls[h]])).astype(yb_ref.dtype)

    @pl.when(t == pl.num_programs(1) - 1)
    def _():
        conv_ref[0] = x[C - (CONV_W - 1):C]


def _rwkv_sample_prep_kernel(pa_ref, prev_ref, mu_ref, w0_ref, w2_ref, a0_ref, a2_ref, g2_ref, kk_ref,
                             ka_ref, r_ref, w_ref, km_ref, v_ref, kkn_ref, b_ref, g_ref):
    r, km, v, kkr, a, lw, g = _rwkv_prep(pa_ref[...], prev_ref[...], mu_ref[...], w0_ref[...], w2_ref,
                                         a0_ref[...], a2_ref, g2_ref, kk_ref[...], ka_ref[...])
    n = r.shape[0]
    heads = lambda x: x.reshape(H_A, HD_A, n)
    kk3 = heads(kkr.T)
    kkn = kk3 * lax.rsqrt(jnp.sum(kk3 * kk3, axis=1, keepdims=True) + 1e-12)
    r_ref[...] = r.T
    w_ref[...] = jnp.exp(lw).T
    km_ref[...] = km.T
    v_ref[...] = v.T
    kkn_ref[...] = kkn.reshape(D_MODEL, n)
    b_ref[...] = (kkn * heads(a.T)).reshape(D_MODEL, n)
    g_ref[...] = g.T


def _rwkv_sample_state_kernel(r_ref, w_ref, km_ref, kkn_ref, b_ref, v_ref, s0_ref, stacked_ref, s_ref, y_ref):
    del stacked_ref
    rh, wh, kh, kkh, bh = r_ref[...], w_ref[...], km_ref[...], kkn_ref[...], b_ref[...]
    G = 8
    row = _iota2((G, 1), 0)

    def body(j, carry):
        base = pl.multiple_of(j * G, G)
        vrows = v_ref[pl.ds(base, G), :]
        U = range(G)
        Ss = [s0_ref[base + u] for u in U]
        sas = [-jnp.sum(Ss[u] * kkh, axis=0, keepdims=True) for u in U]
        Sn = [Ss[u] * wh + sas[u] * bh + vrows[u:u + 1, :] * kh for u in U]
        for u in U:
            s_ref[base + u] = Sn[u]
        yrows = [jnp.sum(Sn[u] * rh, axis=0, keepdims=True) for u in U]
        y = jnp.zeros_like(vrows)
        for u in U:
            y = jnp.where(row == u, yrows[u], y)
        y_ref[pl.ds(base, G), :] = y
        return carry

    lax.fori_loop(0, HD_A // G, body, 0)


def _rwkv_sample_post_kernel(y_ref, r_ref, km_ref, v_ref, g_ref, rk_ref, lnw_ref, lnb_ref, ya_ref):
    n = y_ref.shape[1]
    heads = lambda x: x.reshape(H_A, HD_A, n)
    y3 = heads(y_ref[...])
    yc = y3 - jnp.mean(y3, axis=1, keepdims=True)
    var = jnp.mean(yc * yc, axis=1, keepdims=True)
    yn = (yc * lax.rsqrt(var + RWKV_GN_EPS)).reshape(D_MODEL, n) * lnw_ref[...] + lnb_ref[...]
    bonus = jnp.sum(heads(r_ref[...] * km_ref[...] * rk_ref[...]), axis=1, keepdims=True) * heads(v_ref[...])
    out = (yn + bonus.reshape(D_MODEL, n)) * g_ref[...]
    ya_ref[...] = out.T


def _gdn_sample_prep_kernel(qkv_ref, c0_ref, ab_ref, cw_ref, alog_ref, dtb_ref,
                            q_ref, k_ref, v_ref, eg_ref, beta_ref, cnew_ref):
    x = qkv_ref[...]
    conv = x * cw_ref[CONV_W - 1:CONV_W, :]
    for j in range(CONV_W - 1):
        conv = conv + c0_ref[j] * cw_ref[j:j + 1, :]
    for j in range(CONV_W - 2):
        cnew_ref[j] = c0_ref[j + 1]
    cnew_ref[CONV_W - 2] = x
    qkvc = _silu(conv)
    qs = _head_l2norm(qkvc[:, 0:D_MODEL], 1e-6)
    ks = _head_l2norm(qkvc[:, D_MODEL:2 * D_MODEL], 1e-6)
    for h in range(H_B):
        q_ref[:, LANES * h:LANES * (h + 1)] = qs[h] * (HD_B ** -0.5)
        k_ref[:, LANES * h:LANES * (h + 1)] = ks[h]
    v_ref[...] = qkvc[:, 2 * D_MODEL:3 * D_MODEL]
    g_all, beta_all = _gdn_gates(ab_ref[...], alog_ref[...], dtb_ref[...])
    eg_ref[...] = jnp.exp(_mm_sel_rhs(g_all, _sel_matrix(0), 3))
    beta_ref[...] = _mm_sel_rhs(beta_all, _sel_matrix(H_B), 3)


def _gdn_sample_state_kernel(qt_ref, kt_ref, v_ref, eg_ref, beta_ref, s0_ref, stacked_ref, s_ref, o_ref, *, bb):
    del stacked_ref

    def body(i, carry):
        qt, kt = qt_ref[i], kt_ref[i]
        vb, egb, btb = v_ref[i], eg_ref[i], beta_ref[i]
        H = range(H_B)
        kc = [jnp.broadcast_to(kt[:, h:h + 1], (HD_B, HD_B)) for h in H]
        Sd = [s0_ref[i, h] * egb[h:h + 1, :] for h in H]
        ksr = [jnp.sum(Sd[h] * kc[h], axis=0, keepdims=True) for h in H]
        v_new = [btb[h:h + 1, :] * (vb[h:h + 1, :] - ksr[h]) for h in H]
        Sn = [Sd[h] + kc[h] * v_new[h] for h in H]
        for h in H:
            s_ref[i, h] = Sn[h]
        for h in H:
            o_ref[i, h:h + 1, :] = jnp.sum(Sn[h] * qt[:, h:h + 1], axis=0, keepdims=True)
        return carry

    lax.fori_loop(0, bb, body, 0, unroll=2)


def _gdn_sample_post_kernel(o_ref, z_ref, nw_ref, yb_ref):
    o = o_ref[...]
    z = z_ref[...].astype(F32)
    for h in range(H_B):
        sl = slice(LANES * h, LANES * (h + 1))
        oh = o[:, sl]
        on = oh * lax.rsqrt(jnp.mean(oh * oh, axis=-1, keepdims=True) + RMS_EPS) * nw_ref[...]
        yb_ref[:, sl] = on * _silu(z[:, sl])


def _params(sem):
    return pltpu.CompilerParams(dimension_semantics=sem, vmem_limit_bytes=VMEM_LIMIT)


def _full(shape):
    return pl.BlockSpec(shape, lambda *_: (0,) * len(shape))


def _weight_spec(w, idx):
    lead = len(idx)
    return pl.BlockSpec((None,) * lead + tuple(w.shape[lead:]), lambda *_: tuple(idx) + (0, 0),
                        pipeline_mode=pl.Buffered(1))


def _row_tile(t, cap):
    tm = min(t, cap)
    assert t % tm == 0
    return tm


def _mod_spec(mod):
    rows = mod.shape[2]
    if rows == 1:
        return pl.BlockSpec((3 * N_SUB, 1, 1, D_MODEL), lambda b, i: (0, b, 0, 0))
    return pl.BlockSpec((3 * N_SUB, 1, rows, D_MODEL), lambda b, i: (0, 0, i, 0))


def _ada_call(c_all, w_ada, b_ada):
    depth = w_ada.shape[0]
    n = c_all.shape[0]
    return pl.pallas_call(
        _ada_kernel,
        grid=(depth, 3 * N_SUB),
        in_specs=[pl.BlockSpec((n, D_MODEL), lambda l, j: (0, 0)),
                  pl.BlockSpec((None, D_MODEL, D_MODEL), lambda l, j: (l, 0, j)),
                  pl.BlockSpec((None, 1, D_MODEL), lambda l, j: (l, 0, j))],
        out_specs=pl.BlockSpec((None, None, n, D_MODEL), lambda l, j: (l, j, 0, 0)),
        out_shape=jax.ShapeDtypeStruct((depth, 3 * N_SUB, n, D_MODEL), F32),
        compiler_params=_params(("arbitrary", "arbitrary")),
        name="ada_mod",
    )(c_all, w_ada, b_ada.reshape(depth, 1, 3 * N_SUB * D_MODEL))


def _ffn_call(x, mod, npre, npost, up, down, widx, sub):
    b, t, _ = x.shape
    tm = _row_tile(t, 512)
    xspec = pl.BlockSpec((1, tm, D_MODEL), lambda b, i: (b, i, 0))
    return pl.pallas_call(
        functools.partial(_ffn_kernel, sub=sub),
        grid=(b, t // tm),
        in_specs=[xspec, _mod_spec(mod), _full((1, D_MODEL)), _full((1, D_MODEL)),
                  _weight_spec(up, widx), _weight_spec(down, widx)],
        out_specs=xspec,
        out_shape=jax.ShapeDtypeStruct(x.shape, F32),
        compiler_params=_params(("arbitrary", "arbitrary")),
        name="ffn",
    )(x, mod, npre, npost, up, down)


def _inproj_call(x, mod, npre, w, widx):
    b, t, _ = x.shape
    tm = _row_tile(t, 256)
    widths = (A_COLS, 3 * D_MODEL, D_MODEL, 2 * D_MODEL, AB_COLS)
    dtypes = (F32, F32, BF16, BF16, F32)
    spec = lambda n: pl.BlockSpec((1, tm, n), lambda b, i: (b, i, 0))
    return pl.pallas_call(
        _inproj_kernel,
        grid=(b, t // tm),
        in_specs=[spec(D_MODEL), _mod_spec(mod), _full((1, D_MODEL)), _weight_spec(w, widx)],
        out_specs=[spec(n) for n in widths],
        out_shape=[jax.ShapeDtypeStruct((b, t, n), d) for n, d in zip(widths, dtypes)],
        compiler_params=_params(("arbitrary", "arbitrary")),
        name="in_proj",
    )(x, mod, npre, w)


def _outproj_call(x, ya, yb, pg, mod, npost, w, widx):
    b, t, _ = x.shape
    tm = _row_tile(t, 512)
    spec = lambda n: pl.BlockSpec((1, tm, n), lambda b, i: (b, i, 0))
    return pl.pallas_call(
        _outproj_kernel,
        grid=(b, t // tm),
        in_specs=[spec(D_MODEL), spec(D_MODEL), spec(D_MODEL), spec(2 * D_MODEL), _mod_spec(mod),
                  _full((1, D_MODEL)), _weight_spec(w, widx)],
        out_specs=spec(D_MODEL),
        out_shape=jax.ShapeDtypeStruct(x.shape, F32),
        compiler_params=_params(("arbitrary", "arbitrary")),
        name="out_proj",
    )(x, ya, yb, pg, mod, npost, w)


def _rwkv_prompt_call(pa, rp):
    b, t, _ = pa.shape
    C = RW_C
    R = RW_ROWS
    assert t % C == 0 and b % R == 0
    vec = _full((1, D_MODEL))
    lora = _full((2, LANES, D_MODEL))
    tile3 = _full((N_PAIR * R, 1, LANES))
    per_tile = lambda x: jnp.repeat(x, R, axis=0)
    return pl.pallas_call(
        _rwkv_prompt_kernel,
        grid=(b // R, t // C),
        in_specs=[pl.BlockSpec((R, C, A_COLS), lambda b, i: (b, i, 0)), _full((1, A_COLS)),
                  vec, lora, vec, lora, lora, vec, vec, tile3, tile3, tile3],
        out_specs=[pl.BlockSpec((R, C, D_MODEL), lambda b, i: (b, i, 0)),
                   pl.BlockSpec((R, H_A, HD_A, HD_A), lambda b, i: (b, 0, 0, 0)),
                   pl.BlockSpec((R, 1, A_COLS), lambda b, i: (b, 0, 0))],
        out_shape=[jax.ShapeDtypeStruct((b, t, D_MODEL), BF16),
                   jax.ShapeDtypeStruct((b, H_A, HD_A, HD_A), F32),
                   jax.ShapeDtypeStruct((b, 1, A_COLS), F32)],
        scratch_shapes=[pltpu.VMEM((R, A_COLS), F32), pltpu.VMEM((N_PAIR * R, LANES, LANES), F32)],
        compiler_params=_params(("arbitrary", "arbitrary")),
        name="rwkv_prompt",
    )(pa, rp["mu"], rp["w0"], rp["w2p"], rp["a0"], rp["a2p"], rp["g2"], rp["k_k"], rp["k_a"],
      per_tile(rp["rk3"]), per_tile(rp["lnw3"]), per_tile(rp["lnb3"]))


def _gdn_prompt_call(qkv, ab, z, gp):
    b, t, _ = qkv.shape
    C = GD_C
    assert t % C == 0
    spec = lambda n: pl.BlockSpec((1, C, n), lambda b, i: (b, i, 0))
    return pl.pallas_call(
        _gdn_prompt_kernel,
        grid=(b, t // C),
        in_specs=[spec(3 * D_MODEL), spec(AB_COLS), spec(D_MODEL), _full((CONV_W, 3 * D_MODEL)),
                  _full((1, LANES)), _full((1, LANES)), _full((1, HD_B))],
        out_specs=[spec(D_MODEL),
                   pl.BlockSpec((1, CONV_W - 1, 3 * D_MODEL), lambda b, i: (b, 0, 0)),
                   pl.BlockSpec((1, H_B, HD_B, HD_B), lambda b, i: (b, 0, 0, 0))],
        out_shape=[jax.ShapeDtypeStruct((b, t, D_MODEL), BF16),
                   jax.ShapeDtypeStruct((b, CONV_W - 1, 3 * D_MODEL), F32),
                   jax.ShapeDtypeStruct((b, H_B, HD_B, HD_B), F32)],
        scratch_shapes=[pltpu.VMEM((8 + C, 3 * D_MODEL), F32)],
        compiler_params=_params(("arbitrary", "arbitrary")),
        name="gdn_prompt",
    )(qkv, ab, z, gp["conv_w"], gp["alog"], gp["dtb"], gp["norm_w"])


def _rwkv_sample(pa, shift0, wkv_all, wkv_new, l, rp):
    n = pa.shape[0]
    vec = _full((1, D_MODEL))
    lora = _full((2, LANES, D_MODEL))
    chan = _full((D_MODEL, n))
    r, w, km, v, kkn, bb_, g = pl.pallas_call(
        _rwkv_sample_prep_kernel,
        grid=(1,),
        in_specs=[_full((n, A_COLS)), _full((n, A_COLS)), _full((1, A_COLS)), vec, lora, vec, lora, lora,
                  vec, vec],
        out_specs=[chan] * 7,
        out_shape=[jax.ShapeDtypeStruct((D_MODEL, n), F32)] * 7,
        compiler_params=_params(("arbitrary",)),
        name="rwkv_sample_prep",
    )(pa, shift0, rp["mu"], rp["w0"], rp["w2p"], rp["a0"], rp["a2p"], rp["g2"], rp["k_k"], rp["k_a"])

    hspec = pl.BlockSpec((HD_A, n), lambda h: (h, 0))
    sspec = pl.BlockSpec((None, None, HD_A, HD_A, n), lambda h: (l, h, 0, 0, 0))
    s_new, y = pl.pallas_call(
        _rwkv_sample_state_kernel,
        grid=(H_A,),
        in_specs=[hspec] * 6 + [sspec, pl.BlockSpec(memory_space=pl.ANY)],
        out_specs=[sspec, hspec],
        out_shape=[jax.ShapeDtypeStruct(wkv_all.shape, F32), jax.ShapeDtypeStruct((D_MODEL, n), F32)],
        input_output_aliases={7: 0},
        compiler_params=_params(("arbitrary",)),
        name="rwkv_sample_state",
    )(r, w, km, kkn, bb_, v, wkv_all, wkv_new)

    col = _full((D_MODEL, 1))
    ya = pl.pallas_call(
        _rwkv_sample_post_kernel,
        grid=(1,),
        in_specs=[chan] * 5 + [col] * 3,
        out_specs=_full((n, D_MODEL)),
        out_shape=jax.ShapeDtypeStruct((n, D_MODEL), F32),
        compiler_params=_params(("arbitrary",)),
        name="rwkv_sample_post",
    )(y, r, km, v, g, rp["rk_col"], rp["lnw_col"], rp["lnb_col"])
    return ya, s_new


def _gdn_sample(qkv, conv0, ssm_all, ssm_new, l, ab, z, gp):
    n = qkv.shape[0]
    flat = _full((n, D_MODEL))
    cspec = _full((CONV_W - 1, n, 3 * D_MODEL))
    q, k, v, eg, beta, cnew = pl.pallas_call(
        _gdn_sample_prep_kernel,
        grid=(1,),
        in_specs=[_full((n, 3 * D_MODEL)), cspec, _full((n, AB_COLS)), _full((CONV_W, 3 * D_MODEL)),
                  _full((1, LANES)), _full((1, LANES))],
        out_specs=[flat] * 5 + [cspec],
        out_shape=[jax.ShapeDtypeStruct((n, D_MODEL), F32)] * 5
        + [jax.ShapeDtypeStruct((CONV_W - 1, n, 3 * D_MODEL), F32)],
        compiler_params=_params(("arbitrary",)),
        name="gdn_sample_prep",
    )(qkv, jnp.swapaxes(conv0, 0, 1), ab, gp["conv_w"], gp["alog"], gp["dtb"])

    bb = 4
    assert n % bb == 0
    heads = lambda x: x.reshape(n, H_B, HD_B)
    hspec = pl.BlockSpec((bb, H_B, HD_B), lambda i: (i, 0, 0))
    tspec = pl.BlockSpec((bb, HD_B, H_B), lambda i: (i, 0, 0))
    sspec = pl.BlockSpec((None, bb, H_B, HD_B, HD_B), lambda i: (l, i, 0, 0, 0))
    s_new, o = pl.pallas_call(
        functools.partial(_gdn_sample_state_kernel, bb=bb),
        grid=(n // bb,),
        in_specs=[tspec, tspec, hspec, hspec, hspec, sspec, pl.BlockSpec(memory_space=pl.ANY)],
        out_specs=[sspec, hspec],
        out_shape=[jax.ShapeDtypeStruct(ssm_all.shape, F32), jax.ShapeDtypeStruct((n, H_B, HD_B), F32)],
        input_output_aliases={6: 0},
        compiler_params=_params(("arbitrary",)),
        name="gdn_sample_state",
    )(jnp.swapaxes(heads(q), 1, 2), jnp.swapaxes(heads(k), 1, 2), heads(v), heads(eg), heads(beta), ssm_all,
      ssm_new)

    yb = pl.pallas_call(
        _gdn_sample_post_kernel,
        grid=(1,),
        in_specs=[flat, flat, _full((1, HD_B))],
        out_specs=flat,
        out_shape=jax.ShapeDtypeStruct((n, D_MODEL), F32),
        compiler_params=_params(("arbitrary",)),
        name="gdn_sample_post",
    )(o.reshape(n, D_MODEL), z, gp["norm_w"])
    return yb, jnp.swapaxes(cnew, 0, 1), s_new


def _hi_lo(w):
    hi = w.astype(BF16)
    return jnp.stack([hi, (w - hi.astype(F32)).astype(BF16)])


def _pad_rows(x, rows, at):
    out = jnp.zeros((rows, x.shape[1]), x.dtype)
    return lax.dynamic_update_slice(out, x, (at, 0))


def kernel(x_prompt, x_sample, c_prompt, c_sample, state_rwkv_shift, state_rwkv_wkv, state_gdn_conv, state_gdn_ssm, w_ada, b_ada, norm_pre, norm_post, ffn_up, ffn_down, w_in, w_out, rwkv_mu, rwkv_w0, rwkv_w2, rwkv_a0, rwkv_a2, rwkv_g2, rwkv_k_k, rwkv_k_a, rwkv_r_k, rwkv_ln_w, rwkv_ln_b, gdn_conv, gdn_a_log, gdn_dt_bias, gdn_norm_w):
    depth = w_ada.shape[0]
    bp = x_prompt.shape[0]
    ns = x_sample.shape[0]

    n_c = bp + ns
    n_cp = -(-n_c // 16) * 16
    c_all = jnp.concatenate([c_prompt, c_sample, jnp.zeros((n_cp - n_c, D_MODEL), F32)], axis=0)
    mod = _ada_call(c_all, w_ada, b_ada)

    w_in_t = jnp.transpose(w_in, (0, 2, 1)).astype(BF16)
    up_b = ffn_up.astype(BF16)
    down_b = ffn_down.astype(BF16)
    w_out_b = w_out.astype(BF16)

    wkv_t = jnp.transpose(state_rwkv_wkv, (0, 2, 3, 4, 1))

    yp = x_prompt
    ys = x_sample.reshape(1, ns, D_MODEL)
    outs = [[] for _ in range(6)]
    s_wkv = jnp.zeros(wkv_t.shape, F32)
    s_ssm = jnp.zeros(state_gdn_ssm.shape, F32)
    for l in range(depth):
        mod_p = mod[l, :, 0:bp].reshape(3 * N_SUB, bp, 1, D_MODEL)
        mod_s = mod[l, :, bp:bp + ns].reshape(3 * N_SUB, 1, ns, D_MODEL)
        npre = [norm_pre[l, i][None] for i in range(N_SUB)]
        npost = [norm_post[l, i][None] for i in range(N_SUB)]
        row = lambda x: x[l][None]
        tile3 = lambda x: x[l].reshape(N_PAIR, 1, LANES)
        rp = dict(mu=row(rwkv_mu), w0=row(rwkv_w0), a0=row(rwkv_a0), k_k=row(rwkv_k_k), k_a=row(rwkv_k_a),
                  w2p=_hi_lo(_pad_rows(rwkv_w2[l], LANES, 0)), a2p=_hi_lo(_pad_rows(rwkv_a2[l], LANES, W_LORA)),
                  g2=_hi_lo(rwkv_g2[l]), rk3=tile3(rwkv_r_k), lnw3=tile3(rwkv_ln_w), lnb3=tile3(rwkv_ln_b),
                  rk_col=rwkv_r_k[l].reshape(D_MODEL, 1), lnw_col=rwkv_ln_w[l].reshape(D_MODEL, 1),
                  lnb_col=rwkv_ln_b[l].reshape(D_MODEL, 1))
        lane_row = lambda x: jnp.zeros((1, LANES), F32).at[0, 0:H_B].set(x[l])
        gp = dict(conv_w=gdn_conv[l], norm_w=row(gdn_norm_w), alog=lane_row(gdn_a_log),
                  dtb=lane_row(gdn_dt_bias))

        yp = _ffn_call(yp, mod_p, npre[0], npost[0], up_b, down_b, (l, 0), 0)
        pa, qkv, z, pg, ab = _inproj_call(yp, mod_p, npre[1], w_in_t, (l,))
        ya, p_wkv, p_shift = _rwkv_prompt_call(pa, rp)
        yb, p_conv, p_ssm = _gdn_prompt_call(qkv, ab, z, gp)
        yp = _outproj_call(yp, ya, yb, pg, mod_p, npost[1], w_out_b, (l,))
        yp = _ffn_call(yp, mod_p, npre[2], npost[2], up_b, down_b, (l, 1), 2)

        ys = _ffn_call(ys, mod_s, npre[0], npost[0], up_b, down_b, (l, 0), 0)
        pa, qkv, z, pg, ab = _inproj_call(ys, mod_s, npre[1], w_in_t, (l,))
        ya, s_wkv = _rwkv_sample(pa[0], state_rwkv_shift[l], wkv_t, s_wkv, l, rp)
        yb, s_conv, s_ssm = _gdn_sample(qkv[0], state_gdn_conv[l], state_gdn_ssm, s_ssm, l, ab[0], z[0], gp)
        ys = _outproj_call(ys, ya[None], yb[None], pg, mod_s, npost[1], w_out_b, (l,))
        ys = _ffn_call(ys, mod_s, npre[2], npost[2], up_b, down_b, (l, 1), 2)

        for lst, val in zip(outs, (p_shift[:, 0], p_wkv, p_conv, p_ssm, pa[0], s_conv)):
            lst.append(val)

    st = [jnp.stack(o) for o in outs]
    return (yp, ys.reshape(ns, 1, D_MODEL), st[0], st[1], st[2], st[3], st[4],
            jnp.transpose(s_wkv, (0, 4, 1, 2, 3)), st[5], s_ssm)
```

```python
import functools

import jax
import jax.numpy as jnp
from jax import lax
from jax.experimental import pallas as pl
from jax.experimental.pallas import tpu as pltpu

F32 = jnp.float32
BF16 = jnp.bfloat16

D_MODEL = 1024
HD_A = 64
H_A = D_MODEL // HD_A
W_LORA = 64
A_LORA = 64
G_LORA = 128
A_COLS = 3 * D_MODEL + W_LORA + A_LORA + G_LORA
RWKV_GN_EPS = 6.4e-4
HD_B = 128
H_B = D_MODEL // HD_B
CONV_W = 4
D_FF = 2816
N_SUB = 3
FFN_RES = 0.5
RMS_EPS = 1e-6

LANES = 128
N_PAIR = D_MODEL // LANES
RW_C = 64
RW_ROWS = 4
GD_C = 128
AB_COLS = LANES
IN_COLS_R = A_COLS + 3 * D_MODEL + D_MODEL + 2 * D_MODEL + AB_COLS
VMEM_LIMIT = 56 * 1024 * 1024
REC_PASSES = 1

_NN = (((1,), (0,)), ((), ()))
_NT = (((1,), (1,)), ((), ()))


def _dg(a, b, dn=_NN):
    return lax.dot_general(a, b, dn, preferred_element_type=F32)


def _split2(x):
    hi = x.astype(BF16)
    lo = (x - hi.astype(F32)).astype(BF16)
    return hi, lo


def _split3(x):
    h1 = x.astype(BF16)
    r1 = x - h1.astype(F32)
    h2 = r1.astype(BF16)
    h3 = (r1 - h2.astype(F32)).astype(BF16)
    return h1, h2, h3


def _mm(a, b, dn=_NN, passes=REC_PASSES):
    if passes == 1:
        return _dg(a.astype(BF16), b.astype(BF16), dn)
    ah, al = _split2(a)
    bh, bl = _split2(b)
    return _dg(ah, bh, dn) + (_dg(ah, bl, dn) + _dg(al, bh, dn))


def _mm_sel_rhs(x, e, pieces):
    parts = _split3(x) if pieces == 3 else _split2(x)
    out = _dg(parts[0], e)
    for p in parts[1:]:
        out = out + _dg(p, e)
    return out


def _mm_sel_lhs(e, x, pieces):
    parts = _split3(x) if pieces == 3 else _split2(x)
    out = _dg(e, parts[0])
    for p in parts[1:]:
        out = out + _dg(e, p)
    return out


def _iota2(shape, dim):
    return lax.broadcasted_iota(jnp.int32, shape, dim)


def _softplus(x):
    return jnp.maximum(x, 0.0) + jnp.log(1.0 + jnp.exp(-jnp.abs(x)))


def _silu(x):
    return x * jax.nn.sigmoid(x)


def _rms(x, w):
    return x * lax.rsqrt(jnp.mean(x * x, axis=-1, keepdims=True) + RMS_EPS) * w


def _mod_in(x, npre, mod_ref, sub):
    shift = mod_ref[3 * sub, 0]
    scale = mod_ref[3 * sub + 1, 0]
    return _rms(x, npre) * (1.0 + scale) + shift


def _tri_masks(n, top):
    i = _iota2((n, n), 0)
    j = _iota2((n, n), 1)
    masks = []
    s = 1
    while s <= top:
        sh = s.bit_length() - 1
        same = (i >> (sh + 1)) == (j >> (sh + 1))
        masks.append(same & (((i >> sh) & 1) == 1) & (((j >> sh) & 1) == 0))
        s *= 2
    return masks


def _tri_inv(Ls, masks):
    n = Ls[0].shape[0]
    eye = jnp.where(_iota2((n, n), 0) == _iota2((n, n), 1), 1.0, 0.0).astype(F32)
    Xs = [eye - jnp.where(masks[0], L, 0.0) for L in Ls]
    Lb = [L.astype(BF16) for L in Ls]
    zero = jnp.zeros((), BF16)
    for m in masks[1:]:
        Xb = [X.astype(BF16) for X in Xs]
        Ts = [_dg(xb, jnp.where(m, lb, zero)) for xb, lb in zip(Xb, Lb)]
        Xs = [X - _dg(T.astype(BF16), xb) for X, T, xb in zip(Xs, Ts, Xb)]
    return Xs


def _to_tiles(x):
    return jnp.stack([x[:, LANES * p:LANES * (p + 1)] for p in range(N_PAIR)], axis=0)


def _pair_headsum(x3, ebd):
    p, n, l = x3.shape
    return _mm_sel_rhs(x3.reshape(p * n, l), ebd, 2).reshape(p, n, l)


def _ebd():
    i = _iota2((LANES, LANES), 0)
    j = _iota2((LANES, LANES), 1)
    return jnp.where((i >> 6) == (j >> 6), 1.0, 0.0).astype(BF16)


def _mm_lora(a, w_ref):
    ah, al = _split2(a)
    return _dg(ah, w_ref[0]) + (_dg(ah, w_ref[1]) + _dg(al, w_ref[0]))


def _rwkv_prep(pa, prev, mu, w0, w2p, a0, a2p, g2, k_k, k_a):
    xs = pa + mu * (prev - pa)
    r = xs[:, 0:D_MODEL]
    k = xs[:, D_MODEL:2 * D_MODEL]
    v = xs[:, 2 * D_MODEL:3 * D_MODEL]
    wa = xs[:, 3 * D_MODEL:3 * D_MODEL + LANES]
    gd = xs[:, 3 * D_MODEL + LANES:A_COLS]
    w_lin = _mm_lora(jnp.tanh(wa), w2p)
    a_lin = _mm_lora(wa, a2p)
    g = _mm_lora(jax.nn.sigmoid(gd), g2)
    w_log = -_softplus(-(w0 + w_lin)) - 0.5
    lw = -jnp.exp(w_log)
    a = jax.nn.sigmoid(a0 + a_lin)
    kkr = k * k_k
    km = k * (1.0 + (a - 1.0) * k_a)
    return r, km, v, kkr, a, lw, g


def _rwkv_epilogue(y3, r3, km3, v3, rk3, lnw3, lnb3, ebd):
    mean = _pair_headsum(y3, ebd) * (1.0 / HD_A)
    yc = y3 - mean
    var = _pair_headsum(yc * yc, ebd) * (1.0 / HD_A)
    yn = yc * lax.rsqrt(var + RWKV_GN_EPS) * lnw3 + lnb3
    bonus = _pair_headsum(r3 * km3 * rk3, ebd) * v3
    return yn + bonus


def _gdn_gates(ab, alog, dtb):
    g_all = -jnp.exp(alog) * _softplus(ab + dtb)
    beta_all = jax.nn.sigmoid(ab)
    return g_all, beta_all


def _sel_matrix(offset):
    i = _iota2((LANES, D_MODEL), 0)
    j = _iota2((LANES, D_MODEL), 1)
    return jnp.where(i == (j >> 7) + offset, 1.0, 0.0).astype(BF16)


def _head_l2norm(x, eps):
    outs = []
    for h in range(H_B):
        xh = x[:, LANES * h:LANES * (h + 1)]
        outs.append(xh * lax.rsqrt(jnp.sum(xh * xh, axis=-1, keepdims=True) + eps))
    return outs


def _ada_kernel(c_ref, w_ref, b_ref, o_ref):
    s = _silu(c_ref[...]).astype(BF16)
    o_ref[...] = _dg(s, w_ref[...].astype(BF16)) + b_ref[...]


def _ffn_kernel(x_ref, mod_ref, npre_ref, npost_ref, up_ref, down_ref, o_ref, *, sub):
    x = x_ref[0]
    h = _mod_in(x, npre_ref[...], mod_ref, sub).astype(BF16)
    gate = _dg(h, up_ref[:, 0:D_FF])
    val = _dg(h, up_ref[:, D_FF:2 * D_FF])
    act = (_silu(gate) * val).astype(BF16)
    y = _dg(act, down_ref[...])
    o_ref[0] = x + FFN_RES * mod_ref[3 * sub + 2, 0] * _rms(y, npost_ref[...])


def _inproj_kernel(x_ref, mod_ref, npre_ref, wt_ref, pa_ref, qkv_ref, z_ref, pg_ref, ab_ref):
    h = _mod_in(x_ref[0], npre_ref[...], mod_ref, 1).astype(BF16)
    o_qkv = A_COLS
    o_ab = o_qkv + 3 * D_MODEL
    o_z = o_ab + 2 * H_B
    o_pg = o_z + D_MODEL
    for ref, o in ((pa_ref, 0), (qkv_ref, o_qkv), (z_ref, o_z), (pg_ref, o_pg), (ab_ref, o_ab)):
        n = ref.shape[-1]
        ref[0] = _dg(h, wt_ref[o:o + n, :], _NT).astype(ref.dtype)


def _causal_conv_silu(x, hist_ref, cw_ref):
    n = x.shape[0]
    hist_ref[8:8 + n, :] = x
    conv = x * cw_ref[CONV_W - 1:CONV_W, :]
    for s in range(1, CONV_W):
        conv = conv + hist_ref[8 - s:8 - s + n, :] * cw_ref[CONV_W - 1 - s:CONV_W - s, :]
    hist_ref[0:8, :] = x[n - 8:n]
    return _silu(conv)


def _outproj_kernel(x_ref, ya_ref, yb_ref, pg_ref, mod_ref, npost_ref, w_ref, o_ref):
    pg = pg_ref[0].astype(F32)
    ya = ya_ref[0].astype(F32)
    yb = yb_ref[0].astype(F32)
    m = jax.nn.sigmoid(pg[:, 0:D_MODEL]) * ya + jax.nn.sigmoid(pg[:, D_MODEL:]) * yb
    y = _dg(m.astype(BF16), w_ref[...])
    o_ref[0] = x_ref[0] + mod_ref[3 * 1 + 2, 0] * _rms(y, npost_ref[...])


def _rwkv_prompt_kernel(pa_ref, mu_ref, w0_ref, w2_ref, a0_ref, a2_ref, g2_ref, kk_ref, ka_ref,
                        rk3_ref, lnw3_ref, lnb3_ref, ya_ref, s_ref, shift_ref, carry_ref, sbd_ref):
    C = RW_C
    R = RW_ROWS
    N = R * C
    t = pl.program_id(1)

    @pl.when(t == 0)
    def _():
        carry_ref[...] = jnp.zeros_like(carry_ref)
        sbd_ref[...] = jnp.zeros_like(sbd_ref)

    pa = pa_ref[...].reshape(N, A_COLS)
    last = jnp.concatenate([jnp.broadcast_to(carry_ref[s:s + 1, :], (C, A_COLS)) for s in range(R)], axis=0)
    prev = jnp.where((_iota2((N, 1), 0) & (C - 1)) == 0, last, pltpu.roll(pa, 1, 0))
    for s in range(R):
        carry_ref[s:s + 1, :] = pa[s * C + C - 1:s * C + C]
    r, km, v, kkr, a, lw, g = _rwkv_prep(pa, prev, mu_ref[...], w0_ref[...], w2_ref, a0_ref[...],
                                         a2_ref, g2_ref, kk_ref[...], ka_ref[...])

    ti = _iota2((N, N), 0)
    tj = _iota2((N, N), 1)
    csh = C.bit_length() - 1
    tri = jnp.where((ti >= tj) & ((ti >> csh) == (tj >> csh)), 1.0, 0.0).astype(BF16)
    cum = _mm_sel_lhs(tri, lw, 3)
    ebd = _ebd()
    NQ = N_PAIR * R
    tiles = lambda x: _to_tiles(x).reshape(NQ, C, LANES)
    r3, km3, v3, kkr3, a3, lw3, cum3 = (tiles(x) for x in (r, km, v, kkr, a, lw, cum))
    kk3 = kkr3 * lax.rsqrt(_pair_headsum(kkr3 * kkr3, ebd) + 1e-12)
    b3 = kk3 * a3
    cl = cum3[:, C - 1:C, :]
    e_neg = jnp.exp(-cum3)
    e_end = jnp.exp(cl - cum3)
    rh3 = r3 * jnp.exp(cum3)
    ah3 = kk3 * jnp.exp(cum3 - lw3)
    bt3 = b3 * e_neg
    kt3 = km3 * e_neg
    be3 = b3 * e_end
    ke3 = km3 * e_end
    e_last3 = jnp.exp(cl)

    lane0 = _iota2((1, LANES), 1) < HD_A

    def sm(x):
        return jnp.concatenate([jnp.where(lane0, x, 0.0), jnp.where(lane0, 0.0, x)], axis=0)

    ii = _iota2((2 * C, 2 * C), 0)
    jj = _iota2((2 * C, 2 * C), 1)
    strict = jj < ii
    incl = jj <= ii
    masks = _tri_masks(2 * C, C // 2)

    P = range(NQ)
    T2 = 2 * C
    bsm = lambda x: sm(x).astype(BF16)
    ARs = [jnp.concatenate([bsm(ah3[p]), bsm(rh3[p])], axis=0) for p in P]
    BKs = [jnp.concatenate([bsm(bt3[p]), bsm(kt3[p])], axis=0) for p in P]
    BKe = [jnp.concatenate([bsm(be3[p]), bsm(ke3[p])], axis=0) for p in P]
    Vs = [sm(v3[p]) for p in P]
    Vb = [V.astype(BF16) for V in Vs]
    Ss = [sbd_ref[p] for p in P]
    QQ = [_dg(ARs[p], BKs[p], _NT) for p in P]
    QS = [_dg(ARs[p], Ss[p].astype(BF16), _NT) for p in P]
    Lab = [jnp.where(strict, Q[0:T2, 0:T2], 0.0) for Q in QQ]
    Lak = [jnp.where(strict, Q[0:T2, T2:2 * T2], 0.0).astype(BF16) for Q in QQ]
    Ms = [QS[p][0:T2] + _dg(Lak[p], Vb[p]) for p in P]
    Tinv = _tri_inv(Lab, masks)
    Us = [-_dg(Tinv[p].astype(BF16), Ms[p].astype(BF16)) for p in P]
    UVb = [jnp.concatenate([Us[p].astype(BF16), Vb[p]], axis=0) for p in P]
    incl2 = jnp.concatenate([incl, incl], axis=1)
    Arbk = [jnp.where(incl2, Q[T2:2 * T2, :], 0.0).astype(BF16) for Q in QQ]
    Ys = [QS[p][T2:2 * T2] + _dg(Arbk[p], UVb[p]) for p in P]
    for p in P:
        UVt = jnp.concatenate([Us[p].T, Vs[p].T], axis=1).astype(BF16)
        sbd_ref[p] = Ss[p] * e_last3[p] + _dg(UVt, BKe[p])
    y3 = jnp.stack([Y[0:C] + Y[C:2 * C] for Y in Ys], axis=0)

    out3 = _rwkv_epilogue(y3, r3, km3, v3, rk3_ref[...], lnw3_ref[...], lnb3_ref[...], ebd)
    for q in P:
        p, s = divmod(q, R)
        ya_ref[s, :, LANES * p:LANES * (p + 1)] = (
            out3[q] * g[s * C:(s + 1) * C, LANES * p:LANES * (p + 1)]).astype(ya_ref.dtype)

    @pl.when(t == pl.num_programs(1) - 1)
    def _():
        for s in range(R):
            shift_ref[s] = pa[s * C + C - 1:s * C + C]
        for q in P:
            p, s = divmod(q, R)
            S = sbd_ref[q]
            s_ref[s, 2 * p] = S[0:HD_A, 0:HD_A]
            s_ref[s, 2 * p + 1] = pltpu.roll(S, HD_A, 1)[HD_A:LANES, 0:HD_A]


def _gdn_prompt_kernel(qkv_ref, ab_ref, z_ref, cw_ref, alog_ref, dtb_ref, nw_ref,
                       yb_ref, conv_ref, ssm_ref, hist_ref):
    C = GD_C
    t = pl.program_id(1)

    @pl.when(t == 0)
    def _():
        hist_ref[0:8, :] = jnp.zeros((8, 3 * D_MODEL), F32)
        ssm_ref[...] = jnp.zeros_like(ssm_ref)

    x = qkv_ref[0]
    qkvc = _causal_conv_silu(x, hist_ref, cw_ref)
    qs = _head_l2norm(qkvc[:, 0:D_MODEL], 1e-6)
    ks = _head_l2norm(qkvc[:, D_MODEL:2 * D_MODEL], 1e-6)
    vv = qkvc[:, 2 * D_MODEL:3 * D_MODEL]

    g_all, beta_all = _gdn_gates(ab_ref[0], alog_ref[...], dtb_ref[...])
    tri = jnp.where(_iota2((C, C), 0) >= _iota2((C, C), 1), 1.0, 0.0).astype(BF16)
    gc_all = _mm_sel_lhs(tri, g_all, 3)

    ii = _iota2((C, C), 0)
    jj = _iota2((C, C), 1)
    strict = jj < ii
    incl = jj <= ii
    masks = _tri_masks(C, C // 2)
    z = z_ref[0].astype(F32)

    H = range(H_B)
    sls = [slice(LANES * h, LANES * (h + 1)) for h in H]
    qh = [qs[h] * (HD_B ** -0.5) for h in H]
    bcast = lambda col: jnp.broadcast_to(col, (C, LANES))
    gh = [bcast(gc_all[:, h:h + 1]) for h in H]
    bh = [bcast(beta_all[:, H_B + h:H_B + h + 1]) for h in H]
    gct = gc_all.T
    dec = [jnp.exp(gh[h] - gct[h:h + 1, :]) for h in H]
    eg = [jnp.exp(g) for g in gh]
    glast = [g[C - 1:C, :] for g in gh]
    kb = [ks[h] * bh[h] for h in H]
    KQ = [_dg(jnp.concatenate([kb[h], qh[h]], axis=0).astype(BF16), ks[h].astype(BF16), _NT) for h in H]
    Ls = [jnp.where(strict, KQ[h][0:C] * dec[h], 0.0) for h in H]
    attn = [jnp.where(incl, KQ[h][C:2 * C] * dec[h], 0.0) for h in H]
    Tinv = _tri_inv(Ls, masks)
    rhs = [jnp.concatenate([vv[:, sls[h]] * bh[h], kb[h] * eg[h]], axis=1).astype(BF16) for h in H]
    UW = [_dg(Tinv[h].astype(BF16), rhs[h]) for h in H]
    Ss = [ssm_ref[0, h] for h in H]
    Sb = [S.astype(BF16) for S in Ss]
    v_new = [UW[h][:, 0:LANES] - _dg(UW[h][:, LANES:2 * LANES].astype(BF16), Sb[h]) for h in H]
    vnb = [v.astype(BF16) for v in v_new]
    os_ = [_dg(jnp.concatenate([qh[h] * eg[h], attn[h]], axis=1).astype(BF16),
               jnp.concatenate([Sb[h], vnb[h]], axis=0)) for h in H]
    for h in H:
        k_dec = ks[h] * jnp.exp(glast[h] - gh[h])
        ssm_ref[0, h] = Ss[h] * jnp.exp(glast[h]) + _dg(k_dec.T.astype(BF16), vnb[h])
    for h in H:
        o = os_[h]
        on = o * lax.rsqrt(jnp.mean(o * o, axis=-1, keepdims=True) + RMS_EPS) * nw_ref[...]
        yb_ref[0, :, sls[h]] = (on * _silu(z[:, sls[h]])).astype(yb_ref.dtype)

    @pl.when(t == pl.num_programs(1) - 1)
    def _():
        conv_ref[0] = x[C - (CONV_W - 1):C]


def _rwkv_sample_prep_kernel(pa_ref, prev_ref, mu_ref, w0_ref, w2_ref, a0_ref, a2_ref, g2_ref, kk_ref,
                             ka_ref, r_ref, w_ref, km_ref, v_ref, kkn_ref, b_ref, g_ref):
    r, km, v, kkr, a, lw, g = _rwkv_prep(pa_ref[...], prev_ref[...], mu_ref[...], w0_ref[...], w2_ref,
                                         a0_ref[...], a2_ref, g2_ref, kk_ref[...], ka_ref[...])
    n = r.shape[0]
    heads = lambda x: x.reshape(H_A, HD_A, n)
    kk3 = heads(kkr.T)
    kkn = kk3 * lax.rsqrt(jnp.sum(kk3 * kk3, axis=1, keepdims=True) + 1e-12)
    r_ref[...] = r.T
    w_ref[...] = jnp.exp(lw).T
    km_ref[...] = km.T
    v_ref[...] = v.T
    kkn_ref[...] = kkn.reshape(D_MODEL, n)
    b_ref[...] = (kkn * heads(a.T)).reshape(D_MODEL, n)
    g_ref[...] = g.T


def _rwkv_sample_state_kernel(r_ref, w_ref, km_ref, kkn_ref, b_ref, v_ref, s0_ref, stacked_ref, s_ref, y_ref):
    del stacked_ref
    rh, wh, kh, kkh, bh = r_ref[...], w_ref[...], km_ref[...], kkn_ref[...], b_ref[...]
    G = 8
    row = _iota2((G, 1), 0)

    def body(j, carry):
        base = pl.multiple_of(j * G, G)
        vrows = v_ref[pl.ds(base, G), :]
        U = range(G)
        Ss = [s0_ref[base + u] for u in U]
        sas = [-jnp.sum(Ss[u] * kkh, axis=0, keepdims=True) for u in U]
        Sn = [Ss[u] * wh + sas[u] * bh + vrows[u:u + 1, :] * kh for u in U]
        for u in U:
            s_ref[base + u] = Sn[u]
        yrows = [jnp.sum(Sn[u] * rh, axis=0, keepdims=True) for u in U]
        y = jnp.zeros_like(vrows)
        for u in U:
            y = jnp.where(row == u, yrows[u], y)
        y_ref[pl.ds(base, G), :] = y
        return carry

    lax.fori_loop(0, HD_A // G, body, 0)


def _rwkv_sample_post_kernel(y_ref, r_ref, km_ref, v_ref, g_ref, rk_ref, lnw_ref, lnb_ref, ya_ref):
    n = y_ref.shape[1]
    heads = lambda x: x.reshape(H_A, HD_A, n)
    y3 = heads(y_ref[...])
    yc = y3 - jnp.mean(y3, axis=1, keepdims=True)
    var = jnp.mean(yc * yc, axis=1, keepdims=True)
    yn = (yc * lax.rsqrt(var + RWKV_GN_EPS)).reshape(D_MODEL, n) * lnw_ref[...] + lnb_ref[...]
    bonus = jnp.sum(heads(r_ref[...] * km_ref[...] * rk_ref[...]), axis=1, keepdims=True) * heads(v_ref[...])
    out = (yn + bonus.reshape(D_MODEL, n)) * g_ref[...]
    ya_ref[...] = out.T


def _gdn_sample_prep_kernel(qkv_ref, c0_ref, ab_ref, cw_ref, alog_ref, dtb_ref,
                            q_ref, k_ref, v_ref, eg_ref, beta_ref, cnew_ref):
    x = qkv_ref[...]
    conv = x * cw_ref[CONV_W - 1:CONV_W, :]
    for j in range(CONV_W - 1):
        conv = conv + c0_ref[j] * cw_ref[j:j + 1, :]
    for j in range(CONV_W - 2):
        cnew_ref[j] = c0_ref[j + 1]
    cnew_ref[CONV_W - 2] = x
    qkvc = _silu(conv)
    qs = _head_l2norm(qkvc[:, 0:D_MODEL], 1e-6)
    ks = _head_l2norm(qkvc[:, D_MODEL:2 * D_MODEL], 1e-6)
    for h in range(H_B):
        q_ref[:, LANES * h:LANES * (h + 1)] = qs[h] * (HD_B ** -0.5)
        k_ref[:, LANES * h:LANES * (h + 1)] = ks[h]
    v_ref[...] = qkvc[:, 2 * D_MODEL:3 * D_MODEL]
    g_all, beta_all = _gdn_gates(ab_ref[...], alog_ref[...], dtb_ref[...])
    eg_ref[...] = jnp.exp(_mm_sel_rhs(g_all, _sel_matrix(0), 3))
    beta_ref[...] = _mm_sel_rhs(beta_all, _sel_matrix(H_B), 3)


def _gdn_sample_state_kernel(qt_ref, kt_ref, v_ref, eg_ref, beta_ref, s0_ref, stacked_ref, s_ref, o_ref, *, bb):
    del stacked_ref

    def body(i, carry):
        qt, kt = qt_ref[i], kt_ref[i]
        vb, egb, btb = v_ref[i], eg_ref[i], beta_ref[i]
        H = range(H_B)
        kc = [jnp.broadcast_to(kt[:, h:h + 1], (HD_B, HD_B)) for h in H]
        Sd = [s0_ref[i, h] * egb[h:h + 1, :] for h in H]
        ksr = [jnp.sum(Sd[h] * kc[h], axis=0, keepdims=True) for h in H]
        v_new = [btb[h:h + 1, :] * (vb[h:h + 1, :] - ksr[h]) for h in H]
        Sn = [Sd[h] + kc[h] * v_new[h] for h in H]
        for h in H:
            s_ref[i, h] = Sn[h]
        for h in H:
            o_ref[i, h:h + 1, :] = jnp.sum(Sn[h] * qt[:, h:h + 1], axis=0, keepdims=True)
        return carry

    lax.fori_loop(0, bb, body, 0, unroll=2)


def _gdn_sample_post_kernel(o_ref, z_ref, nw_ref, yb_ref):
    o = o_ref[...]
    z = z_ref[...].astype(F32)
    for h in range(H_B):
        sl = slice(LANES * h, LANES * (h + 1))
        oh = o[:, sl]
        on = oh * lax.rsqrt(jnp.mean(oh * oh, axis=-1, keepdims=True) + RMS_EPS) * nw_ref[...]
        yb_ref[:, sl] = on * _silu(z[:, sl])


def _params(sem):
    return pltpu.CompilerParams(dimension_semantics=sem, vmem_limit_bytes=VMEM_LIMIT)


def _full(shape):
    return pl.BlockSpec(shape, lambda *_: (0,) * len(shape))


def _weight_spec(w, idx):
    lead = len(idx)
    return pl.BlockSpec((None,) * lead + tuple(w.shape[lead:]), lambda *_: tuple(idx) + (0, 0),
                        pipeline_mode=pl.Buffered(1))


def _row_tile(t, cap):
    tm = min(t, cap)
    assert t % tm == 0
    return tm


def _mod_spec(mod):
    rows = mod.shape[2]
    if rows == 1:
        return pl.BlockSpec((3 * N_SUB, 1, 1, D_MODEL), lambda b, i: (0, b, 0, 0))
    return pl.BlockSpec((3 * N_SUB, 1, rows, D_MODEL), lambda b, i: (0, 0, i, 0))


def _ada_call(c_all, w_ada, b_ada):
    depth = w_ada.shape[0]
    n = c_all.shape[0]
    return pl.pallas_call(
        _ada_kernel,
        grid=(depth, 3 * N_SUB),
        in_specs=[pl.BlockSpec((n, D_MODEL), lambda l, j: (0, 0)),
                  pl.BlockSpec((None, D_MODEL, D_MODEL), lambda l, j: (l, 0, j)),
                  pl.BlockSpec((None, 1, D_MODEL), lambda l, j: (l, 0, j))],
        out_specs=pl.BlockSpec((None, None, n, D_MODEL), lambda l, j: (l, j, 0, 0)),
        out_shape=jax.ShapeDtypeStruct((depth, 3 * N_SUB, n, D_MODEL), F32),
        compiler_params=_params(("arbitrary", "arbitrary")),
        name="ada_mod",
    )(c_all, w_ada, b_ada.reshape(depth, 1, 3 * N_SUB * D_MODEL))


def _ffn_call(x, mod, npre, npost, up, down, widx, sub):
    b, t, _ = x.shape
    tm = _row_tile(t, 512)
    xspec = pl.BlockSpec((1, tm, D_MODEL), lambda b, i: (b, i, 0))
    return pl.pallas_call(
        functools.partial(_ffn_kernel, sub=sub),
        grid=(b, t // tm),
        in_specs=[xspec, _mod_spec(mod), _full((1, D_MODEL)), _full((1, D_MODEL)),
                  _weight_spec(up, widx), _weight_spec(down, widx)],
        out_specs=xspec,
        out_shape=jax.ShapeDtypeStruct(x.shape, F32),
        compiler_params=_params(("arbitrary", "arbitrary")),
        name="ffn",
    )(x, mod, npre, npost, up, down)


def _inproj_call(x, mod, npre, w, widx):
    b, t, _ = x.shape
    tm = _row_tile(t, 256)
    widths = (A_COLS, 3 * D_MODEL, D_MODEL, 2 * D_MODEL, AB_COLS)
    dtypes = (F32, F32, BF16, BF16, F32)
    spec = lambda n: pl.BlockSpec((1, tm, n), lambda b, i: (b, i, 0))
    return pl.pallas_call(
        _inproj_kernel,
        grid=(b, t // tm),
        in_specs=[spec(D_MODEL), _mod_spec(mod), _full((1, D_MODEL)), _weight_spec(w, widx)],
        out_specs=[spec(n) for n in widths],
        out_shape=[jax.ShapeDtypeStruct((b, t, n), d) for n, d in zip(widths, dtypes)],
        compiler_params=_params(("arbitrary", "arbitrary")),
        name="in_proj",
    )(x, mod, npre, w)


def _outproj_call(x, ya, yb, pg, mod, npost, w, widx):
    b, t, _ = x.shape
    tm = _row_tile(t, 512)
    spec = lambda n: pl.BlockSpec((1, tm, n), lambda b, i: (b, i, 0))
    return pl.pallas_call(
        _outproj_kernel,
        grid=(b, t // tm),
        in_specs=[spec(D_MODEL), spec(D_MODEL), spec(D_MODEL), spec(2 * D_MODEL), _mod_spec(mod),
                  _full((1, D_MODEL)), _weight_spec(w, widx)],
        out_specs=spec(D_MODEL),
        out_shape=jax.ShapeDtypeStruct(x.shape, F32),
        compiler_params=_params(("arbitrary", "arbitrary")),
        name="out_proj",
    )(x, ya, yb, pg, mod, npost, w)


def _rwkv_prompt_call(pa, rp):
    b, t, _ = pa.shape
    C = RW_C
    R = RW_ROWS
    assert t % C == 0 and b % R == 0
    vec = _full((1, D_MODEL))
    lora = _full((2, LANES, D_MODEL))
    tile3 = _full((N_PAIR * R, 1, LANES))
    per_tile = lambda x: jnp.repeat(x, R, axis=0)
    return pl.pallas_call(
        _rwkv_prompt_kernel,
        grid=(b // R, t // C),
        in_specs=[pl.BlockSpec((R, C, A_COLS), lambda b, i: (b, i, 0)), _full((1, A_COLS)),
                  vec, lora, vec, lora, lora, vec, vec, tile3, tile3, tile3],
        out_specs=[pl.BlockSpec((R, C, D_MODEL), lambda b, i: (b, i, 0)),
                   pl.BlockSpec((R, H_A, HD_A, HD_A), lambda b, i: (b, 0, 0, 0)),
                   pl.BlockSpec((R, 1, A_COLS), lambda b, i: (b, 0, 0))],
        out_shape=[jax.ShapeDtypeStruct((b, t, D_MODEL), BF16),
                   jax.ShapeDtypeStruct((b, H_A, HD_A, HD_A), F32),
                   jax.ShapeDtypeStruct((b, 1, A_COLS), F32)],
        scratch_shapes=[pltpu.VMEM((R, A_COLS), F32), pltpu.VMEM((N_PAIR * R, LANES, LANES), F32)],
        compiler_params=_params(("arbitrary", "arbitrary")),
        name="rwkv_prompt",
    )(pa, rp["mu"], rp["w0"], rp["w2p"], rp["a0"], rp["a2p"], rp["g2"], rp["k_k"], rp["k_a"],
      per_tile(rp["rk3"]), per_tile(rp["lnw3"]), per_tile(rp["lnb3"]))


def _gdn_prompt_call(qkv, ab, z, gp):
    b, t, _ = qkv.shape
    C = GD_C
    assert t % C == 0
    spec = lambda n: pl.BlockSpec((1, C, n), lambda b, i: (b, i, 0))
    return pl.pallas_call(
        _gdn_prompt_kernel,
        grid=(b, t // C),
        in_specs=[spec(3 * D_MODEL), spec(AB_COLS), spec(D_MODEL), _full((CONV_W, 3 * D_MODEL)),
                  _full((1, LANES)), _full((1, LANES)), _full((1, HD_B))],
        out_specs=[spec(D_MODEL),
                   pl.BlockSpec((1, CONV_W - 1, 3 * D_MODEL), lambda b, i: (b, 0, 0)),
                   pl.BlockSpec((1, H_B, HD_B, HD_B), lambda b, i: (b, 0, 0, 0))],
        out_shape=[jax.ShapeDtypeStruct((b, t, D_MODEL), BF16),
                   jax.ShapeDtypeStruct((b, CONV_W - 1, 3 * D_MODEL), F32),
                   jax.ShapeDtypeStruct((b, H_B, HD_B, HD_B), F32)],
        scratch_shapes=[pltpu.VMEM((8 + C, 3 * D_MODEL), F32)],
        compiler_params=_params(("arbitrary", "arbitrary")),
        name="gdn_prompt",
    )(qkv, ab, z, gp["conv_w"], gp["alog"], gp["dtb"], gp["norm_w"])


def _rwkv_sample(pa, shift0, wkv_all, wkv_new, l, rp):
    n = pa.shape[0]
    vec = _full((1, D_MODEL))
    lora = _full((2, LANES, D_MODEL))
    chan = _full((D_MODEL, n))
    r, w, km, v, kkn, bb_, g = pl.pallas_call(
        _rwkv_sample_prep_kernel,
        grid=(1,),
        in_specs=[_full((n, A_COLS)), _full((n, A_COLS)), _full((1, A_COLS)), vec, lora, vec, lora, lora,
                  vec, vec],
        out_specs=[chan] * 7,
        out_shape=[jax.ShapeDtypeStruct((D_MODEL, n), F32)] * 7,
        compiler_params=_params(("arbitrary",)),
        name="rwkv_sample_prep",
    )(pa, shift0, rp["mu"], rp["w0"], rp["w2p"], rp["a0"], rp["a2p"], rp["g2"], rp["k_k"], rp["k_a"])

    hspec = pl.BlockSpec((HD_A, n), lambda h: (h, 0))
    sspec = pl.BlockSpec((None, None, HD_A, HD_A, n), lambda h: (l, h, 0, 0, 0))
    s_new, y = pl.pallas_call(
        _rwkv_sample_state_kernel,
        grid=(H_A,),
        in_specs=[hspec] * 6 + [sspec, pl.BlockSpec(memory_space=pl.ANY)],
        out_specs=[sspec, hspec],
        out_shape=[jax.ShapeDtypeStruct(wkv_all.shape, F32), jax.ShapeDtypeStruct((D_MODEL, n), F32)],
        input_output_aliases={7: 0},
        compiler_params=_params(("arbitrary",)),
        name="rwkv_sample_state",
    )(r, w, km, kkn, bb_, v, wkv_all, wkv_new)

    col = _full((D_MODEL, 1))
    ya = pl.pallas_call(
        _rwkv_sample_post_kernel,
        grid=(1,),
        in_specs=[chan] * 5 + [col] * 3,
        out_specs=_full((n, D_MODEL)),
        out_shape=jax.ShapeDtypeStruct((n, D_MODEL), F32),
        compiler_params=_params(("arbitrary",)),
        name="rwkv_sample_post",
    )(y, r, km, v, g, rp["rk_col"], rp["lnw_col"], rp["lnb_col"])
    return ya, s_new


def _gdn_sample(qkv, conv0, ssm_all, ssm_new, l, ab, z, gp):
    n = qkv.shape[0]
    flat = _full((n, D_MODEL))
    cspec = _full((CONV_W - 1, n, 3 * D_MODEL))
    q, k, v, eg, beta, cnew = pl.pallas_call(
        _gdn_sample_prep_kernel,
        grid=(1,),
        in_specs=[_full((n, 3 * D_MODEL)), cspec, _full((n, AB_COLS)), _full((CONV_W, 3 * D_MODEL)),
                  _full((1, LANES)), _full((1, LANES))],
        out_specs=[flat] * 5 + [cspec],
        out_shape=[jax.ShapeDtypeStruct((n, D_MODEL), F32)] * 5
        + [jax.ShapeDtypeStruct((CONV_W - 1, n, 3 * D_MODEL), F32)],
        compiler_params=_params(("arbitrary",)),
        name="gdn_sample_prep",
    )(qkv, jnp.swapaxes(conv0, 0, 1), ab, gp["conv_w"], gp["alog"], gp["dtb"])

    bb = 4
    assert n % bb == 0
    heads = lambda x: x.reshape(n, H_B, HD_B)
    hspec = pl.BlockSpec((bb, H_B, HD_B), lambda i: (i, 0, 0))
    tspec = pl.BlockSpec((bb, HD_B, H_B), lambda i: (i, 0, 0))
    sspec = pl.BlockSpec((None, bb, H_B, HD_B, HD_B), lambda i: (l, i, 0, 0, 0))
    s_new, o = pl.pallas_call(
        functools.partial(_gdn_sample_state_kernel, bb=bb),
        grid=(n // bb,),
        in_specs=[tspec, tspec, hspec, hspec, hspec, sspec, pl.BlockSpec(memory_space=pl.ANY)],
        out_specs=[sspec, hspec],
        out_shape=[jax.ShapeDtypeStruct(ssm_all.shape, F32), jax.ShapeDtypeStruct((n, H_B, HD_B), F32)],
        input_output_aliases={6: 0},
        compiler_params=_params(("arbitrary",)),
        name="gdn_sample_state",
    )(jnp.swapaxes(heads(q), 1, 2), jnp.swapaxes(heads(k), 1, 2), heads(v), heads(eg), heads(beta), ssm_all,
      ssm_new)

    yb = pl.pallas_call(
        _gdn_sample_post_kernel,
        grid=(1,),
        in_specs=[flat, flat, _full((1, HD_B))],
        out_specs=flat,
        out_shape=jax.ShapeDtypeStruct((n, D_MODEL), F32),
        compiler_params=_params(("arbitrary",)),
        name="gdn_sample_post",
    )(o.reshape(n, D_MODEL), z, gp["norm_w"])
    return yb, jnp.swapaxes(cnew, 0, 1), s_new


def _hi_lo(w):
    hi = w.astype(BF16)
    return jnp.stack([hi, (w - hi.astype(F32)).astype(BF16)])


def _pad_rows(x, rows, at):
    out = jnp.zeros((rows, x.shape[1]), x.dtype)
    return lax.dynamic_update_slice(out, x, (at, 0))


def kernel(x_prompt, x_sample, c_prompt, c_sample, state_rwkv_shift, state_rwkv_wkv, state_gdn_conv, state_gdn_ssm, w_ada, b_ada, norm_pre, norm_post, ffn_up, ffn_down, w_in, w_out, rwkv_mu, rwkv_w0, rwkv_w2, rwkv_a0, rwkv_a2, rwkv_g2, rwkv_k_k, rwkv_k_a, rwkv_r_k, rwkv_ln_w, rwkv_ln_b, gdn_conv, gdn_a_log, gdn_dt_bias, gdn_norm_w):
    depth = w_ada.shape[0]
    bp = x_prompt.shape[0]
    ns = x_sample.shape[0]

    n_c = bp + ns
    n_cp = -(-n_c // 16) * 16
    c_all = jnp.concatenate([c_prompt, c_sample, jnp.zeros((n_cp - n_c, D_MODEL), F32)], axis=0)
    mod = _ada_call(c_all, w_ada, b_ada)

    w_in_t = jnp.transpose(w_in, (0, 2, 1)).astype(BF16)
    up_b = ffn_up.astype(BF16)
    down_b = ffn_down.astype(BF16)
    w_out_b = w_out.astype(BF16)

    wkv_t = jnp.transpose(state_rwkv_wkv, (0, 2, 3, 4, 1))

    yp = x_prompt
    ys = x_sample.reshape(1, ns, D_MODEL)
    outs = [[] for _ in range(6)]
    s_wkv = jnp.zeros(wkv_t.shape, F32)
    s_ssm = jnp.zeros(state_gdn_ssm.shape, F32)
    for l in range(depth):
        mod_p = mod[l, :, 0:bp].reshape(3 * N_SUB, bp, 1, D_MODEL)
        mod_s = mod[l, :, bp:bp + ns].reshape(3 * N_SUB, 1, ns, D_MODEL)
        npre = [norm_pre[l, i][None] for i in range(N_SUB)]
        npost = [norm_post[l, i][None] for i in range(N_SUB)]
        row = lambda x: x[l][None]
        tile3 = lambda x: x[l].reshape(N_PAIR, 1, LANES)
        rp = dict(mu=row(rwkv_mu), w0=row(rwkv_w0), a0=row(rwkv_a0), k_k=row(rwkv_k_k), k_a=row(rwkv_k_a),
                  w2p=_hi_lo(_pad_rows(rwkv_w2[l], LANES, 0)), a2p=_hi_lo(_pad_rows(rwkv_a2[l], LANES, W_LORA)),
                  g2=_hi_lo(rwkv_g2[l]), rk3=tile3(rwkv_r_k), lnw3=tile3(rwkv_ln_w), lnb3=tile3(rwkv_ln_b),
                  rk_col=rwkv_r_k[l].reshape(D_MODEL, 1), lnw_col=rwkv_ln_w[l].reshape(D_MODEL, 1),
                  lnb_col=rwkv_ln_b[l].reshape(D_MODEL, 1))
        lane_row = lambda x: jnp.zeros((1, LANES), F32).at[0, 0:H_B].set(x[l])
        gp = dict(conv_w=gdn_conv[l], norm_w=row(gdn_norm_w), alog=lane_row(gdn_a_log),
                  dtb=lane_row(gdn_dt_bias))

        yp = _ffn_call(yp, mod_p, npre[0], npost[0], up_b, down_b, (l, 0), 0)
        pa, qkv, z, pg, ab = _inproj_call(yp, mod_p, npre[1], w_in_t, (l,))
        ya, p_wkv, p_shift = _rwkv_prompt_call(pa, rp)
        yb, p_conv, p_ssm = _gdn_prompt_call(qkv, ab, z, gp)
        yp = _outproj_call(yp, ya, yb, pg, mod_p, npost[1], w_out_b, (l,))
        yp = _ffn_call(yp, mod_p, npre[2], npost[2], up_b, down_b, (l, 1), 2)

        ys = _ffn_call(ys, mod_s, npre[0], npost[0], up_b, down_b, (l, 0), 0)
        pa, qkv, z, pg, ab = _inproj_call(ys, mod_s, npre[1], w_in_t, (l,))
        ya, s_wkv = _rwkv_sample(pa[0], state_rwkv_shift[l], wkv_t, s_wkv, l, rp)
        yb, s_conv, s_ssm = _gdn_sample(qkv[0], state_gdn_conv[l], state_gdn_ssm, s_ssm, l, ab[0], z[0], gp)
        ys = _outproj_call(ys, ya[None], yb[None], pg, mod_s, npost[1], w_out_b, (l,))
        ys = _ffn_call(ys, mod_s, npre[2], npost[2], up_b, down_b, (l, 1), 2)

        for lst, val in zip(outs, (p_shift[:, 0], p_wkv, p_conv, p_ssm, pa[0], s_conv)):
            lst.append(val)

    st = [jnp.stack(o) for o in outs]
    return (yp, ys.reshape(ns, 1, D_MODEL), st[0], st[1], st[2], st[3], st[4],
            jnp.transpose(s_wkv, (0, 4, 1, 2, 3)), st[5], s_ssm)
```
